```python
import math
import jax, jax.numpy as jnp
from jax import lax
import numpy as np

D_MODEL = 2048
BATCH = 16
SEQ = 256
DEPTH = 2
DEC_BATCH = 4
DEC_SEQ = 1024
PAST_LEN = 512

GRID_W = 64
N_MIXERS = 2
N_HYENA = (DEPTH + 1) // 2
N_RET = DEPTH // 2
D_FF = 4 * D_MODEL
EPS = 1e-6
HYENA_ORDER = 2
FILTER_BANDS = 16
FILTER_EMB = 1 + 2 * FILTER_BANDS
FILTER_WIDTH = 64
DECAY_FAST_PCT = 0.3
DECAY_SLOW_PCT = 1.5
DECAY_TARGET = 1e-2
MIN_DECAY = math.log(DECAY_TARGET) / DECAY_SLOW_PCT
MAX_DECAY = math.log(DECAY_TARGET) / DECAY_FAST_PCT
RET_HEADS = 8
RET_DK = D_MODEL // RET_HEADS
RET_DV = 2 * D_MODEL // RET_HEADS
RET_CHUNK = 128
ROPE_BASE = 10000.0

kernel_name = "hyena_retnet_prefix_dit_step"


def _rms_norm(x, g):
    xf = x.astype(jnp.float32)
    y = xf * lax.rsqrt(jnp.mean(xf * xf, axis=-1, keepdims=True) + EPS)
    return (y * g.astype(jnp.float32)).astype(x.dtype)


def _short_conv(x, w, b):
    xp = jnp.pad(x, ((0, 0), (1, 1), (0, 0)))
    return xp[:, :-2] * w[0] + xp[:, 1:-1] * w[1] + xp[:, 2:] * w[2] + b


def _hyena_filters(L, w1, b1, w2, b2, w3, b3, freq, w_out):
    f32 = jnp.float32
    pos = jnp.arange(L, dtype=f32)
    t = pos / L
    omega = 2.0 * math.pi * pos / L
    bands = jnp.linspace(1e-4, FILTER_BANDS - 1, FILTER_BANDS, dtype=f32)
    ang = omega[:, None] * bands[None, :]
    feat = jnp.concatenate([t[:, None], jnp.cos(ang), -jnp.sin(ang)], axis=-1)
    freq = freq.astype(f32)
    h = jnp.sin(freq[0] * (feat @ w1.astype(f32) + b1.astype(f32)))
    h = jnp.sin(freq[1] * (h @ w2.astype(f32) + b2.astype(f32)))
    h = jnp.sin(freq[2] * (h @ w3.astype(f32) + b3.astype(f32)))
    h = h @ w_out.astype(f32)
    deltas = jnp.abs(jnp.linspace(MIN_DECAY, MAX_DECAY, D_MODEL, dtype=f32))
    window = jnp.exp(-t[:, None] * deltas[None, :])
    return h.reshape(L, HYENA_ORDER, 2, D_MODEL) * window[:, None, None, :]


def _long_conv(u, hf, hb, skip):
    L = u.shape[1]
    n = 2 * L
    k = jnp.concatenate([hf, jnp.zeros((1, hf.shape[1]), hf.dtype), hb[:0:-1]], axis=0)
    y = jnp.fft.irfft(jnp.fft.rfft(u, n=n, axis=1) * jnp.fft.rfft(k, n=n, axis=0)[None], n=n, axis=1)[:, :L]
    return y + u * skip.astype(jnp.float32)


def _hyena(h, w_in, b_in, conv_w, conv_b, f_w1, f_b1, f_w2, f_b2, f_w3, f_b3, f_freq, f_wout, f_skip, w_out, b_out):
    L = h.shape[1]
    proj = _short_conv(h @ w_in + b_in, conv_w, conv_b).astype(jnp.float32)
    v, x1, x2 = jnp.split(proj, 3, axis=-1)
    filt = _hyena_filters(L, f_w1, f_b1, f_w2, f_b2, f_w3, f_b3, f_freq, f_wout)
    z = x1 * _long_conv(v, filt[:, 0, 0], filt[:, 0, 1], f_skip[0])
    z = x2 * _long_conv(z, filt[:, 1, 0], filt[:, 1, 1], f_skip[1])
    return z.astype(h.dtype) @ w_out + b_out


def _axial_rotary(x):
    f32 = jnp.float32
    L = x.shape[1]
    n_rows = L // GRID_W
    rows = jnp.repeat(jnp.arange(n_rows, dtype=f32), GRID_W)
    cols = jnp.tile(jnp.arange(GRID_W, dtype=f32), n_rows)
    n_pairs = RET_DK // 4
    inv = ROPE_BASE ** (-jnp.arange(n_pairs, dtype=f32) / n_pairs)
    ang = jnp.concatenate([rows[:, None] * inv, cols[:, None] * inv], axis=-1)
    cos = jnp.cos(ang)[None, :, None, :]
    sin = jnp.sin(ang)[None, :, None, :]
    xp = x.reshape(*x.shape[:-1], RET_DK // 2, 2)
    x0, x1 = xp[..., 0], xp[..., 1]
    return jnp.stack([x0 * cos - x1 * sin, x0 * sin + x1 * cos], axis=-1).reshape(x.shape)


def _retention_scan(q, k, v, log_g, s0):
    B, H, L, _ = q.shape
    C = RET_CHUNK
    n = L // C
    idx = jnp.arange(C, dtype=jnp.float32)
    diff = idx[:, None] - idx[None, :]
    lower = diff >= 0
    intra = jnp.where(lower, jnp.exp(jnp.where(lower, diff, 0.0)[None] * log_g[:, None, None]), 0.0)
    q_decay = jnp.exp((idx[None, :] + 1.0) * log_g[:, None])[..., None]
    k_decay = jnp.exp((C - 1.0 - idx[None, :]) * log_g[:, None])[..., None]
    chunk_decay = jnp.exp(C * log_g)[:, None, None]
    qs = jnp.moveaxis(q.reshape(B, H, n, C, RET_DK), 2, 0)
    ks = jnp.moveaxis(k.reshape(B, H, n, C, RET_DK), 2, 0)
    vs = jnp.moveaxis(v.reshape(B, H, n, C, RET_DV), 2, 0)

    def body(s, qkv):
        qc, kc, vc = qkv
        scores = jnp.einsum('bhid,bhjd->bhij', qc, kc) * intra
        out = jnp.einsum('bhij,bhjv->bhiv', scores, vc) + jnp.einsum('bhid,bhdv->bhiv', qc * q_decay, s)
        s_new = s * chunk_decay + jnp.einsum('bhjd,bhjv->bhdv', kc * k_decay, vc)
        return s_new, out

    s_final, outs = lax.scan(body, s0, (qs, ks, vs))
    return jnp.moveaxis(outs, 0, 2).reshape(B, H, L, RET_DV), s_final


def _retention(h, s0, w_qkvg, decay_logit, gn_g, w_o, grid_positions):
    f32 = jnp.float32
    B, L, _ = h.shape
    q, k, v, g = jnp.split(h @ w_qkvg, [D_MODEL, 2 * D_MODEL, 4 * D_MODEL], axis=-1)
    q = q.reshape(B, L, RET_HEADS, RET_DK).astype(f32)
    k = k.reshape(B, L, RET_HEADS, RET_DK).astype(f32)
    if grid_positions:
        q = _axial_rotary(q)
        k = _axial_rotary(k)
    k = k * RET_DK ** -0.5
    v = v.reshape(B, L, RET_HEADS, RET_DV).astype(f32)
    q, k, v = (jnp.swapaxes(a, 1, 2) for a in (q, k, v))
    log_g = jax.nn.log_sigmoid(decay_logit.astype(f32))
    s0 = s0.astype(f32)
    y_f, s_f = _retention_scan(q, k, v, log_g[0], s0[:, 0])
    y_b, s_b = _retention_scan(jnp.flip(q, 2), jnp.flip(k, 2), jnp.flip(v, 2), log_g[1], s0[:, 1])
    y = jnp.swapaxes(y_f + jnp.flip(y_b, 2), 1, 2)
    mu = jnp.mean(y, axis=-1, keepdims=True)
    var = jnp.mean(jnp.square(y - mu), axis=-1, keepdims=True)
    y = ((y - mu) * lax.rsqrt(var + EPS)).reshape(B, L, 2 * D_MODEL) * gn_g.astype(f32)
    out = (jax.nn.silu(g.astype(f32)) * y).astype(h.dtype) @ w_o
    return out, jnp.stack([s_f, s_b], axis=1)


def _sq_relu_mlp(h, w1, w2):
    return jnp.square(jax.nn.relu(h @ w1)) @ w2


def setup_inputs(seed: int = 0) -> dict:
    key = jax.random.key(seed)
    ks = jax.random.split(key, 32)
    f32 = jnp.float32
    D = D_MODEL

    def nrm(k, shape, s):
        return jax.random.normal(k, shape, f32) * s

    base_logit = jnp.log(2.0 ** (5.0 + jnp.arange(RET_HEADS, dtype=f32)) - 1.0)
    return {
        "x_prompt": nrm(ks[0], (BATCH, SEQ, D), 1.0),
        "x_sample": nrm(ks[1], (DEC_BATCH, DEC_SEQ, D), 1.0),
        "state_ret": nrm(ks[2], (DEC_BATCH, N_RET, 2, RET_HEADS, RET_DK, RET_DV), 0.1),
        "c": nrm(ks[3], (DEC_BATCH, D), 1.0),
        "c_ctx": nrm(ks[4], (D,), 1.0),
        "w_ada": nrm(ks[5], (DEPTH, D, 6 * D), 0.5 * D ** -0.5),
        "b_ada": nrm(ks[6], (DEPTH, 6 * D), 0.02),
        "norm_g": 1.0 + nrm(ks[7], (DEPTH, 2, D), 0.02),
        "final_g": 1.0 + nrm(ks[8], (D,), 0.02),
        "hy_w_in": nrm(ks[9], (N_HYENA, D, 3 * D), D ** -0.5),
        "hy_b_in": nrm(ks[10], (N_HYENA, 3 * D), 0.02),
        "hy_conv_w": nrm(ks[11], (N_HYENA, 3, 3 * D), 0.5),
        "hy_conv_b": nrm(ks[12], (N_HYENA, 3 * D), 0.02),
        "hy_f_w1": nrm(ks[13], (N_HYENA, FILTER_EMB, FILTER_WIDTH), FILTER_EMB ** -0.5),
        "hy_f_b1": nrm(ks[14], (N_HYENA, FILTER_WIDTH), 0.1),
        "hy_f_w2": nrm(ks[15], (N_HYENA, FILTER_WIDTH, FILTER_WIDTH), FILTER_WIDTH ** -0.5),
        "hy_f_b2": nrm(ks[16], (N_HYENA, FILTER_WIDTH), 0.1),
        "hy_f_w3": nrm(ks[17], (N_HYENA, FILTER_WIDTH, FILTER_WIDTH), FILTER_WIDTH ** -0.5),
        "hy_f_b3": nrm(ks[18], (N_HYENA, FILTER_WIDTH), 0.1),
        "hy_f_freq": 1.0 + nrm(ks[19], (N_HYENA, 3, FILTER_WIDTH), 0.01),
        "hy_f_wout": nrm(ks[20], (N_HYENA, FILTER_WIDTH, HYENA_ORDER * 2 * D), 0.1 * FILTER_WIDTH ** -0.5),
        "hy_f_skip": nrm(ks[21], (N_HYENA, HYENA_ORDER, D), 0.5),
        "hy_w_out": nrm(ks[22], (N_HYENA, D, D), D ** -0.5),
        "hy_b_out": nrm(ks[23], (N_HYENA, D), 0.02),
        "ret_w_qkvg": nrm(ks[24], (N_RET, D, 6 * D), D ** -0.5),
        "ret_decay": base_logit[None, None, :] + nrm(ks[25], (N_RET, 2, RET_HEADS), 0.1),
        "ret_gn_g": 1.0 + nrm(ks[26], (N_RET, 2 * D), 0.02),
        "ret_w_o": nrm(ks[27], (N_RET, 2 * D, D), (2 * D) ** -0.5),
        "mlp_w1": nrm(ks[28], (DEPTH, D, D_FF), D ** -0.5),
        "mlp_w2": nrm(ks[29], (DEPTH, D_FF, D), D_FF ** -0.5),
    }


def reference(x_prompt, x_sample, state_ret, c, c_ctx, w_ada, b_ada, norm_g, final_g,
              hy_w_in, hy_b_in, hy_conv_w, hy_conv_b, hy_f_w1, hy_f_b1, hy_f_w2, hy_f_b2,
              hy_f_w3, hy_f_b3, hy_f_freq, hy_f_wout, hy_f_skip, hy_w_out, hy_b_out,
              ret_w_qkvg, ret_decay, ret_gn_g, ret_w_o, mlp_w1, mlp_w2):

    def run(x, cond, s_init, latent):
        cond_act = jax.nn.silu(cond)
        states = []
        for layer in range(DEPTH):
            mod = (cond_act @ w_ada[layer] + b_ada[layer])[:, None, :]
            sh1, sc1, g1, sh2, sc2, g2 = jnp.split(mod, 6, axis=-1)
            h = _rms_norm(x, norm_g[layer, 0]) * (1.0 + sc1) + sh1
            i = layer // N_MIXERS
            if layer % N_MIXERS == 0:
                m = _hyena(h, hy_w_in[i], hy_b_in[i], hy_conv_w[i], hy_conv_b[i],
                           hy_f_w1[i], hy_f_b1[i], hy_f_w2[i], hy_f_b2[i], hy_f_w3[i], hy_f_b3[i],
                           hy_f_freq[i], hy_f_wout[i], hy_f_skip[i], hy_w_out[i], hy_b_out[i])
            else:
                m, s = _retention(h, s_init[:, i], ret_w_qkvg[i], ret_decay[i], ret_gn_g[i], ret_w_o[i], latent)
                states.append(s)
            x = x + g1 * m
            h = _rms_norm(x, norm_g[layer, 1]) * (1.0 + sc2) + sh2
            x = x + g2 * _sq_relu_mlp(h, mlp_w1[layer], mlp_w2[layer])
        return _rms_norm(x, final_g), states

    zero_state = jnp.zeros((x_prompt.shape[0], N_RET, 2, RET_HEADS, RET_DK, RET_DV), jnp.float32)
    y_prompt, ctx_states = run(x_prompt, c_ctx[None, :], zero_state, False)
    y_sample, _ = run(x_sample, c, state_ret, True)
    new_state_ret = jnp.stack(ctx_states, axis=1).astype(x_prompt.dtype)
    return (y_prompt, y_sample, new_state_ret)
```

```python
import functools
import math

import numpy as np
import jax
import jax.numpy as jnp
from jax import lax
from jax.experimental import pallas as pl
from jax.experimental.pallas import tpu as pltpu

F32 = jnp.float32
BF16 = jnp.bfloat16

EPS = 1e-6
GRID_W = 64
ROPE_BASE = 10000.0
FILTER_BANDS = 16
DECAY_TARGET = 1e-2
MIN_DECAY = math.log(DECAY_TARGET) / 1.5
MAX_DECAY = math.log(DECAY_TARGET) / 0.3

V7X_VMEM_BYTES = 64 * 1024 * 1024
VMEM_LIMIT = V7X_VMEM_BYTES - 8 * 1024 * 1024

TOKEN_TILE = 1024
COL_TILE = 512
FF_TILE = 512
NORM_ROWS = 128
MLP_ROWS = 256
CONV_CH_TILE = 256
FILT_CH_TILE = 512
MOD_ROWS = 8


def _params(semantics):
    return pltpu.CompilerParams(dimension_semantics=semantics, vmem_limit_bytes=VMEM_LIMIT)


def _bdot(a, b):
    return jnp.dot(a, b, preferred_element_type=F32)


def _fdot(a, b):
    return jnp.dot(a, b, preferred_element_type=F32, precision=lax.Precision.HIGHEST)


def _modulated_norm(x, g, scale, shift):
    ms = jnp.mean(x * x, axis=-1, keepdims=True)
    return (x * lax.rsqrt(ms + EPS) * g) * (1.0 + scale) + shift


def _fill_normed(x_ref, g_ref, sc_ref, sh_ref, h_scr):
    g = g_ref[...]
    sc = sc_ref[0]
    sh = sh_ref[0]

    def body(r, carry):
        r0 = pl.multiple_of(r * NORM_ROWS, NORM_ROWS)
        x = x_ref[pl.ds(r0, NORM_ROWS), :]
        h_scr[pl.ds(r0, NORM_ROWS), :] = _modulated_norm(x, g, sc, sh).astype(BF16)
        return carry

    lax.fori_loop(0, x_ref.shape[0] // NORM_ROWS, body, 0)


def _ada_kernel(cond_ref, w_ref, b_ref, o_ref):
    c = cond_ref[...]
    a = (c * jax.nn.sigmoid(c)).astype(BF16)
    o_ref[0] = _bdot(a, w_ref[0].astype(BF16)) + b_ref[0]


def _ada_mod(cond, w_ada, b_ada):
    depth, d, n = w_ada.shape
    tn = 1024
    return pl.pallas_call(
        _ada_kernel,
        grid=(depth, n // tn),
        in_specs=[
            pl.BlockSpec((MOD_ROWS, d), lambda l, j: (0, 0)),
            pl.BlockSpec((1, d, tn), lambda l, j: (l, 0, j)),
            pl.BlockSpec((1, 1, tn), lambda l, j: (l, 0, j)),
        ],
        out_specs=pl.BlockSpec((1, MOD_ROWS, tn), lambda l, j: (l, 0, j)),
        out_shape=jax.ShapeDtypeStruct((depth, MOD_ROWS, n), F32),
        compiler_params=_params(("arbitrary", "arbitrary")),
        name="ada_mod",
    )(cond, w_ada, b_ada.reshape(depth, 1, n))


class _Tokens:
    def __init__(self, n_prompt, prompt_len, n_sample, sample_len, d):
        assert sample_len == TOKEN_TILE
        assert (n_prompt * prompt_len) % TOKEN_TILE == 0 and TOKEN_TILE % prompt_len == 0
        assert prompt_len & (prompt_len - 1) == 0 and sample_len & (sample_len - 1) == 0
        assert n_sample + 1 <= MOD_ROWS
        self.n_prompt, self.prompt_len = n_prompt, prompt_len
        self.n_sample, self.sample_len = n_sample, sample_len
        self.d = d
        self.prompt_tiles = n_prompt * prompt_len // TOKEN_TILE
        self.tiles = self.prompt_tiles + n_sample
        self.total = self.tiles * TOKEN_TILE

    def mod_row(self, i):
        return jnp.maximum(i - self.prompt_tiles + 1, 0)

    def mod_spec(self, layer, chunk, width, tiled):
        per = self.d // width

        def index(i, j):
            return (layer * MOD_ROWS + self.mod_row(i), 0, chunk * per + (j if tiled else 0))

        return pl.BlockSpec((1, 1, width), index)


def _hyena_in_kernel(x_ref, g_ref, sc_ref, sh_ref, w_ref, b_ref, cw_ref, cb_ref, o_ref, h_scr,
                     *, prompt_tiles, prompt_len, sample_len):
    i = pl.program_id(0)

    @pl.when(pl.program_id(1) == 0)
    def _():
        _fill_normed(x_ref, g_ref, sc_ref, sh_ref, h_scr)

    p = _bdot(h_scr[...], w_ref[...]) + b_ref[...]
    rows = p.shape[0]
    seq = jnp.where(i < prompt_tiles, prompt_len, sample_len)
    pos = lax.broadcasted_iota(jnp.int32, p.shape, 0) & (seq - 1)
    prev = jnp.where(pos == 0, 0.0, pltpu.roll(p, 1, 0))
    nxt = jnp.where(pos == seq - 1, 0.0, pltpu.roll(p, rows - 1, 0))
    cw = cw_ref[...]
    o_ref[...] = prev * cw[0:1] + p * cw[1:2] + nxt * cw[2:3] + cb_ref[...]


def _hyena_in(tok, x, norm_g, mod3, layer, w, b, conv_w, conv_b):
    d = tok.d
    n = w.shape[1]
    kern = functools.partial(_hyena_in_kernel, prompt_tiles=tok.prompt_tiles,
                             prompt_len=tok.prompt_len, sample_len=tok.sample_len)
    return pl.pallas_call(
        kern,
        grid=(tok.tiles, n // COL_TILE),
        in_specs=[
            pl.BlockSpec((TOKEN_TILE, d), lambda i, j: (i, 0)),
            pl.BlockSpec((1, d), lambda i, j: (0, 0)),
            tok.mod_spec(layer, 1, d, False),
            tok.mod_spec(layer, 0, d, False),
            pl.BlockSpec((d, COL_TILE), lambda i, j: (0, j)),
            pl.BlockSpec((1, COL_TILE), lambda i, j: (0, j)),
            pl.BlockSpec((3, COL_TILE), lambda i, j: (0, j)),
            pl.BlockSpec((1, COL_TILE), lambda i, j: (0, j)),
        ],
        out_specs=pl.BlockSpec((TOKEN_TILE, COL_TILE), lambda i, j: (i, j)),
        out_shape=jax.ShapeDtypeStruct((tok.total, n), F32),
        scratch_shapes=[pltpu.VMEM((TOKEN_TILE, d), BF16)],
        compiler_params=_params(("arbitrary", "arbitrary")),
        name="hyena_in",
    )(x, norm_g.reshape(1, d), mod3, mod3, w, b.reshape(1, n), conv_w, conv_b.reshape(1, n))


def _dft_matrices(L):
    f = np.arange(L, dtype=np.int64)[:, None]
    t = np.arange(L, dtype=np.int64)[None, :]
    ang = ((f * t) % (2 * L)).astype(np.float64) * (np.pi / L)
    cos, sin = np.cos(ang), np.sin(ang)
    nyq = np.where(np.arange(L) % 2 == 0, 1.0, -1.0)
    bot = sin.copy()
    bot[0, :] = nyq
    fwd = np.concatenate([cos, bot], axis=0)
    wgt = np.full((L,), 2.0)
    wgt[0] = 1.0
    inv_cos = (cos * wgt[:, None]).T / (2 * L)
    inv_sin = (-2.0 * sin).T / (2 * L)
    inv_sin[:, 0] = nyq / (2 * L)
    inv = np.concatenate([inv_cos, inv_sin], axis=1)
    return jnp.asarray(fwd, F32).astype(BF16), jnp.asarray(inv, F32).astype(BF16)


def _filter_features(L, width):
    pos = np.arange(L, dtype=np.float64)
    t = pos / L
    omega = 2.0 * math.pi * pos / L
    bands = np.linspace(1e-4, FILTER_BANDS - 1, FILTER_BANDS)
    ang = omega[:, None] * bands[None, :]
    feat = np.concatenate([t[:, None], np.cos(ang), -np.sin(ang)], axis=-1)
    feat = np.pad(feat, ((0, 0), (0, width - feat.shape[1])))
    return jnp.asarray(feat, F32), jnp.asarray(t[:, None], F32)


def _filter_kernel(feat_ref, w1_ref, b1_ref, w2_ref, b2_ref, w3_ref, b3_ref, freq_ref,
                   wof_ref, wob_ref, t_ref, delta_ref, fwd_ref,
                   kr_ref, kiz_ref, krn_ref, h_scr, *, L):
    first = jnp.logical_and(pl.program_id(0) == 0, pl.program_id(1) == 0)

    @pl.when(first)
    def _():
        h = jnp.sin(freq_ref[0:1, :] * (_fdot(feat_ref[...], w1_ref[...]) + b1_ref[...]))
        h = jnp.sin(freq_ref[1:2, :] * (_fdot(h, w2_ref[...]) + b2_ref[...]))
        h = jnp.sin(freq_ref[2:3, :] * (_fdot(h, w3_ref[...]) + b3_ref[...]))
        h_scr[...] = h

    h = h_scr[...]
    window = jnp.exp(-t_ref[...] * delta_ref[...])
    hf = _fdot(h, wof_ref[...]) * window
    hb = _fdot(h, wob_ref[...]) * window
    row = lax.broadcasted_iota(jnp.int32, hf.shape, 0)
    hb = jnp.where(row == 0, 0.0, hb)
    tap_sum = hf + hb
    tap_dif = hb - hf
    kr = _bdot(fwd_ref[0:L, :], tap_sum.astype(BF16))
    ki = _bdot(fwd_ref[L:2 * L, :], tap_dif.astype(BF16))
    sign = jnp.where((row & 1) == 0, 1.0, -1.0)
    k_nyq = jnp.sum(tap_sum * sign, axis=0, keepdims=True)
    kr_ref[0] = kr
    kiz_ref[0] = jnp.where(row == 0, 0.0, ki)
    krn_ref[0] = jnp.where(row == 0, -k_nyq, kr)


def _filter_spectra(L, d, fwd, w1, b1, w2, b2, w3, b3, freq, wout):
    width = w2.shape[0]
    order = wout.shape[1] // (2 * d)
    feat, t = _filter_features(L, width)
    w1p = jnp.pad(w1, ((0, width - w1.shape[0]), (0, 0)))
    deltas = jnp.asarray(np.abs(np.linspace(MIN_DECAY, MAX_DECAY, d))[None, :], F32)
    ct = FILT_CH_TILE
    per = d // ct
    small = lambda shape: pl.BlockSpec(shape, lambda o, j: (0, 0))
    out_spec = pl.BlockSpec((1, L, ct), lambda o, j: (o, 0, j))
    out_shape = jax.ShapeDtypeStruct((order, L, d), F32)
    return pl.pallas_call(
        functools.partial(_filter_kernel, L=L),
        grid=(order, per),
        in_specs=[
            small((L, width)), small((width, width)), small((1, width)),
            small((width, width)), small((1, width)),
            small((width, width)), small((1, width)), small((3, width)),
            pl.BlockSpec((width, ct), lambda o, j: (0, (2 * o) * per + j)),
            pl.BlockSpec((width, ct), lambda o, j: (0, (2 * o + 1) * per + j)),
            small((L, 1)),
            pl.BlockSpec((1, ct), lambda o, j: (0, j)),
            small((2 * L, L)),
        ],
        out_specs=[out_spec, out_spec, out_spec],
        out_shape=[out_shape, out_shape, out_shape],
        scratch_shapes=[pltpu.VMEM((L, width), F32)],
        compiler_params=_params(("arbitrary", "arbitrary")),
        name=f"hyena_filter_{L}",
    )(feat, w1p, b1.reshape(1, width), w2, b2.reshape(1, width), w3, b3.reshape(1, width),
      freq, wout, wout, t, deltas, fwd)


def _longconv_kernel(*refs, L, aliased):
    if aliased:
        refs = refs[1:]
    v_ref, x1_ref, x2_ref, kr_ref, kiz_ref, krn_ref, skip_ref, fwd_ref, inv_ref, o_ref = refs

    def conv(u, o):
        spec = _bdot(fwd_ref[...], u.astype(BF16))
        a, b = spec[:L], spec[L:]
        kr, kiz, krn = kr_ref[o], kiz_ref[o], krn_ref[o]
        top = (a * kr + b * kiz).astype(BF16)
        bot = (a * kiz - b * krn).astype(BF16)
        return _bdot(inv_ref[:, 0:L], top) + _bdot(inv_ref[:, L:2 * L], bot)

    skip = skip_ref[...]
    v = v_ref[...]
    z = x1_ref[...] * (conv(v, 0) + v * skip[0:1])
    z = x2_ref[...] * (conv(z, 1) + z * skip[1:2])
    o_ref[...] = z.astype(BF16)


def _longconv(proj, spectra, skip, fwd, inv, L, n_seq, row0, d, prev=None):
    total = proj.shape[0]
    ct = CONV_CH_TILE
    per = d // ct
    rb0 = row0 // L
    kr, kiz, krn = spectra
    order = kr.shape[0]
    kspec = pl.BlockSpec((order, L, ct), lambda j, b: (0, 0, j))
    in_specs = [
        pl.BlockSpec((L, ct), lambda j, b: (rb0 + b, j)),
        pl.BlockSpec((L, ct), lambda j, b: (rb0 + b, per + j)),
        pl.BlockSpec((L, ct), lambda j, b: (rb0 + b, 2 * per + j)),
        kspec, kspec, kspec,
        pl.BlockSpec((order, ct), lambda j, b: (0, j)),
        pl.BlockSpec((2 * L, L), lambda j, b: (0, 0)),
        pl.BlockSpec((L, 2 * L), lambda j, b: (0, 0)),
    ]
    args = [proj, proj, proj, kr, kiz, krn, skip, fwd, inv]
    aliases = {}
    if prev is not None:
        in_specs = [pl.BlockSpec(memory_space=pl.ANY)] + in_specs
        args = [prev] + args
        aliases = {0: 0}
    return pl.pallas_call(
        functools.partial(_longconv_kernel, L=L, aliased=prev is not None),
        grid=(per, n_seq),
        in_specs=in_specs,
        out_specs=pl.BlockSpec((L, ct), lambda j, b: (rb0 + b, j)),
        out_shape=jax.ShapeDtypeStruct((total, d), BF16),
        input_output_aliases=aliases,
        compiler_params=_params(("arbitrary", "arbitrary")),
        name=f"hyena_longconv_{L}",
    )(*args)


def _proj_res_kernel(*refs, has_bias):
    if has_bias:
        a_ref, w_ref, b_ref, x_ref, gate_ref, o_ref = refs
    else:
        a_ref, w_ref, x_ref, gate_ref, o_ref = refs
    m = _bdot(a_ref[...], w_ref[...])
    if has_bias:
        m = m + b_ref[...]
    o_ref[...] = x_ref[...] + gate_ref[0] * m


def _proj_res(tok, a, w, b, x, mod3, layer):
    k, n = w.shape
    in_specs = [
        pl.BlockSpec((TOKEN_TILE, k), lambda i, j: (i, 0)),
        pl.BlockSpec((k, COL_TILE), lambda i, j: (0, j)),
    ]
    args = [a, w]
    if b is not None:
        in_specs.append(pl.BlockSpec((1, COL_TILE), lambda i, j: (0, j)))
        args.append(b.reshape(1, n))
    in_specs += [
        pl.BlockSpec((TOKEN_TILE, COL_TILE), lambda i, j: (i, j)),
        tok.mod_spec(layer, 2, COL_TILE, True),
    ]
    args += [x, mod3]
    return pl.pallas_call(
        functools.partial(_proj_res_kernel, has_bias=b is not None),
        grid=(tok.tiles, n // COL_TILE),
        in_specs=in_specs,
        out_specs=pl.BlockSpec((TOKEN_TILE, COL_TILE), lambda i, j: (i, j)),
        out_shape=jax.ShapeDtypeStruct((tok.total, n), F32),
        compiler_params=_params(("arbitrary", "arbitrary")),
        name=f"proj_res_{k}",
    )(*args)


def _mlp_kernel(*refs, final_norm):
    if final_norm:
        x_ref, g_ref, sc_ref, sh_ref, gate_ref, w1_ref, w2_ref, fg_ref, o_ref, h_scr = refs
    else:
        x_ref, g_ref, sc_ref, sh_ref, gate_ref, w1_ref, w2_ref, o_ref, h_scr = refs
    f = pl.program_id(1)
    steps = x_ref.shape[0] // MLP_ROWS

    @pl.when(f == 0)
    def _():
        _fill_normed(x_ref, g_ref, sc_ref, sh_ref, h_scr)

    def body(r, carry):
        r0 = pl.multiple_of(r * MLP_ROWS, MLP_ROWS)
        u = _bdot(h_scr[pl.ds(r0, MLP_ROWS), :], w1_ref[...])
        u = jnp.square(jnp.maximum(u, 0.0)).astype(BF16)
        dn = _bdot(u, w2_ref[...])

        @pl.when(f == 0)
        def _():
            o_ref[pl.ds(r0, MLP_ROWS), :] = dn

        @pl.when(f > 0)
        def _():
            o_ref[pl.ds(r0, MLP_ROWS), :] += dn

        return carry

    lax.fori_loop(0, steps, body, 0)

    @pl.when(f == pl.num_programs(1) - 1)
    def _():
        gate = gate_ref[0]

        def fin(r, carry):
            r0 = pl.multiple_of(r * NORM_ROWS, NORM_ROWS)
            y = x_ref[pl.ds(r0, NORM_ROWS), :] + gate * o_ref[pl.ds(r0, NORM_ROWS), :]
            if final_norm:
                ms = jnp.mean(y * y, axis=-1, keepdims=True)
                y = y * lax.rsqrt(ms + EPS) * fg_ref[...]
            o_ref[pl.ds(r0, NORM_ROWS), :] = y
            return carry

        lax.fori_loop(0, x_ref.shape[0] // NORM_ROWS, fin, 0)


def _mlp(tok, x, norm_g, mod3, layer, w1, w2, final_g=None):
    d = tok.d
    ff = w1.shape[1]
    in_specs = [
        pl.BlockSpec((TOKEN_TILE, d), lambda i, f: (i, 0)),
        pl.BlockSpec((1, d), lambda i, f: (0, 0)),
        tok.mod_spec(layer, 4, d, False),
        tok.mod_spec(layer, 3, d, False),
        tok.mod_spec(layer, 5, d, False),
        pl.BlockSpec((d, FF_TILE), lambda i, f: (0, f)),
        pl.BlockSpec((FF_TILE, d), lambda i, f: (f, 0)),
    ]
    args = [x, norm_g.reshape(1, d), mod3, mod3, mod3, w1, w2]
    if final_g is not None:
        in_specs.append(pl.BlockSpec((1, d), lambda i, f: (0, 0)))
        args.append(final_g.reshape(1, d))
    return pl.pallas_call(
        functools.partial(_mlp_kernel, final_norm=final_g is not None),
        grid=(tok.tiles, ff // FF_TILE),
        in_specs=in_specs,
        out_specs=pl.BlockSpec((TOKEN_TILE, d), lambda i, f: (i, 0)),
        out_shape=jax.ShapeDtypeStruct((tok.total, d), F32),
        scratch_shapes=[pltpu.VMEM((TOKEN_TILE, d), BF16)],
        compiler_params=_params(("arbitrary", "arbitrary")),
        name="mlp",
    )(*args)


def _rotary_tables(L, dk, width):
    n_rows = L // GRID_W
    rows = np.repeat(np.arange(n_rows, dtype=np.float64), GRID_W)
    cols = np.tile(np.arange(GRID_W, dtype=np.float64), n_rows)
    n_pairs = dk // 4
    inv = ROPE_BASE ** (-np.arange(n_pairs, dtype=np.float64) / n_pairs)
    ang = np.concatenate([rows[:, None] * inv, cols[:, None] * inv], axis=-1)
    cos = np.repeat(np.cos(ang), 2, axis=-1)
    sin = np.repeat(np.sin(ang), 2, axis=-1)
    sin[:, 0::2] *= -1.0
    reps = width // dk
    return (jnp.asarray(np.tile(cos, (1, reps)), F32), jnp.asarray(np.tile(sin, (1, reps)), F32))


def _qkvg_kernel(x_ref, g_ref, sc_ref, sh_ref, w_ref, cos_ref, sin_ref, o_ref, h_scr,
                 *, prompt_tiles, qk_tiles, k_tile0, k_scale):
    i = pl.program_id(0)
    j = pl.program_id(1)

    @pl.when(j == 0)
    def _():
        _fill_normed(x_ref, g_ref, sc_ref, sh_ref, h_scr)

    acc = _bdot(h_scr[...], w_ref[...])
    scale = jnp.where(jnp.logical_and(j >= k_tile0, j < qk_tiles), k_scale, 1.0)
    rotate = jnp.logical_and(i >= prompt_tiles, j < qk_tiles)

    @pl.when(rotate)
    def _():
        width = acc.shape[1]
        lane = lax.broadcasted_iota(jnp.int32, acc.shape, 1)
        partner = jnp.where((lane & 1) == 0, pltpu.roll(acc, width - 1, 1), pltpu.roll(acc, 1, 1))
        o_ref[...] = ((acc * cos_ref[...] + partner * sin_ref[...]) * scale).astype(BF16)

    @pl.when(jnp.logical_not(rotate))
    def _():
        o_ref[...] = (acc * scale).astype(BF16)


def _qkvg(tok, x, norm_g, mod3, layer, w, dk):
    d = tok.d
    n = w.shape[1]
    cos, sin = _rotary_tables(tok.sample_len, dk, COL_TILE)
    kern = functools.partial(_qkvg_kernel, prompt_tiles=tok.prompt_tiles,
                             qk_tiles=2 * d // COL_TILE, k_tile0=d // COL_TILE,
                             k_scale=dk ** -0.5)
    return pl.pallas_call(
        kern,
        grid=(tok.tiles, n // COL_TILE),
        in_specs=[
            pl.BlockSpec((TOKEN_TILE, d), lambda i, j: (i, 0)),
            pl.BlockSpec((1, d), lambda i, j: (0, 0)),
            tok.mod_spec(layer, 1, d, False),
            tok.mod_spec(layer, 0, d, False),
            pl.BlockSpec((d, COL_TILE), lambda i, j: (0, j)),
            pl.BlockSpec((TOKEN_TILE, COL_TILE), lambda i, j: (0, 0)),
            pl.BlockSpec((TOKEN_TILE, COL_TILE), lambda i, j: (0, 0)),
        ],
        out_specs=pl.BlockSpec((TOKEN_TILE, COL_TILE), lambda i, j: (i, j)),
        out_shape=jax.ShapeDtypeStruct((tok.total, n), BF16),
        scratch_shapes=[pltpu.VMEM((TOKEN_TILE, d), BF16)],
        compiler_params=_params(("arbitrary", "arbitrary")),
        name="ret_qkvg",
    )(x, norm_g.reshape(1, d), mod3, mod3, w, cos, sin)


def _log_sigmoid(x):
    return jnp.minimum(x, 0.0) - jnp.log1p(jnp.exp(-jnp.abs(x)))


def _retention_kernel(*refs, L, C, has_s0, emit_state, aliased):
    refs = list(refs)
    if aliased:
        refs.pop(0)
    decay_ref, q_ref, k_ref, v_ref, g_ref, gn_ref = refs[:6]
    refs = refs[6:]
    s0_ref = refs.pop(0) if has_s0 else None
    y_ref = refs.pop(0)
    so_ref = refs.pop(0) if emit_state else None
    sb_scr = refs.pop(0)

    head = pl.program_id(1)
    dk = q_ref.shape[1]
    dv = v_ref.shape[1]
    nc = L // C

    def log_decay(direction, width):
        return _log_sigmoid(jnp.full((1, width), decay_ref[direction, head], F32))

    ri = lax.broadcasted_iota(jnp.int32, (C, C), 0)
    ci = lax.broadcasted_iota(jnp.int32, (C, C), 1)
    diff = (ri - ci).astype(F32)
    lower = diff >= 0
    upper = diff <= 0
    mask = (jnp.where(lower, jnp.exp(jnp.where(lower, diff, 0.0) * log_decay(0, C)), 0.0)
            + jnp.where(upper, jnp.exp(jnp.where(upper, -diff, 0.0) * log_decay(1, C)), 0.0))
    rk = lax.broadcasted_iota(jnp.int32, (C, dk), 0).astype(F32)
    q_dec_f = jnp.exp((rk + 1.0) * log_decay(0, dk))
    k_dec_f = jnp.exp((C - 1.0 - rk) * log_decay(0, dk))
    q_dec_b = jnp.exp((C - rk) * log_decay(1, dk))
    k_dec_b = jnp.exp(rk * log_decay(1, dk))
    chunk_dec_f = jnp.exp(C * log_decay(0, dv))
    chunk_dec_b = jnp.exp(C * log_decay(1, dv))

    def kv(n, k_dec):
        kd = (k_ref[n * C:(n + 1) * C, :].astype(F32) * k_dec).astype(BF16)
        return lax.dot_general(kd, v_ref[n * C:(n + 1) * C, :], (((0,), (0,)), ((), ())),
                               preferred_element_type=F32)

    def advance(state, dec, upd):
        return upd if state is None else state * dec + upd

    state_b = s0_ref[0, 0, 1, 0] if has_s0 else None
    has_b = [False] * nc
    for n in reversed(range(nc)):
        if state_b is not None:
            sb_scr[n] = state_b.astype(BF16)
            has_b[n] = True
        if n > 0 or emit_state:
            state_b = advance(state_b, chunk_dec_b, kv(n, k_dec_b))
    if emit_state:
        so_ref[0, 0, 1, 0] = state_b

    state_f = s0_ref[0, 0, 0, 0] if has_s0 else None
    gn = gn_ref[...]
    for n in range(nc):
        rows = slice(n * C, (n + 1) * C)
        q = q_ref[rows, :]
        scores = lax.dot_general(q, k_ref[rows, :], (((1,), (1,)), ((), ())),
                                 preferred_element_type=F32)
        y = _bdot((scores * mask).astype(BF16), v_ref[rows, :])
        qf = q.astype(F32)
        if state_f is not None:
            y = y + _bdot((qf * q_dec_f).astype(BF16), state_f.astype(BF16))
        if has_b[n]:
            y = y + _bdot((qf * q_dec_b).astype(BF16), sb_scr[n])
        mu = jnp.mean(y, axis=-1, keepdims=True)
        yc = y - mu
        var = jnp.mean(yc * yc, axis=-1, keepdims=True)
        yn = (yc * lax.rsqrt(var + EPS)) * gn
        gate = g_ref[rows, :].astype(F32)
        y_ref[rows, :] = ((gate * jax.nn.sigmoid(gate)) * yn).astype(BF16)
        if n < nc - 1 or emit_state:
            state_f = advance(state_f, chunk_dec_f, kv(n, k_dec_f))
    if emit_state:
        so_ref[0, 0, 0, 0] = state_f


def _retention(qkvg, decay, gn_g, s0, L, C, n_seq, row0, heads, dk, dv, emit_state, prev=None):
    total = qkvg.shape[0]
    rb0 = row0 // L
    kq = heads
    kv0 = 2 * heads * dk // dv
    kg0 = kv0 + heads
    in_specs = [
        pl.BlockSpec(memory_space=pltpu.SMEM),
        pl.BlockSpec((L, dk), lambda b, h: (rb0 + b, h)),
        pl.BlockSpec((L, dk), lambda b, h: (rb0 + b, kq + h)),
        pl.BlockSpec((L, dv), lambda b, h: (rb0 + b, kv0 + h)),
        pl.BlockSpec((L, dv), lambda b, h: (rb0 + b, kg0 + h)),
        pl.BlockSpec((1, dv), lambda b, h: (0, h)),
    ]
    args = [decay, qkvg, qkvg, qkvg, qkvg, gn_g.reshape(1, heads * dv)]
    state_block = (1, 1, 2, 1, dk, dv)
    state_index = lambda b, h: (b, 0, 0, h, 0, 0)
    if s0 is not None:
        in_specs.append(pl.BlockSpec(state_block, state_index))
        args.append(s0)
    aliases = {}
    if prev is not None:
        in_specs = [pl.BlockSpec(memory_space=pl.ANY)] + in_specs
        args = [prev] + args
        aliases = {0: 0}
    out_specs = [pl.BlockSpec((L, dv), lambda b, h: (rb0 + b, h))]
    out_shape = [jax.ShapeDtypeStruct((total, heads * dv), BF16)]
    if emit_state:
        out_specs.append(pl.BlockSpec(state_block, state_index))
        out_shape.append(jax.ShapeDtypeStruct((n_seq, 1, 2, heads, dk, dv), F32))
    kern = functools.partial(_retention_kernel, L=L, C=C, has_s0=s0 is not None,
                             emit_state=emit_state, aliased=prev is not None)
    return pl.pallas_call(
        kern,
        grid=(n_seq, heads),
        in_specs=in_specs,
        out_specs=out_specs,
        out_shape=out_shape,
        scratch_shapes=[pltpu.VMEM((L // C, dk, dv), BF16)],
        input_output_aliases=aliases,
        compiler_params=_params(("arbitrary", "arbitrary")),
        name=f"retention_{L}",
    )(*args)


def kernel(x_prompt, x_sample, state_ret, c, c_ctx, w_ada, b_ada, norm_g, final_g, hy_w_in, hy_b_in, hy_conv_w, hy_conv_b, hy_f_w1, hy_f_b1, hy_f_w2, hy_f_b2, hy_f_w3, hy_f_b3, hy_f_freq, hy_f_wout, hy_f_skip, hy_w_out, hy_b_out, ret_w_qkvg, ret_decay, ret_gn_g, ret_w_o, mlp_w1, mlp_w2):
    n_prompt, prompt_len, d = x_prompt.shape
    n_sample, sample_len, _ = x_sample.shape
    heads, dk, dv = state_ret.shape[3:]
    depth = w_ada.shape[0]
    assert depth == 2 and hy_w_in.shape[0] == 1 and ret_w_qkvg.shape[0] == 1
    tok = _Tokens(n_prompt, prompt_len, n_sample, sample_len, d)

    x = jnp.concatenate([x_prompt.reshape(-1, d), x_sample.reshape(-1, d)], axis=0)
    cond = jnp.concatenate(
        [c_ctx[None, :], c, jnp.zeros((MOD_ROWS - 1 - n_sample, d), F32)], axis=0)
    mod = _ada_mod(cond, w_ada, b_ada)
    mod3 = mod.reshape(depth * MOD_ROWS, 1, mod.shape[-1])

    proj = _hyena_in(tok, x, norm_g[0, 0], mod3, 0, hy_w_in[0].astype(BF16), hy_b_in[0],
                     hy_conv_w[0], hy_conv_b[0])
    z = None
    for L, n_seq, row0 in ((prompt_len, n_prompt, 0),
                           (sample_len, n_sample, n_prompt * prompt_len)):
        fwd, inv = _dft_matrices(L)
        spectra = _filter_spectra(L, d, fwd, hy_f_w1[0], hy_f_b1[0], hy_f_w2[0], hy_f_b2[0],
                                  hy_f_w3[0], hy_f_b3[0], hy_f_freq[0], hy_f_wout[0])
        z = _longconv(proj, spectra, hy_f_skip[0], fwd, inv, L, n_seq, row0, d, prev=z)
    x = _proj_res(tok, z, hy_w_out[0].astype(BF16), hy_b_out[0], x, mod3, 0)
    x = _mlp(tok, x, norm_g[0, 1], mod3, 0, mlp_w1[0].astype(BF16), mlp_w2[0].astype(BF16))

    qkvg = _qkvg(tok, x, norm_g[1, 0], mod3, 1, ret_w_qkvg[0].astype(BF16), dk)
    yg, new_state = _retention(qkvg, ret_decay[0], ret_gn_g[0], None, prompt_len, prompt_len,
                               n_prompt, 0, heads, dk, dv, emit_state=True)
    (yg,) = _retention(qkvg, ret_decay[0], ret_gn_g[0], state_ret, sample_len, 256,
                       n_sample, n_prompt * prompt_len, heads, dk, dv, emit_state=False,
                       prev=yg)
    x = _proj_res(tok, yg, ret_w_o[0].astype(BF16), None, x, mod3, 1)
    out = _mlp(tok, x, norm_g[1, 1], mod3, 1, mlp_w1[1].astype(BF16), mlp_w2[1].astype(BF16),
               final_g=final_g)

    split = n_prompt * prompt_len
    y_prompt = out[:split].reshape(n_prompt, prompt_len, d)
    y_sample = out[split:].reshape(n_sample, sample_len, d)
    return (y_prompt, y_sample, new_state.astype(x_prompt.dtype))
```

```python
import functools
import math

import numpy as np
import jax
import jax.numpy as jnp
from jax import lax
from jax.experimental import pallas as pl
from jax.experimental.pallas import tpu as pltpu

F32 = jnp.float32
BF16 = jnp.bfloat16

EPS = 1e-6
GRID_W = 64
ROPE_BASE = 10000.0
FILTER_BANDS = 16
DECAY_TARGET = 1e-2
MIN_DECAY = math.log(DECAY_TARGET) / 1.5
MAX_DECAY = math.log(DECAY_TARGET) / 0.3

V7X_VMEM_BYTES = 64 * 1024 * 1024
VMEM_LIMIT = V7X_VMEM_BYTES - 8 * 1024 * 1024

TOKEN_TILE = 1024
COL_TILE = 1024
SUB_COLS = 512
RES_COL_TILE = 512
FF_TILE = 1024
NORM_ROWS = 128
MLP_ROWS = 256
CONV_TILE_ELEMS = 256 * 1024
FILT_CH_TILE = 512
RET_CHUNK = 256
MOD_ROWS = 8


def _params(semantics):
    return pltpu.CompilerParams(dimension_semantics=semantics, vmem_limit_bytes=VMEM_LIMIT)


def _bdot(a, b):
    return jnp.dot(a, b, preferred_element_type=F32)


def _fdot(a, b):
    return jnp.dot(a, b, preferred_element_type=F32, precision=lax.Precision.HIGHEST)


def _modulated_norm(x, g, scale, shift):
    ms = jnp.mean(x * x, axis=-1, keepdims=True)
    return (x * lax.rsqrt(ms + EPS) * g) * (1.0 + scale) + shift


def _fill_normed(x_ref, g_ref, sc_ref, sh_ref, h_scr, zero_ref=None):
    g = g_ref[...]
    sc = sc_ref[0]
    sh = sh_ref[0]

    def body(r, carry):
        r0 = pl.multiple_of(r * NORM_ROWS, NORM_ROWS)
        x = x_ref[pl.ds(r0, NORM_ROWS), :]
        h_scr[pl.ds(r0, NORM_ROWS), :] = _modulated_norm(x, g, sc, sh).astype(BF16)
        if zero_ref is not None:
            zero_ref[pl.ds(r0, NORM_ROWS), :] = jnp.zeros((NORM_ROWS, zero_ref.shape[1]), F32)
        return carry

    lax.fori_loop(0, x_ref.shape[0] // NORM_ROWS, body, 0)


class _Group:
    def __init__(self, n_seq, seq_len, d, mod_row0, mod_row_step):
        assert seq_len & (seq_len - 1) == 0
        assert (n_seq * seq_len) % TOKEN_TILE == 0
        assert TOKEN_TILE % seq_len == 0 or (mod_row_step == 0 and seq_len % TOKEN_TILE == 0)
        assert mod_row_step == 0 or seq_len == TOKEN_TILE
        self.n_seq, self.seq_len, self.d = n_seq, seq_len, d
        self.rows = n_seq * seq_len
        self.tiles = self.rows // TOKEN_TILE
        self.mod_row0, self.mod_row_step = mod_row0, mod_row_step

    def mod_spec(self, layer, chunk, width, tiled):
        per = self.d // width
        base = layer * MOD_ROWS + self.mod_row0
        step = self.mod_row_step

        def index(i, j):
            return (base + step * i, 0, chunk * per + (j if tiled else 0))

        return pl.BlockSpec((1, 1, width), index)


def _ada_kernel(cond_ref, w_ref, b_ref, o_ref):
    c = cond_ref[...]
    a = (c * jax.nn.sigmoid(c)).astype(BF16)
    o_ref[0] = _bdot(a, w_ref[0].astype(BF16)) + b_ref[0]


def _ada_mod(cond, w_ada, b_ada):
    depth, d, n = w_ada.shape
    tn = 1024
    return pl.pallas_call(
        _ada_kernel,
        grid=(depth, n // tn),
        in_specs=[
            pl.BlockSpec((MOD_ROWS, d), lambda l, j: (0, 0)),
            pl.BlockSpec((1, d, tn), lambda l, j: (l, 0, j)),
            pl.BlockSpec((1, 1, tn), lambda l, j: (l, 0, j)),
        ],
        out_specs=pl.BlockSpec((1, MOD_ROWS, tn), lambda l, j: (l, 0, j)),
        out_shape=jax.ShapeDtypeStruct((depth, MOD_ROWS, n), F32),
        compiler_params=_params(("arbitrary", "arbitrary")),
        name="ada_mod",
    )(cond, w_ada, b_ada.reshape(depth, 1, n))


def _hyena_in_kernel(x_ref, g_ref, sc_ref, sh_ref, w_ref, b_ref, cw_ref, cb_ref, o_ref, h_scr,
                     *, seq_len):
    @pl.when(pl.program_id(1) == 0)
    def _():
        _fill_normed(x_ref, g_ref, sc_ref, sh_ref, h_scr)

    rows = h_scr.shape[0]
    pos = lax.broadcasted_iota(jnp.int32, (rows, SUB_COLS), 0) & (seq_len - 1)
    first = pos == 0
    last = pos == seq_len - 1
    for s in range(o_ref.shape[1] // SUB_COLS):
        cols = slice(s * SUB_COLS, (s + 1) * SUB_COLS)
        p = _bdot(h_scr[...], w_ref[:, cols]) + b_ref[:, cols]
        prev = jnp.where(first, 0.0, pltpu.roll(p, 1, 0))
        nxt = jnp.where(last, 0.0, pltpu.roll(p, rows - 1, 0))
        o_ref[:, cols] = (prev * cw_ref[0:1, cols] + p * cw_ref[1:2, cols]
                          + nxt * cw_ref[2:3, cols] + cb_ref[:, cols])


def _hyena_in(grp, x, norm_g, mod3, layer, w, b, conv_w, conv_b):
    d = grp.d
    n = w.shape[1]
    return pl.pallas_call(
        functools.partial(_hyena_in_kernel, seq_len=grp.seq_len),
        grid=(grp.tiles, n // COL_TILE),
        in_specs=[
            pl.BlockSpec((TOKEN_TILE, d), lambda i, j: (i, 0)),
            pl.BlockSpec((1, d), lambda i, j: (0, 0)),
            grp.mod_spec(layer, 1, d, False),
            grp.mod_spec(layer, 0, d, False),
            pl.BlockSpec((d, COL_TILE), lambda i, j: (0, j)),
            pl.BlockSpec((1, COL_TILE), lambda i, j: (0, j)),
            pl.BlockSpec((3, COL_TILE), lambda i, j: (0, j)),
            pl.BlockSpec((1, COL_TILE), lambda i, j: (0, j)),
        ],
        out_specs=pl.BlockSpec((TOKEN_TILE, COL_TILE), lambda i, j: (i, j)),
        out_shape=jax.ShapeDtypeStruct((grp.rows, n), F32),
        scratch_shapes=[pltpu.VMEM((TOKEN_TILE, d), BF16)],
        compiler_params=_params(("arbitrary", "arbitrary")),
        name=f"hyena_in_{grp.seq_len}",
    )(x, norm_g.reshape(1, d), mod3, mod3, w, b.reshape(1, n), conv_w, conv_b.reshape(1, n))


def _dft_matrices(L):
    f = np.arange(L, dtype=np.int64)[:, None]
    t = np.arange(L, dtype=np.int64)[None, :]
    ang = ((f * t) % (2 * L)).astype(np.float64) * (np.pi / L)
    cos, sin = np.cos(ang), np.sin(ang)
    nyq = np.where(np.arange(L) % 2 == 0, 1.0, -1.0)
    bot = sin.copy()
    bot[0, :] = nyq
    fwd = np.concatenate([cos, bot], axis=0)
    wgt = np.full((L,), 2.0)
    wgt[0] = 1.0
    inv_cos = (cos * wgt[:, None]).T / (2 * L)
    inv_sin = (-2.0 * sin).T / (2 * L)
    inv_sin[:, 0] = nyq / (2 * L)
    inv = np.concatenate([inv_cos, inv_sin], axis=1)
    return jnp.asarray(fwd, F32).astype(BF16), jnp.asarray(inv, F32).astype(BF16)


def _filter_features(L, width):
    pos = np.arange(L, dtype=np.float64)
    t = pos / L
    omega = 2.0 * math.pi * pos / L
    bands = np.linspace(1e-4, FILTER_BANDS - 1, FILTER_BANDS)
    ang = omega[:, None] * bands[None, :]
    feat = np.concatenate([t[:, None], np.cos(ang), -np.sin(ang)], axis=-1)
    feat = np.pad(feat, ((0, 0), (0, width - feat.shape[1])))
    return jnp.asarray(feat, F32), jnp.asarray(t[:, None], F32)


def _filter_kernel(feat_ref, w1_ref, b1_ref, w2_ref, b2_ref, w3_ref, b3_ref, freq_ref,
                   wof_ref, wob_ref, t_ref, delta_ref, fwd_ref,
                   kr_ref, kiz_ref, krn_ref, h_scr, *, L):
    first = jnp.logical_and(pl.program_id(0) == 0, pl.program_id(1) == 0)

    @pl.when(first)
    def _():
        h = jnp.sin(freq_ref[0:1, :] * (_fdot(feat_ref[...], w1_ref[...]) + b1_ref[...]))
        h = jnp.sin(freq_ref[1:2, :] * (_fdot(h, w2_ref[...]) + b2_ref[...]))
        h = jnp.sin(freq_ref[2:3, :] * (_fdot(h, w3_ref[...]) + b3_ref[...]))
        h_scr[...] = h

    h = h_scr[...]
    window = jnp.exp(-t_ref[...] * delta_ref[...])
    hf = _fdot(h, wof_ref[...]) * window
    hb = _fdot(h, wob_ref[...]) * window
    row = lax.broadcasted_iota(jnp.int32, hf.shape, 0)
    hb = jnp.where(row == 0, 0.0, hb)
    tap_sum = hf + hb
    tap_dif = hb - hf
    kr = _bdot(fwd_ref[0:L, :], tap_sum.astype(BF16))
    ki = _bdot(fwd_ref[L:2 * L, :], tap_dif.astype(BF16))
    sign = jnp.where((row & 1) == 0, 1.0, -1.0)
    k_nyq = jnp.sum(tap_sum * sign, axis=0, keepdims=True)
    kr_ref[0] = kr
    kiz_ref[0] = jnp.where(row == 0, 0.0, ki)
    krn_ref[0] = jnp.where(row == 0, -k_nyq, kr)


def _filter_spectra(L, d, fwd, w1, b1, w2, b2, w3, b3, freq, wout):
    width = w2.shape[0]
    order = wout.shape[1] // (2 * d)
    feat, t = _filter_features(L, width)
    w1p = jnp.pad(w1, ((0, width - w1.shape[0]), (0, 0)))
    deltas = jnp.asarray(np.abs(np.linspace(MIN_DECAY, MAX_DECAY, d))[None, :], F32)
    ct = FILT_CH_TILE
    per = d // ct
    small = lambda shape: pl.BlockSpec(shape, lambda o, j: (0, 0))
    out_spec = pl.BlockSpec((1, L, ct), lambda o, j: (o, 0, j))
    out_shape = jax.ShapeDtypeStruct((order, L, d), F32)
    return pl.pallas_call(
        functools.partial(_filter_kernel, L=L),
        grid=(order, per),
        in_specs=[
            small((L, width)), small((width, width)), small((1, width)),
            small((width, width)), small((1, width)),
            small((width, width)), small((1, width)), small((3, width)),
            pl.BlockSpec((width, ct), lambda o, j: (0, (2 * o) * per + j)),
            pl.BlockSpec((width, ct), lambda o, j: (0, (2 * o + 1) * per + j)),
            small((L, 1)),
            pl.BlockSpec((1, ct), lambda o, j: (0, j)),
            small((2 * L, L)),
        ],
        out_specs=[out_spec, out_spec, out_spec],
        out_shape=[out_shape, out_shape, out_shape],
        scratch_shapes=[pltpu.VMEM((L, width), F32)],
        compiler_params=_params(("arbitrary", "arbitrary")),
        name=f"hyena_filter_{L}",
    )(feat, w1p, b1.reshape(1, width), w2, b2.reshape(1, width), w3, b3.reshape(1, width),
      freq, wout, wout, t, deltas, fwd)


def _longconv_kernel(v_ref, x1_ref, x2_ref, kr_ref, kiz_ref, krn_ref, skip_ref, fwd_ref, inv_ref,
                     o_ref, *, L):
    def conv(u, o):
        spec = _bdot(fwd_ref[...], u.astype(BF16))
        a, b = spec[:L], spec[L:]
        kr, kiz, krn = kr_ref[o], kiz_ref[o], krn_ref[o]
        top = (a * kr + b * kiz).astype(BF16)
        bot = (a * kiz - b * krn).astype(BF16)
        return _bdot(inv_ref[:, 0:L], top) + _bdot(inv_ref[:, L:2 * L], bot)

    v = v_ref[...]
    z = x1_ref[...] * (conv(v, 0) + v * skip_ref[0:1, :])
    z = x2_ref[...] * (conv(z, 1) + z * skip_ref[1:2, :])
    o_ref[...] = z.astype(BF16)


def _longconv(grp, proj, spectra, skip, fwd, inv):
    L, d = grp.seq_len, grp.d
    ct = min(d, CONV_TILE_ELEMS // L)
    per = d // ct
    kr, kiz, krn = spectra
    order = kr.shape[0]
    kspec = pl.BlockSpec((order, L, ct), lambda j, b: (0, 0, j))
    return pl.pallas_call(
        functools.partial(_longconv_kernel, L=L),
        grid=(per, grp.n_seq),
        in_specs=[
            pl.BlockSpec((L, ct), lambda j, b: (b, j)),
            pl.BlockSpec((L, ct), lambda j, b: (b, per + j)),
            pl.BlockSpec((L, ct), lambda j, b: (b, 2 * per + j)),
            kspec, kspec, kspec,
            pl.BlockSpec((order, ct), lambda j, b: (0, j)),
            pl.BlockSpec((2 * L, L), lambda j, b: (0, 0)),
            pl.BlockSpec((L, 2 * L), lambda j, b: (0, 0)),
        ],
        out_specs=pl.BlockSpec((L, ct), lambda j, b: (b, j)),
        out_shape=jax.ShapeDtypeStruct((grp.rows, d), BF16),
        compiler_params=_params(("arbitrary", "arbitrary")),
        name=f"hyena_longconv_{L}",
    )(proj, proj, proj, kr, kiz, krn, skip, fwd, inv)


def _proj_res_kernel(*refs, has_bias):
    if has_bias:
        a_ref, w_ref, b_ref, x_ref, gate_ref, o_ref = refs
    else:
        a_ref, w_ref, x_ref, gate_ref, o_ref = refs
    m = _bdot(a_ref[...], w_ref[...])
    if has_bias:
        m = m + b_ref[...]
    o_ref[...] = x_ref[...] + gate_ref[0] * m


def _proj_res(grp, a, w, b, x, mod3, layer):
    k, n = w.shape
    tn = RES_COL_TILE
    in_specs = [
        pl.BlockSpec((TOKEN_TILE, k), lambda i, j: (i, 0)),
        pl.BlockSpec((k, tn), lambda i, j: (0, j)),
    ]
    args = [a, w]
    if b is not None:
        in_specs.append(pl.BlockSpec((1, tn), lambda i, j: (0, j)))
        args.append(b.reshape(1, n))
    in_specs += [
        pl.BlockSpec((TOKEN_TILE, tn), lambda i, j: (i, j)),
        grp.mod_spec(layer, 2, tn, True),
    ]
    args += [x, mod3]
    return pl.pallas_call(
        functools.partial(_proj_res_kernel, has_bias=b is not None),
        grid=(grp.tiles, n // tn),
        in_specs=in_specs,
        out_specs=pl.BlockSpec((TOKEN_TILE, tn), lambda i, j: (i, j)),
        out_shape=jax.ShapeDtypeStruct((grp.rows, n), F32),
        compiler_params=_params(("arbitrary", "arbitrary")),
        name=f"proj_res_{k}_{grp.seq_len}",
    )(*args)


def _mlp_kernel(*refs, final_norm):
    if final_norm:
        x_ref, g_ref, sc_ref, sh_ref, gate_ref, w1_ref, w2_ref, fg_ref, o_ref, h_scr = refs
    else:
        x_ref, g_ref, sc_ref, sh_ref, gate_ref, w1_ref, w2_ref, o_ref, h_scr = refs
    f = pl.program_id(1)

    @pl.when(f == 0)
    def _():
        _fill_normed(x_ref, g_ref, sc_ref, sh_ref, h_scr, zero_ref=o_ref)

    for r in range(x_ref.shape[0] // MLP_ROWS):
        rows = slice(r * MLP_ROWS, (r + 1) * MLP_ROWS)
        u = _bdot(h_scr[rows, :], w1_ref[0])
        u = jnp.square(jnp.maximum(u, 0.0)).astype(BF16)
        o_ref[rows, :] += _bdot(u, w2_ref[0])

    @pl.when(f == pl.num_programs(1) - 1)
    def _():
        gate = gate_ref[0]

        def fin(r, carry):
            r0 = pl.multiple_of(r * NORM_ROWS, NORM_ROWS)
            y = x_ref[pl.ds(r0, NORM_ROWS), :] + gate * o_ref[pl.ds(r0, NORM_ROWS), :]
            if final_norm:
                ms = jnp.mean(y * y, axis=-1, keepdims=True)
                y = y * lax.rsqrt(ms + EPS) * fg_ref[...]
            o_ref[pl.ds(r0, NORM_ROWS), :] = y
            return carry

        lax.fori_loop(0, x_ref.shape[0] // NORM_ROWS, fin, 0)


def _mlp(grp, x, norm_g, mod3, layer, w1, w2, final_g=None):
    d = grp.d
    ff = w1.shape[2]
    in_specs = [
        pl.BlockSpec((TOKEN_TILE, d), lambda i, f: (i, 0), pipeline_mode=pl.Buffered(1)),
        pl.BlockSpec((1, d), lambda i, f: (0, 0)),
        grp.mod_spec(layer, 4, d, False),
        grp.mod_spec(layer, 3, d, False),
        grp.mod_spec(layer, 5, d, False),
        pl.BlockSpec((1, d, FF_TILE), lambda i, f: (layer, 0, f)),
        pl.BlockSpec((1, FF_TILE, d), lambda i, f: (layer, f, 0)),
    ]
    args = [x, norm_g.reshape(1, d), mod3, mod3, mod3, w1, w2]
    if final_g is not None:
        in_specs.append(pl.BlockSpec((1, d), lambda i, f: (0, 0)))
        args.append(final_g.reshape(1, d))
    return pl.pallas_call(
        functools.partial(_mlp_kernel, final_norm=final_g is not None),
        grid=(grp.tiles, ff // FF_TILE),
        in_specs=in_specs,
        out_specs=pl.BlockSpec((TOKEN_TILE, d), lambda i, f: (i, 0)),
        out_shape=jax.ShapeDtypeStruct((grp.rows, d), F32),
        scratch_shapes=[pltpu.VMEM((TOKEN_TILE, d), BF16)],
        compiler_params=_params(("arbitrary", "arbitrary")),
        name=f"mlp_{layer}_{grp.seq_len}",
    )(*args)


def _rotary_tables(L, dk, width):
    n_rows = L // GRID_W
    rows = np.repeat(np.arange(n_rows, dtype=np.float64), GRID_W)
    cols = np.tile(np.arange(GRID_W, dtype=np.float64), n_rows)
    n_pairs = dk // 4
    inv = ROPE_BASE ** (-np.arange(n_pairs, dtype=np.float64) / n_pairs)
    ang = np.concatenate([rows[:, None] * inv, cols[:, None] * inv], axis=-1)
    cos = np.repeat(np.cos(ang), 2, axis=-1)
    sin = np.repeat(np.sin(ang), 2, axis=-1)
    sin[:, 0::2] *= -1.0
    reps = width // dk
    return (jnp.asarray(np.tile(cos, (1, reps)), F32), jnp.asarray(np.tile(sin, (1, reps)), F32))


def _qkvg_kernel(*refs, rotary, qk_tiles, k_tile0, k_scale):
    if rotary:
        x_ref, g_ref, sc_ref, sh_ref, w_ref, cos_ref, sin_ref, o_ref, h_scr = refs
    else:
        x_ref, g_ref, sc_ref, sh_ref, w_ref, o_ref, h_scr = refs
    j = pl.program_id(1)

    @pl.when(j == 0)
    def _():
        _fill_normed(x_ref, g_ref, sc_ref, sh_ref, h_scr)

    is_qk = j < qk_tiles
    scale = jnp.where(jnp.logical_and(j >= k_tile0, is_qk), k_scale, 1.0)
    if rotary:
        lane = lax.broadcasted_iota(jnp.int32, (h_scr.shape[0], SUB_COLS), 1)
        even = (lane & 1) == 0
        cos = jnp.where(is_qk, cos_ref[...], 1.0)
        sin = jnp.where(is_qk, sin_ref[...], 0.0)
    for s in range(o_ref.shape[1] // SUB_COLS):
        cols = slice(s * SUB_COLS, (s + 1) * SUB_COLS)
        acc = _bdot(h_scr[...], w_ref[:, cols])
        if rotary:
            partner = jnp.where(even, pltpu.roll(acc, SUB_COLS - 1, 1), pltpu.roll(acc, 1, 1))
            acc = acc * cos + partner * sin
        o_ref[:, cols] = (acc * scale).astype(BF16)


def _qkvg(grp, x, norm_g, mod3, layer, w, dk, rotary):
    d = grp.d
    n = w.shape[1]
    in_specs = [
        pl.BlockSpec((TOKEN_TILE, d), lambda i, j: (i, 0)),
        pl.BlockSpec((1, d), lambda i, j: (0, 0)),
        grp.mod_spec(layer, 1, d, False),
        grp.mod_spec(layer, 0, d, False),
        pl.BlockSpec((d, COL_TILE), lambda i, j: (0, j)),
    ]
    args = [x, norm_g.reshape(1, d), mod3, mod3, w]
    if rotary:
        assert grp.seq_len == TOKEN_TILE
        cos, sin = _rotary_tables(grp.seq_len, dk, SUB_COLS)
        table = pl.BlockSpec((TOKEN_TILE, SUB_COLS), lambda i, j: (0, 0))
        in_specs += [table, table]
        args += [cos, sin]
    kern = functools.partial(_qkvg_kernel, rotary=rotary, qk_tiles=2 * d // COL_TILE,
                             k_tile0=d // COL_TILE, k_scale=dk ** -0.5)
    return pl.pallas_call(
        kern,
        grid=(grp.tiles, n // COL_TILE),
        in_specs=in_specs,
        out_specs=pl.BlockSpec((TOKEN_TILE, COL_TILE), lambda i, j: (i, j)),
        out_shape=jax.ShapeDtypeStruct((grp.rows, n), BF16),
        scratch_shapes=[pltpu.VMEM((TOKEN_TILE, d), BF16)],
        compiler_params=_params(("arbitrary", "arbitrary")),
        name=f"ret_qkvg_{grp.seq_len}",
    )(*args)


def _log_sigmoid(x):
    return jnp.minimum(x, 0.0) - jnp.log1p(jnp.exp(-jnp.abs(x)))


def _retention_kernel(*refs, L, C, has_s0, emit_state):
    refs = list(refs)
    decay_ref, q_ref, k_ref, v_ref, g_ref, gn_ref = refs[:6]
    refs = refs[6:]
    s0_ref = refs.pop(0) if has_s0 else None
    y_ref = refs.pop(0)
    so_ref = refs.pop(0) if emit_state else None
    sb_scr = refs.pop(0)

    head = pl.program_id(1)
    dk = q_ref.shape[1]
    dv = v_ref.shape[1]
    nc = L // C

    def log_decay(direction, width):
        return _log_sigmoid(jnp.full((1, width), decay_ref[direction, head], F32))

    ri = lax.broadcasted_iota(jnp.int32, (C, C), 0)
    ci = lax.broadcasted_iota(jnp.int32, (C, C), 1)
    diff = (ri - ci).astype(F32)
    lower = diff >= 0
    upper = diff <= 0
    mask = (jnp.where(lower, jnp.exp(jnp.where(lower, diff, 0.0) * log_decay(0, C)), 0.0)
            + jnp.where(upper, jnp.exp(jnp.where(upper, -diff, 0.0) * log_decay(1, C)), 0.0))
    rk = lax.broadcasted_iota(jnp.int32, (C, dk), 0).astype(F32)
    q_dec_f = jnp.exp((rk + 1.0) * log_decay(0, dk))
    k_dec_f = jnp.exp((C - 1.0 - rk) * log_decay(0, dk))
    q_dec_b = jnp.exp((C - rk) * log_decay(1, dk))
    k_dec_b = jnp.exp(rk * log_decay(1, dk))
    chunk_dec_f = jnp.exp(C * log_decay(0, dv))
    chunk_dec_b = jnp.exp(C * log_decay(1, dv))

    def kv(n, k_dec):
        kd = (k_ref[n * C:(n + 1) * C, :].astype(F32) * k_dec).astype(BF16)
        return lax.dot_general(kd, v_ref[n * C:(n + 1) * C, :], (((0,), (0,)), ((), ())),
                               preferred_element_type=F32)

    def advance(state, dec, upd):
        return upd if state is None else state * dec + upd

    state_b = s0_ref[0, 0, 1, 0] if has_s0 else None
    has_b = [False] * nc
    for n in reversed(range(nc)):
        if state_b is not None:
            sb_scr[n] = state_b.astype(BF16)
            has_b[n] = True
        if n > 0 or emit_state:
            state_b = advance(state_b, chunk_dec_b, kv(n, k_dec_b))
    if emit_state:
        so_ref[0, 0, 1, 0] = state_b

    state_f = s0_ref[0, 0, 0, 0] if has_s0 else None
    gn = gn_ref[...]
    for n in range(nc):
        rows = slice(n * C, (n + 1) * C)
        q = q_ref[rows, :]
        scores = lax.dot_general(q, k_ref[rows, :], (((1,), (1,)), ((), ())),
                                 preferred_element_type=F32)
        y = _bdot((scores * mask).astype(BF16), v_ref[rows, :])
        qf = q.astype(F32)
        if state_f is not None:
            y = y + _bdot((qf * q_dec_f).astype(BF16), state_f.astype(BF16))
        if has_b[n]:
            y = y + _bdot((qf * q_dec_b).astype(BF16), sb_scr[n])
        mu = jnp.mean(y, axis=-1, keepdims=True)
        yc = y - mu
        var = jnp.mean(yc * yc, axis=-1, keepdims=True)
        yn = (yc * lax.rsqrt(var + EPS)) * gn
        gate = g_ref[rows, :].astype(F32)
        y_ref[rows, :] = ((gate * jax.nn.sigmoid(gate)) * yn).astype(BF16)
        if n < nc - 1 or emit_state:
            state_f = advance(state_f, chunk_dec_f, kv(n, k_dec_f))
    if emit_state:
        so_ref[0, 0, 0, 0] = state_f


def _retention(grp, qkvg, decay, gn_g, s0, heads, dk, dv, emit_state):
    L = grp.seq_len
    C = min(L, RET_CHUNK)
    kq = heads
    kv0 = 2 * heads * dk // dv
    kg0 = kv0 + heads
    in_specs = [
        pl.BlockSpec(memory_space=pltpu.SMEM),
        pl.BlockSpec((L, dk), lambda b, h: (b, h)),
        pl.BlockSpec((L, dk), lambda b, h: (b, kq + h)),
        pl.BlockSpec((L, dv), lambda b, h: (b, kv0 + h)),
        pl.BlockSpec((L, dv), lambda b, h: (b, kg0 + h)),
        pl.BlockSpec((1, dv), lambda b, h: (0, h)),
    ]
    args = [decay, qkvg, qkvg, qkvg, qkvg, gn_g.reshape(1, heads * dv)]
    state_block = (1, 1, 2, 1, dk, dv)
    state_index = lambda b, h: (b, 0, 0, h, 0, 0)
    if s0 is not None:
        in_specs.append(pl.BlockSpec(state_block, state_index))
        args.append(s0)
    out_specs = [pl.BlockSpec((L, dv), lambda b, h: (b, h))]
    out_shape = [jax.ShapeDtypeStruct((grp.rows, heads * dv), BF16)]
    if emit_state:
        out_specs.append(pl.BlockSpec(state_block, state_index))
        out_shape.append(jax.ShapeDtypeStruct((grp.n_seq, 1, 2, heads, dk, dv), F32))
    kern = functools.partial(_retention_kernel, L=L, C=C, has_s0=s0 is not None,
                             emit_state=emit_state)
    return pl.pallas_call(
        kern,
        grid=(grp.n_seq, heads),
        in_specs=in_specs,
        out_specs=out_specs,
        out_shape=out_shape,
        scratch_shapes=[pltpu.VMEM((L // C, dk, dv), BF16)],
        compiler_params=_params(("arbitrary", "arbitrary")),
        name=f"retention_{L}",
    )(*args)


def kernel(x_prompt, x_sample, state_ret, c, c_ctx, w_ada, b_ada, norm_g, final_g, hy_w_in, hy_b_in, hy_conv_w, hy_conv_b, hy_f_w1, hy_f_b1, hy_f_w2, hy_f_b2, hy_f_w3, hy_f_b3, hy_f_freq, hy_f_wout, hy_f_skip, hy_w_out, hy_b_out, ret_w_qkvg, ret_decay, ret_gn_g, ret_w_o, mlp_w1, mlp_w2):
    n_prompt, prompt_len, d = x_prompt.shape
    n_sample, sample_len, _ = x_sample.shape
    heads, dk, dv = state_ret.shape[3:]
    depth = w_ada.shape[0]
    assert depth == 2 and hy_w_in.shape[0] == 1 and ret_w_qkvg.shape[0] == 1
    assert n_sample + 1 <= MOD_ROWS

    cond = jnp.concatenate(
        [c_ctx[None, :], c, jnp.zeros((MOD_ROWS - 1 - n_sample, d), F32)], axis=0)
    mod = _ada_mod(cond, w_ada, b_ada)
    mod3 = mod.reshape(depth * MOD_ROWS, 1, mod.shape[-1])

    w_in = hy_w_in[0].astype(BF16)
    w_out = hy_w_out[0].astype(BF16)
    w_qkvg = ret_w_qkvg[0].astype(BF16)
    w_o = ret_w_o[0].astype(BF16)
    w1 = mlp_w1.astype(BF16)
    w2 = mlp_w2.astype(BF16)

    def run(grp, x, s0, latent):
        L = grp.seq_len
        proj = _hyena_in(grp, x, norm_g[0, 0], mod3, 0, w_in, hy_b_in[0], hy_conv_w[0],
                         hy_conv_b[0])
        fwd, inv = _dft_matrices(L)
        spectra = _filter_spectra(L, d, fwd, hy_f_w1[0], hy_f_b1[0], hy_f_w2[0], hy_f_b2[0],
                                  hy_f_w3[0], hy_f_b3[0], hy_f_freq[0], hy_f_wout[0])
        z = _longconv(grp, proj, spectra, hy_f_skip[0], fwd, inv)
        x = _proj_res(grp, z, w_out, hy_b_out[0], x, mod3, 0)
        x = _mlp(grp, x, norm_g[0, 1], mod3, 0, w1, w2)
        qkvg = _qkvg(grp, x, norm_g[1, 0], mod3, 1, w_qkvg, dk, rotary=latent)
        outs = _retention(grp, qkvg, ret_decay[0], ret_gn_g[0], s0, heads, dk, dv,
                          emit_state=s0 is None)
        x = _proj_res(grp, outs[0], w_o, None, x, mod3, 1)
        y = _mlp(grp, x, norm_g[1, 1], mod3, 1, w1, w2, final_g=final_g)
        return y, (outs[1] if s0 is None else None)

    prompt = _Group(n_prompt, prompt_len, d, mod_row0=0, mod_row_step=0)
    sample = _Group(n_sample, sample_len, d, mod_row0=1, mod_row_step=1)
    y_prompt, new_state = run(prompt, x_prompt.reshape(-1, d), None, False)
    y_sample, _ = run(sample, x_sample.reshape(-1, d), state_ret, True)
    return (y_prompt.reshape(x_prompt.shape), y_sample.reshape(x_sample.shape),
            new_state.astype(x_prompt.dtype))
```

```python
import functools
import math

import numpy as np
import jax
import jax.numpy as jnp
from jax import lax
from jax.experimental import pallas as pl
from jax.experimental.pallas import tpu as pltpu

F32 = jnp.float32
BF16 = jnp.bfloat16

EPS = 1e-6
GRID_W = 64
ROPE_BASE = 10000.0
FILTER_BANDS = 16
DECAY_TARGET = 1e-2
MIN_DECAY = math.log(DECAY_TARGET) / 1.5
MAX_DECAY = math.log(DECAY_TARGET) / 0.3

V7X_VMEM_BYTES = 64 * 1024 * 1024
VMEM_LIMIT = V7X_VMEM_BYTES - 8 * 1024 * 1024

TOKEN_TILE = 1024
COL_TILE = 1024
SUB_COLS = 512
RES_COL_TILE = 512
FF_TILE = 512
NORM_ROWS = 128
MLP_ROWS = 256
CONV_TILE_ELEMS = 256 * 1024
CONV_MIN_LANES = 512
FILT_CH_TILE = 512
RET_CHUNK = 256
MOD_ROWS = 8


def _params(semantics):
    return pltpu.CompilerParams(dimension_semantics=semantics, vmem_limit_bytes=VMEM_LIMIT)


def _bdot(a, b):
    return jnp.dot(a, b, preferred_element_type=F32)


def _fdot(a, b):
    return jnp.dot(a, b, preferred_element_type=F32, precision=lax.Precision.HIGHEST)


def _modulated_norm(x, g, scale, shift):
    ms = jnp.mean(x * x, axis=-1, keepdims=True)
    return (x * lax.rsqrt(ms + EPS) * g) * (1.0 + scale) + shift


def _fill_normed(x_ref, g_ref, sc_ref, sh_ref, h_scr, zero_ref=None):
    g = g_ref[...]
    sc = sc_ref[0]
    sh = sh_ref[0]

    def body(r, carry):
        r0 = pl.multiple_of(r * NORM_ROWS, NORM_ROWS)
        x = x_ref[pl.ds(r0, NORM_ROWS), :]
        h_scr[pl.ds(r0, NORM_ROWS), :] = _modulated_norm(x, g, sc, sh).astype(BF16)
        if zero_ref is not None:
            zero_ref[pl.ds(r0, NORM_ROWS), :] = jnp.zeros((NORM_ROWS, zero_ref.shape[1]), F32)
        return carry

    lax.fori_loop(0, x_ref.shape[0] // NORM_ROWS, body, 0)


class _Group:
    def __init__(self, n_seq, seq_len, d, mod_row0, mod_row_step):
        assert seq_len & (seq_len - 1) == 0
        assert (n_seq * seq_len) % TOKEN_TILE == 0
        assert TOKEN_TILE % seq_len == 0 or (mod_row_step == 0 and seq_len % TOKEN_TILE == 0)
        assert mod_row_step == 0 or seq_len == TOKEN_TILE
        self.n_seq, self.seq_len, self.d = n_seq, seq_len, d
        self.rows = n_seq * seq_len
        self.tiles = self.rows // TOKEN_TILE
        self.mod_row0, self.mod_row_step = mod_row0, mod_row_step

    def mod_spec(self, layer, chunk, width, tiled):
        per = self.d // width
        base = layer * MOD_ROWS + self.mod_row0
        step = self.mod_row_step

        def index(i, j):
            return (base + step * i, 0, chunk * per + (j if tiled else 0))

        return pl.BlockSpec((1, 1, width), index)


def _ada_kernel(cond_ref, w_ref, b_ref, o_ref):
    c = cond_ref[...]
    a = (c * jax.nn.sigmoid(c)).astype(BF16)
    o_ref[0] = _bdot(a, w_ref[0].astype(BF16)) + b_ref[0]


def _ada_mod(cond, w_ada, b_ada):
    depth, d, n = w_ada.shape
    tn = 1024
    return pl.pallas_call(
        _ada_kernel,
        grid=(depth, n // tn),
        in_specs=[
            pl.BlockSpec((MOD_ROWS, d), lambda l, j: (0, 0)),
            pl.BlockSpec((1, d, tn), lambda l, j: (l, 0, j)),
            pl.BlockSpec((1, 1, tn), lambda l, j: (l, 0, j)),
        ],
        out_specs=pl.BlockSpec((1, MOD_ROWS, tn), lambda l, j: (l, 0, j)),
        out_shape=jax.ShapeDtypeStruct((depth, MOD_ROWS, n), F32),
        compiler_params=_params(("arbitrary", "arbitrary")),
        name="ada_mod",
    )(cond, w_ada, b_ada.reshape(depth, 1, n))


def _hyena_in_kernel(x_ref, g_ref, sc_ref, sh_ref, w_ref, b_ref, cw_ref, cb_ref, o_ref, h_scr,
                     *, seq_len):
    @pl.when(pl.program_id(1) == 0)
    def _():
        _fill_normed(x_ref, g_ref, sc_ref, sh_ref, h_scr)

    rows = h_scr.shape[0]
    pos = lax.broadcasted_iota(jnp.int32, (rows, SUB_COLS), 0) & (seq_len - 1)
    first = pos == 0
    last = pos == seq_len - 1
    for s in range(o_ref.shape[1] // SUB_COLS):
        cols = slice(s * SUB_COLS, (s + 1) * SUB_COLS)
        p = _bdot(h_scr[...], w_ref[:, cols].astype(BF16)) + b_ref[:, cols]
        prev = jnp.where(first, 0.0, pltpu.roll(p, 1, 0))
        nxt = jnp.where(last, 0.0, pltpu.roll(p, rows - 1, 0))
        o_ref[:, cols] = (prev * cw_ref[0:1, cols] + p * cw_ref[1:2, cols]
                          + nxt * cw_ref[2:3, cols] + cb_ref[:, cols])


def _hyena_in(grp, x, norm_g, mod3, layer, w, b, conv_w, conv_b):
    d = grp.d
    n = w.shape[1]
    return pl.pallas_call(
        functools.partial(_hyena_in_kernel, seq_len=grp.seq_len),
        grid=(grp.tiles, n // COL_TILE),
        in_specs=[
            pl.BlockSpec((TOKEN_TILE, d), lambda i, j: (i, 0)),
            pl.BlockSpec((1, d), lambda i, j: (0, 0)),
            grp.mod_spec(layer, 1, d, False),
            grp.mod_spec(layer, 0, d, False),
            pl.BlockSpec((d, COL_TILE), lambda i, j: (0, j)),
            pl.BlockSpec((1, COL_TILE), lambda i, j: (0, j)),
            pl.BlockSpec((3, COL_TILE), lambda i, j: (0, j)),
            pl.BlockSpec((1, COL_TILE), lambda i, j: (0, j)),
        ],
        out_specs=pl.BlockSpec((TOKEN_TILE, COL_TILE), lambda i, j: (i, j)),
        out_shape=jax.ShapeDtypeStruct((grp.rows, n), F32),
        scratch_shapes=[pltpu.VMEM((TOKEN_TILE, d), BF16)],
        compiler_params=_params(("arbitrary", "arbitrary")),
        name=f"hyena_in_{grp.seq_len}",
    )(x, norm_g.reshape(1, d), mod3, mod3, w, b.reshape(1, n), conv_w, conv_b.reshape(1, n))


def _dft_matrices(L):
    f = np.arange(L, dtype=np.int64)[:, None]
    t = np.arange(L, dtype=np.int64)[None, :]
    ang = ((f * t) % (2 * L)).astype(np.float64) * (np.pi / L)
    cos, sin = np.cos(ang), np.sin(ang)
    nyq = np.where(np.arange(L) % 2 == 0, 1.0, -1.0)
    bot = sin.copy()
    bot[0, :] = nyq
    fwd = np.concatenate([cos, bot], axis=0)
    wgt = np.full((L,), 2.0)
    wgt[0] = 1.0
    inv_cos = (cos * wgt[:, None]).T / (2 * L)
    inv_sin = (-2.0 * sin).T / (2 * L)
    inv_sin[:, 0] = nyq / (2 * L)
    inv = np.concatenate([inv_cos, inv_sin], axis=1)
    return jnp.asarray(fwd, F32).astype(BF16), jnp.asarray(inv, F32).astype(BF16)


def _filter_features(L, width):
    pos = np.arange(L, dtype=np.float64)
    t = pos / L
    omega = 2.0 * math.pi * pos / L
    bands = np.linspace(1e-4, FILTER_BANDS - 1, FILTER_BANDS)
    ang = omega[:, None] * bands[None, :]
    feat = np.concatenate([t[:, None], np.cos(ang), -np.sin(ang)], axis=-1)
    feat = np.pad(feat, ((0, 0), (0, width - feat.shape[1])))
    return jnp.asarray(feat, F32), jnp.asarray(t[:, None], F32)


def _filter_kernel(feat_ref, w1_ref, b1_ref, w2_ref, b2_ref, w3_ref, b3_ref, freq_ref,
                   wof_ref, wob_ref, t_ref, delta_ref, fwd_ref,
                   kr_ref, kiz_ref, krn_ref, h_scr, *, L):
    first = jnp.logical_and(pl.program_id(0) == 0, pl.program_id(1) == 0)

    @pl.when(first)
    def _():
        h = jnp.sin(freq_ref[0:1, :] * (_fdot(feat_ref[...], w1_ref[...]) + b1_ref[...]))
        h = jnp.sin(freq_ref[1:2, :] * (_fdot(h, w2_ref[...]) + b2_ref[...]))
        h = jnp.sin(freq_ref[2:3, :] * (_fdot(h, w3_ref[...]) + b3_ref[...]))
        h_scr[...] = h

    h = h_scr[...]
    window = jnp.exp(-t_ref[...] * delta_ref[...])
    hf = _fdot(h, wof_ref[...]) * window
    hb = _fdot(h, wob_ref[...]) * window
    row = lax.broadcasted_iota(jnp.int32, hf.shape, 0)
    hb = jnp.where(row == 0, 0.0, hb)
    tap_sum = hf + hb
    tap_dif = hb - hf
    kr = _bdot(fwd_ref[0:L, :], tap_sum.astype(BF16))
    ki = _bdot(fwd_ref[L:2 * L, :], tap_dif.astype(BF16))
    sign = jnp.where((row & 1) == 0, 1.0, -1.0)
    k_nyq = jnp.sum(tap_sum * sign, axis=0, keepdims=True)
    kr_ref[0] = kr
    kiz_ref[0] = jnp.where(row == 0, 0.0, ki)
    krn_ref[0] = jnp.where(row == 0, -k_nyq, kr)


def _filter_spectra(L, d, fwd, w1, b1, w2, b2, w3, b3, freq, wout):
    width = w2.shape[0]
    order = wout.shape[1] // (2 * d)
    feat, t = _filter_features(L, width)
    w1p = jnp.pad(w1, ((0, width - w1.shape[0]), (0, 0)))
    deltas = jnp.asarray(np.abs(np.linspace(MIN_DECAY, MAX_DECAY, d))[None, :], F32)
    ct = FILT_CH_TILE
    per = d // ct
    small = lambda shape: pl.BlockSpec(shape, lambda o, j: (0, 0))
    out_spec = pl.BlockSpec((1, L, ct), lambda o, j: (o, 0, j))
    out_shape = jax.ShapeDtypeStruct((order, L, d), F32)
    return pl.pallas_call(
        functools.partial(_filter_kernel, L=L),
        grid=(order, per),
        in_specs=[
            small((L, width)), small((width, width)), small((1, width)),
            small((width, width)), small((1, width)),
            small((width, width)), small((1, width)), small((3, width)),
            pl.BlockSpec((width, ct), lambda o, j: (0, (2 * o) * per + j)),
            pl.BlockSpec((width, ct), lambda o, j: (0, (2 * o + 1) * per + j)),
            small((L, 1)),
            pl.BlockSpec((1, ct), lambda o, j: (0, j)),
            small((2 * L, L)),
        ],
        out_specs=[out_spec, out_spec, out_spec],
        out_shape=[out_shape, out_shape, out_shape],
        scratch_shapes=[pltpu.VMEM((L, width), F32)],
        compiler_params=_params(("arbitrary", "arbitrary")),
        name=f"hyena_filter_{L}",
    )(feat, w1p, b1.reshape(1, width), w2, b2.reshape(1, width), w3, b3.reshape(1, width),
      freq, wout, wout, t, deltas, fwd)


def _longconv_kernel(v_ref, x1_ref, x2_ref, kr_ref, kiz_ref, krn_ref, skip_ref, fwd_ref, inv_ref,
                     o_ref, *, L, pack):
    def lanes(ref):
        return jnp.concatenate([ref[s * L:(s + 1) * L, :] for s in range(pack)], axis=1)

    def tiled(x):
        return jnp.concatenate([x] * pack, axis=1)

    def conv(u, o):
        spec = _bdot(fwd_ref[...], u.astype(BF16))
        a, b = spec[:L], spec[L:]
        kr, kiz, krn = tiled(kr_ref[o]), tiled(kiz_ref[o]), tiled(krn_ref[o])
        top = (a * kr + b * kiz).astype(BF16)
        bot = (a * kiz - b * krn).astype(BF16)
        return _bdot(inv_ref[:, 0:L], top) + _bdot(inv_ref[:, L:2 * L], bot)

    v = lanes(v_ref)
    z = lanes(x1_ref) * (conv(v, 0) + v * tiled(skip_ref[0:1, :]))
    z = lanes(x2_ref) * (conv(z, 1) + z * tiled(skip_ref[1:2, :]))
    ct = o_ref.shape[1]
    for s in range(pack):
        o_ref[s * L:(s + 1) * L, :] = z[:, s * ct:(s + 1) * ct].astype(BF16)


def _longconv(grp, proj, spectra, skip, fwd, inv):
    L, d = grp.seq_len, grp.d
    ct = min(d, CONV_TILE_ELEMS // L)
    pack = max(1, CONV_MIN_LANES // ct)
    assert grp.n_seq % pack == 0
    per = d // ct
    kr, kiz, krn = spectra
    order = kr.shape[0]
    kspec = pl.BlockSpec((order, L, ct), lambda j, b: (0, 0, j))
    return pl.pallas_call(
        functools.partial(_longconv_kernel, L=L, pack=pack),
        grid=(per, grp.n_seq // pack),
        in_specs=[
            pl.BlockSpec((pack * L, ct), lambda j, b: (b, j)),
            pl.BlockSpec((pack * L, ct), lambda j, b: (b, per + j)),
            pl.BlockSpec((pack * L, ct), lambda j, b: (b, 2 * per + j)),
            kspec, kspec, kspec,
            pl.BlockSpec((order, ct), lambda j, b: (0, j)),
            pl.BlockSpec((2 * L, L), lambda j, b: (0, 0)),
            pl.BlockSpec((L, 2 * L), lambda j, b: (0, 0)),
        ],
        out_specs=pl.BlockSpec((pack * L, ct), lambda j, b: (b, j)),
        out_shape=jax.ShapeDtypeStruct((grp.rows, d), BF16),
        compiler_params=_params(("arbitrary", "arbitrary")),
        name=f"hyena_longconv_{L}",
    )(proj, proj, proj, kr, kiz, krn, skip, fwd, inv)


def _proj_res_kernel(*refs, has_bias):
    if has_bias:
        a_ref, w_ref, b_ref, x_ref, gate_ref, o_ref = refs
    else:
        a_ref, w_ref, x_ref, gate_ref, o_ref = refs
    m = _bdot(a_ref[...], w_ref[...].astype(BF16))
    if has_bias:
        m = m + b_ref[...]
    o_ref[...] = x_ref[...] + gate_ref[0] * m


def _proj_res(grp, a, w, b, x, mod3, layer):
    k, n = w.shape
    tn = RES_COL_TILE
    in_specs = [
        pl.BlockSpec((TOKEN_TILE, k), lambda i, j: (i, 0)),
        pl.BlockSpec((k, tn), lambda i, j: (0, j)),
    ]
    args = [a, w]
    if b is not None:
        in_specs.append(pl.BlockSpec((1, tn), lambda i, j: (0, j)))
        args.append(b.reshape(1, n))
    in_specs += [
        pl.BlockSpec((TOKEN_TILE, tn), lambda i, j: (i, j)),
        grp.mod_spec(layer, 2, tn, True),
    ]
    args += [x, mod3]
    return pl.pallas_call(
        functools.partial(_proj_res_kernel, has_bias=b is not None),
        grid=(grp.tiles, n // tn),
        in_specs=in_specs,
        out_specs=pl.BlockSpec((TOKEN_TILE, tn), lambda i, j: (i, j)),
        out_shape=jax.ShapeDtypeStruct((grp.rows, n), F32),
        compiler_params=_params(("arbitrary", "arbitrary")),
        name=f"proj_res_{k}_{grp.seq_len}",
    )(*args)


def _mlp_kernel(*refs, final_norm):
    if final_norm:
        x_ref, g_ref, sc_ref, sh_ref, gate_ref, w1_ref, w2_ref, fg_ref, o_ref, h_scr = refs
    else:
        x_ref, g_ref, sc_ref, sh_ref, gate_ref, w1_ref, w2_ref, o_ref, h_scr = refs
    f = pl.program_id(1)

    @pl.when(f == 0)
    def _():
        _fill_normed(x_ref, g_ref, sc_ref, sh_ref, h_scr, zero_ref=o_ref)

    w1 = w1_ref[0].astype(BF16)
    w2 = w2_ref[0].astype(BF16)
    for r in range(x_ref.shape[0] // MLP_ROWS):
        rows = slice(r * MLP_ROWS, (r + 1) * MLP_ROWS)
        u = _bdot(h_scr[rows, :], w1)
        u = jnp.square(jnp.maximum(u, 0.0)).astype(BF16)
        o_ref[rows, :] += _bdot(u, w2)

    @pl.when(f == pl.num_programs(1) - 1)
    def _():
        gate = gate_ref[0]

        def fin(r, carry):
            r0 = pl.multiple_of(r * NORM_ROWS, NORM_ROWS)
            y = x_ref[pl.ds(r0, NORM_ROWS), :] + gate * o_ref[pl.ds(r0, NORM_ROWS), :]
            if final_norm:
                ms = jnp.mean(y * y, axis=-1, keepdims=True)
                y = y * lax.rsqrt(ms + EPS) * fg_ref[...]
            o_ref[pl.ds(r0, NORM_ROWS), :] = y
            return carry

        lax.fori_loop(0, x_ref.shape[0] // NORM_ROWS, fin, 0)


def _mlp(grp, x, norm_g, mod3, layer, w1, w2, final_g=None):
    d = grp.d
    ff = w1.shape[2]
    in_specs = [
        pl.BlockSpec((TOKEN_TILE, d), lambda i, f: (i, 0), pipeline_mode=pl.Buffered(1)),
        pl.BlockSpec((1, d), lambda i, f: (0, 0)),
        grp.mod_spec(layer, 4, d, False),
        grp.mod_spec(layer, 3, d, False),
        grp.mod_spec(layer, 5, d, False),
        pl.BlockSpec((1, d, FF_TILE), lambda i, f: (layer, 0, f)),
        pl.BlockSpec((1, FF_TILE, d), lambda i, f: (layer, f, 0)),
    ]
    args = [x, norm_g.reshape(1, d), mod3, mod3, mod3, w1, w2]
    if final_g is not None:
        in_specs.append(pl.BlockSpec((1, d), lambda i, f: (0, 0)))
        args.append(final_g.reshape(1, d))
    return pl.pallas_call(
        functools.partial(_mlp_kernel, final_norm=final_g is not None),
        grid=(grp.tiles, ff // FF_TILE),
        in_specs=in_specs,
        out_specs=pl.BlockSpec((TOKEN_TILE, d), lambda i, f: (i, 0)),
        out_shape=jax.ShapeDtypeStruct((grp.rows, d), F32),
        scratch_shapes=[pltpu.VMEM((TOKEN_TILE, d), BF16)],
        compiler_params=_params(("arbitrary", "arbitrary")),
        name=f"mlp_{layer}_{grp.seq_len}",
    )(*args)


def _rotary_tables(L, dk, width):
    n_rows = L // GRID_W
    rows = np.repeat(np.arange(n_rows, dtype=np.float64), GRID_W)
    cols = np.tile(np.arange(GRID_W, dtype=np.float64), n_rows)
    n_pairs = dk // 4
    inv = ROPE_BASE ** (-np.arange(n_pairs, dtype=np.float64) / n_pairs)
    ang = np.concatenate([rows[:, None] * inv, cols[:, None] * inv], axis=-1)
    cos = np.repeat(np.cos(ang), 2, axis=-1)
    sin = np.repeat(np.sin(ang), 2, axis=-1)
    sin[:, 0::2] *= -1.0
    reps = width // dk
    return (jnp.asarray(np.tile(cos, (1, reps)), F32), jnp.asarray(np.tile(sin, (1, reps)), F32))


def _qkvg_kernel(*refs, rotary, qk_tiles, k_tile0, k_scale):
    if rotary:
        x_ref, g_ref, sc_ref, sh_ref, w_ref, cos_ref, sin_ref, o_ref, h_scr = refs
    else:
        x_ref, g_ref, sc_ref, sh_ref, w_ref, o_ref, h_scr = refs
    j = pl.program_id(1)

    @pl.when(j == 0)
    def _():
        _fill_normed(x_ref, g_ref, sc_ref, sh_ref, h_scr)

    is_qk = j < qk_tiles
    scale = jnp.where(jnp.logical_and(j >= k_tile0, is_qk), k_scale, 1.0)
    if rotary:
        lane = lax.broadcasted_iota(jnp.int32, (h_scr.shape[0], SUB_COLS), 1)
        even = (lane & 1) == 0
        cos = jnp.where(is_qk, cos_ref[...], 1.0)
        sin = jnp.where(is_qk, sin_ref[...], 0.0)
    for s in range(o_ref.shape[1] // SUB_COLS):
        cols = slice(s * SUB_COLS, (s + 1) * SUB_COLS)
        acc = _bdot(h_scr[...], w_ref[:, cols].astype(BF16))
        if rotary:
            partner = jnp.where(even, pltpu.roll(acc, SUB_COLS - 1, 1), pltpu.roll(acc, 1, 1))
            acc = acc * cos + partner * sin
        o_ref[:, cols] = (acc * scale).astype(BF16)


def _qkvg(grp, x, norm_g, mod3, layer, w, dk, rotary):
    d = grp.d
    n = w.shape[1]
    in_specs = [
        pl.BlockSpec((TOKEN_TILE, d), lambda i, j: (i, 0)),
        pl.BlockSpec((1, d), lambda i, j: (0, 0)),
        grp.mod_spec(layer, 1, d, False),
        grp.mod_spec(layer, 0, d, False),
        pl.BlockSpec((d, COL_TILE), lambda i, j: (0, j)),
    ]
    args = [x, norm_g.reshape(1, d), mod3, mod3, w]
    if rotary:
        assert grp.seq_len == TOKEN_TILE
        cos, sin = _rotary_tables(grp.seq_len, dk, SUB_COLS)
        table = pl.BlockSpec((TOKEN_TILE, SUB_COLS), lambda i, j: (0, 0))
        in_specs += [table, table]
        args += [cos, sin]
    kern = functools.partial(_qkvg_kernel, rotary=rotary, qk_tiles=2 * d // COL_TILE,
                             k_tile0=d // COL_TILE, k_scale=dk ** -0.5)
    return pl.pallas_call(
        kern,
        grid=(grp.tiles, n // COL_TILE),
        in_specs=in_specs,
        out_specs=pl.BlockSpec((TOKEN_TILE, COL_TILE), lambda i, j: (i, j)),
        out_shape=jax.ShapeDtypeStruct((grp.rows, n), BF16),
        scratch_shapes=[pltpu.VMEM((TOKEN_TILE, d), BF16)],
        compiler_params=_params(("arbitrary", "arbitrary")),
        name=f"ret_qkvg_{grp.seq_len}",
    )(*args)


def _log_sigmoid(x):
    return jnp.minimum(x, 0.0) - jnp.log1p(jnp.exp(-jnp.abs(x)))


def _retention_kernel(*refs, L, C, has_s0, emit_state):
    refs = list(refs)
    decay_ref, q_ref, k_ref, v_ref, g_ref, gn_ref = refs[:6]
    refs = refs[6:]
    s0_ref = refs.pop(0) if has_s0 else None
    y_ref = refs.pop(0)
    so_ref = refs.pop(0) if emit_state else None
    sb_scr = refs.pop(0)

    head = pl.program_id(1)
    dk = q_ref.shape[1]
    dv = v_ref.shape[1]
    nc = L // C

    def log_decay(direction, width):
        return _log_sigmoid(jnp.full((1, width), decay_ref[direction, head], F32))

    ri = lax.broadcasted_iota(jnp.int32, (C, C), 0)
    ci = lax.broadcasted_iota(jnp.int32, (C, C), 1)
    diff = (ri - ci).astype(F32)
    lower = diff >= 0
    upper = diff <= 0
    mask = (jnp.where(lower, jnp.exp(jnp.where(lower, diff, 0.0) * log_decay(0, C)), 0.0)
            + jnp.where(upper, jnp.exp(jnp.where(upper, -diff, 0.0) * log_decay(1, C)), 0.0))
    rk = lax.broadcasted_iota(jnp.int32, (C, dk), 0).astype(F32)
    q_dec_f = jnp.exp((rk + 1.0) * log_decay(0, dk))
    k_dec_f = jnp.exp((C - 1.0 - rk) * log_decay(0, dk))
    q_dec_b = jnp.exp((C - rk) * log_decay(1, dk))
    k_dec_b = jnp.exp(rk * log_decay(1, dk))
    chunk_dec_f = jnp.exp(C * log_decay(0, dv))
    chunk_dec_b = jnp.exp(C * log_decay(1, dv))

    def kv(n, k_dec):
        kd = (k_ref[n * C:(n + 1) * C, :].astype(F32) * k_dec).astype(BF16)
        return lax.dot_general(kd, v_ref[n * C:(n + 1) * C, :], (((0,), (0,)), ((), ())),
                               preferred_element_type=F32)

    def advance(state, dec, upd):
        return upd if state is None else state * dec + upd

    state_b = s0_ref[0, 0, 1, 0] if has_s0 else None
    has_b = [False] * nc
    for n in reversed(range(nc)):
        if state_b is not None:
            sb_scr[n] = state_b.astype(BF16)
            has_b[n] = True
        if n > 0 or emit_state:
            state_b = advance(state_b, chunk_dec_b, kv(n, k_dec_b))
    if emit_state:
        so_ref[0, 0, 1, 0] = state_b

    state_f = s0_ref[0, 0, 0, 0] if has_s0 else None
    gn = gn_ref[...]
    for n in range(nc):
        rows = slice(n * C, (n + 1) * C)
        q = q_ref[rows, :]
        scores = lax.dot_general(q, k_ref[rows, :], (((1,), (1,)), ((), ())),
                                 preferred_element_type=F32)
        y = _bdot((scores * mask).astype(BF16), v_ref[rows, :])
        qf = q.astype(F32)
        if state_f is not None:
            y = y + _bdot((qf * q_dec_f).astype(BF16), state_f.astype(BF16))
        if has_b[n]:
            y = y + _bdot((qf * q_dec_b).astype(BF16), sb_scr[n])
        mu = jnp.mean(y, axis=-1, keepdims=True)
        yc = y - mu
        var = jnp.mean(yc * yc, axis=-1, keepdims=True)
        yn = (yc * lax.rsqrt(var + EPS)) * gn
        gate = g_ref[rows, :].astype(F32)
        y_ref[rows, :] = ((gate * jax.nn.sigmoid(gate)) * yn).astype(BF16)
        if n < nc - 1 or emit_state:
            state_f = advance(state_f, chunk_dec_f, kv(n, k_dec_f))
    if emit_state:
        so_ref[0, 0, 0, 0] = state_f


def _retention(grp, qkvg, decay, gn_g, s0, heads, dk, dv, emit_state):
    L = grp.seq_len
    C = min(L, RET_CHUNK)
    kq = heads
    kv0 = 2 * heads * dk // dv
    kg0 = kv0 + heads
    in_specs = [
        pl.BlockSpec(memory_space=pltpu.SMEM),
        pl.BlockSpec((L, dk), lambda b, h: (b, h)),
        pl.BlockSpec((L, dk), lambda b, h: (b, kq + h)),
        pl.BlockSpec((L, dv), lambda b, h: (b, kv0 + h)),
        pl.BlockSpec((L, dv), lambda b, h: (b, kg0 + h)),
        pl.BlockSpec((1, dv), lambda b, h: (0, h)),
    ]
    args = [decay, qkvg, qkvg, qkvg, qkvg, gn_g.reshape(1, heads * dv)]
    state_block = (1, 1, 2, 1, dk, dv)
    state_index = lambda b, h: (b, 0, 0, h, 0, 0)
    if s0 is not None:
        in_specs.append(pl.BlockSpec(state_block, state_index))
        args.append(s0)
    out_specs = [pl.BlockSpec((L, dv), lambda b, h: (b, h))]
    out_shape = [jax.ShapeDtypeStruct((grp.rows, heads * dv), BF16)]
    if emit_state:
        out_specs.append(pl.BlockSpec(state_block, state_index))
        out_shape.append(jax.ShapeDtypeStruct((grp.n_seq, 1, 2, heads, dk, dv), F32))
    kern = functools.partial(_retention_kernel, L=L, C=C, has_s0=s0 is not None,
                             emit_state=emit_state)
    return pl.pallas_call(
        kern,
        grid=(grp.n_seq, heads),
        in_specs=in_specs,
        out_specs=out_specs,
        out_shape=out_shape,
        scratch_shapes=[pltpu.VMEM((L // C, dk, dv), BF16)],
        compiler_params=_params(("arbitrary", "arbitrary")),
        name=f"retention_{L}",
    )(*args)


def kernel(x_prompt, x_sample, state_ret, c, c_ctx, w_ada, b_ada, norm_g, final_g, hy_w_in, hy_b_in, hy_conv_w, hy_conv_b, hy_f_w1, hy_f_b1, hy_f_w2, hy_f_b2, hy_f_w3, hy_f_b3, hy_f_freq, hy_f_wout, hy_f_skip, hy_w_out, hy_b_out, ret_w_qkvg, ret_decay, ret_gn_g, ret_w_o, mlp_w1, mlp_w2):
    n_prompt, prompt_len, d = x_prompt.shape
    n_sample, sample_len, _ = x_sample.shape
    heads, dk, dv = state_ret.shape[3:]
    depth = w_ada.shape[0]
    assert depth == 2 and hy_w_in.shape[0] == 1 and ret_w_qkvg.shape[0] == 1
    assert n_sample + 1 <= MOD_ROWS

    cond = jnp.concatenate(
        [c_ctx[None, :], c, jnp.zeros((MOD_ROWS - 1 - n_sample, d), F32)], axis=0)
    mod = _ada_mod(cond, w_ada, b_ada)
    mod3 = mod.reshape(depth * MOD_ROWS, 1, mod.shape[-1])

    w_in, w_out, w_qkvg, w_o, w1, w2 = (hy_w_in[0], hy_w_out[0], ret_w_qkvg[0], ret_w_o[0],
                                        mlp_w1, mlp_w2)

    def run(grp, x, s0, latent):
        L = grp.seq_len
        proj = _hyena_in(grp, x, norm_g[0, 0], mod3, 0, w_in, hy_b_in[0], hy_conv_w[0],
                         hy_conv_b[0])
        fwd, inv = _dft_matrices(L)
        spectra = _filter_spectra(L, d, fwd, hy_f_w1[0], hy_f_b1[0], hy_f_w2[0], hy_f_b2[0],
                                  hy_f_w3[0], hy_f_b3[0], hy_f_freq[0], hy_f_wout[0])
        z = _longconv(grp, proj, spectra, hy_f_skip[0], fwd, inv)
        x = _proj_res(grp, z, w_out, hy_b_out[0], x, mod3, 0)
        x = _mlp(grp, x, norm_g[0, 1], mod3, 0, w1, w2)
        qkvg = _qkvg(grp, x, norm_g[1, 0], mod3, 1, w_qkvg, dk, rotary=latent)
        outs = _retention(grp, qkvg, ret_decay[0], ret_gn_g[0], s0, heads, dk, dv,
                          emit_state=s0 is None)
        x = _proj_res(grp, outs[0], w_o, None, x, mod3, 1)
        y = _mlp(grp, x, norm_g[1, 1], mod3, 1, w1, w2, final_g=final_g)
        return y, (outs[1] if s0 is None else None)

    prompt = _Group(n_prompt, prompt_len, d, mod_row0=0, mod_row_step=0)
    sample = _Group(n_sample, sample_len, d, mod_row0=1, mod_row_step=1)
    y_prompt, new_state = run(prompt, x_prompt.reshape(-1, d), None, False)
    y_sample, _ = run(sample, x_sample.reshape(-1, d), state_ret, True)
    return (y_prompt.reshape(x_prompt.shape), y_sample.reshape(x_sample.shape),
            new_state.astype(x_prompt.dtype))
```

```python
import functools
import math

import numpy as np
import jax
import jax.numpy as jnp
from jax import lax
from jax.experimental import pallas as pl
from jax.experimental.pallas import tpu as pltpu

F32 = jnp.float32
BF16 = jnp.bfloat16

EPS = 1e-6
GRID_W = 64
ROPE_BASE = 10000.0
FILTER_BANDS = 16
DECAY_TARGET = 1e-2
MIN_DECAY = math.log(DECAY_TARGET) / 1.5
MAX_DECAY = math.log(DECAY_TARGET) / 0.3

V7X_VMEM_BYTES = 64 * 1024 * 1024
VMEM_LIMIT = V7X_VMEM_BYTES - 8 * 1024 * 1024

TOKEN_TILE = 1024
COL_TILE = 1024
SUB_COLS = 512
RES_COL_TILE = 512
FF_TILE = 512
NORM_ROWS = 128
MLP_ROWS = 256
CONV_TILE_ELEMS = 256 * 1024
CONV_MIN_LANES = 512
FILT_CH_TILE = 512
RET_CHUNK = 256
RET_STEP_ROWS = 1024
MOD_ROWS = 8


def _params(semantics):
    return pltpu.CompilerParams(dimension_semantics=semantics, vmem_limit_bytes=VMEM_LIMIT)


def _bdot(a, b):
    return jnp.dot(a, b, preferred_element_type=F32)


def _fdot(a, b):
    return jnp.dot(a, b, preferred_element_type=F32, precision=lax.Precision.HIGHEST)


def _modulated_norm(x, g, scale, shift):
    ms = jnp.mean(x * x, axis=-1, keepdims=True)
    return (x * lax.rsqrt(ms + EPS) * g) * (1.0 + scale) + shift


def _fill_normed(x_ref, g_ref, sc_ref, sh_ref, h_scr, zero_ref=None):
    g = g_ref[...]
    sc = sc_ref[0]
    sh = sh_ref[0]

    def body(r, carry):
        r0 = pl.multiple_of(r * NORM_ROWS, NORM_ROWS)
        x = x_ref[pl.ds(r0, NORM_ROWS), :]
        h_scr[pl.ds(r0, NORM_ROWS), :] = _modulated_norm(x, g, sc, sh).astype(BF16)
        if zero_ref is not None:
            zero_ref[pl.ds(r0, NORM_ROWS), :] = jnp.zeros((NORM_ROWS, zero_ref.shape[1]), F32)
        return carry

    lax.fori_loop(0, x_ref.shape[0] // NORM_ROWS, body, 0)


class _Group:
    def __init__(self, n_seq, seq_len, d, mod_row0, mod_row_step):
        assert seq_len & (seq_len - 1) == 0
        assert (n_seq * seq_len) % TOKEN_TILE == 0
        assert TOKEN_TILE % seq_len == 0 or (mod_row_step == 0 and seq_len % TOKEN_TILE == 0)
        assert mod_row_step == 0 or seq_len == TOKEN_TILE
        self.n_seq, self.seq_len, self.d = n_seq, seq_len, d
        self.rows = n_seq * seq_len
        self.tiles = self.rows // TOKEN_TILE
        self.mod_row0, self.mod_row_step = mod_row0, mod_row_step

    def mod_spec(self, layer, chunk, width, tiled):
        per = self.d // width
        base = layer * MOD_ROWS + self.mod_row0
        step = self.mod_row_step

        def index(i, j):
            return (base + step * i, 0, chunk * per + (j if tiled else 0))

        return pl.BlockSpec((1, 1, width), index)


def _ada_kernel(cond_ref, w_ref, b_ref, o_ref):
    c = cond_ref[...]
    a = (c * jax.nn.sigmoid(c)).astype(BF16)
    o_ref[0] = _bdot(a, w_ref[0].astype(BF16)) + b_ref[0]


def _ada_mod(cond, w_ada, b_ada):
    depth, d, n = w_ada.shape
    tn = 1024
    return pl.pallas_call(
        _ada_kernel,
        grid=(depth, n // tn),
        in_specs=[
            pl.BlockSpec((MOD_ROWS, d), lambda l, j: (0, 0)),
            pl.BlockSpec((1, d, tn), lambda l, j: (l, 0, j)),
            pl.BlockSpec((1, 1, tn), lambda l, j: (l, 0, j)),
        ],
        out_specs=pl.BlockSpec((1, MOD_ROWS, tn), lambda l, j: (l, 0, j)),
        out_shape=jax.ShapeDtypeStruct((depth, MOD_ROWS, n), F32),
        compiler_params=_params(("arbitrary", "arbitrary")),
        name="ada_mod",
    )(cond, w_ada, b_ada.reshape(depth, 1, n))


def _hyena_in_kernel(x_ref, g_ref, sc_ref, sh_ref, w_ref, b_ref, cw_ref, cb_ref, o_ref, h_scr,
                     *, seq_len):
    @pl.when(pl.program_id(1) == 0)
    def _():
        _fill_normed(x_ref, g_ref, sc_ref, sh_ref, h_scr)

    rows = h_scr.shape[0]
    pos = lax.broadcasted_iota(jnp.int32, (rows, SUB_COLS), 0) & (seq_len - 1)
    first = pos == 0
    last = pos == seq_len - 1
    for s in range(o_ref.shape[1] // SUB_COLS):
        cols = slice(s * SUB_COLS, (s + 1) * SUB_COLS)
        p = _bdot(h_scr[...], w_ref[:, cols].astype(BF16)) + b_ref[:, cols]
        prev = jnp.where(first, 0.0, pltpu.roll(p, 1, 0))
        nxt = jnp.where(last, 0.0, pltpu.roll(p, rows - 1, 0))
        o_ref[:, cols] = (prev * cw_ref[0:1, cols] + p * cw_ref[1:2, cols]
                          + nxt * cw_ref[2:3, cols] + cb_ref[:, cols])


def _hyena_in(grp, x, norm_g, mod3, layer, w, b, conv_w, conv_b):
    d = grp.d
    n = w.shape[1]
    return pl.pallas_call(
        functools.partial(_hyena_in_kernel, seq_len=grp.seq_len),
        grid=(grp.tiles, n // COL_TILE),
        in_specs=[
            pl.BlockSpec((TOKEN_TILE, d), lambda i, j: (i, 0)),
            pl.BlockSpec((1, d), lambda i, j: (0, 0)),
            grp.mod_spec(layer, 1, d, False),
            grp.mod_spec(layer, 0, d, False),
            pl.BlockSpec((d, COL_TILE), lambda i, j: (0, j)),
            pl.BlockSpec((1, COL_TILE), lambda i, j: (0, j)),
            pl.BlockSpec((3, COL_TILE), lambda i, j: (0, j)),
            pl.BlockSpec((1, COL_TILE), lambda i, j: (0, j)),
        ],
        out_specs=pl.BlockSpec((TOKEN_TILE, COL_TILE), lambda i, j: (i, j)),
        out_shape=jax.ShapeDtypeStruct((grp.rows, n), F32),
        scratch_shapes=[pltpu.VMEM((TOKEN_TILE, d), BF16)],
        compiler_params=_params(("arbitrary", "arbitrary")),
        name=f"hyena_in_{grp.seq_len}",
    )(x, norm_g.reshape(1, d), mod3, mod3, w, b.reshape(1, n), conv_w, conv_b.reshape(1, n))


def _dft_matrices(L):
    f = np.arange(L, dtype=np.int64)[:, None]
    t = np.arange(L, dtype=np.int64)[None, :]
    ang = ((f * t) % (2 * L)).astype(np.float64) * (np.pi / L)
    cos, sin = np.cos(ang), np.sin(ang)
    nyq = np.where(np.arange(L) % 2 == 0, 1.0, -1.0)
    bot = sin.copy()
    bot[0, :] = nyq
    fwd = np.concatenate([cos, bot], axis=0)
    wgt = np.full((L,), 2.0)
    wgt[0] = 1.0
    inv_cos = (cos * wgt[:, None]).T / (2 * L)
    inv_sin = (-2.0 * sin).T / (2 * L)
    inv_sin[:, 0] = nyq / (2 * L)
    inv = np.concatenate([inv_cos, inv_sin], axis=1)
    return jnp.asarray(fwd, F32).astype(BF16), jnp.asarray(inv, F32).astype(BF16)


def _filter_features(L, width):
    pos = np.arange(L, dtype=np.float64)
    t = pos / L
    omega = 2.0 * math.pi * pos / L
    bands = np.linspace(1e-4, FILTER_BANDS - 1, FILTER_BANDS)
    ang = omega[:, None] * bands[None, :]
    feat = np.concatenate([t[:, None], np.cos(ang), -np.sin(ang)], axis=-1)
    feat = np.pad(feat, ((0, 0), (0, width - feat.shape[1])))
    return jnp.asarray(feat, F32), jnp.asarray(t[:, None], F32)


def _filter_kernel(feat_ref, w1_ref, b1_ref, w2_ref, b2_ref, w3_ref, b3_ref, freq_ref,
                   wof_ref, wob_ref, t_ref, delta_ref, fwd_ref,
                   kr_ref, kiz_ref, krn_ref, h_scr, *, L):
    first = jnp.logical_and(pl.program_id(0) == 0, pl.program_id(1) == 0)

    @pl.when(first)
    def _():
        h = jnp.sin(freq_ref[0:1, :] * (_fdot(feat_ref[...], w1_ref[...]) + b1_ref[...]))
        h = jnp.sin(freq_ref[1:2, :] * (_fdot(h, w2_ref[...]) + b2_ref[...]))
        h = jnp.sin(freq_ref[2:3, :] * (_fdot(h, w3_ref[...]) + b3_ref[...]))
        h_scr[...] = h

    h = h_scr[...]
    window = jnp.exp(-t_ref[...] * delta_ref[...])
    hf = _fdot(h, wof_ref[...]) * window
    hb = _fdot(h, wob_ref[...]) * window
    row = lax.broadcasted_iota(jnp.int32, hf.shape, 0)
    hb = jnp.where(row == 0, 0.0, hb)
    tap_sum = hf + hb
    tap_dif = hb - hf
    kr = _bdot(fwd_ref[0:L, :], tap_sum.astype(BF16))
    ki = _bdot(fwd_ref[L:2 * L, :], tap_dif.astype(BF16))
    sign = jnp.where((row & 1) == 0, 1.0, -1.0)
    k_nyq = jnp.sum(tap_sum * sign, axis=0, keepdims=True)
    kr_ref[0] = kr
    kiz_ref[0] = jnp.where(row == 0, 0.0, ki)
    krn_ref[0] = jnp.where(row == 0, -k_nyq, kr)


def _filter_spectra(L, d, fwd, w1, b1, w2, b2, w3, b3, freq, wout):
    width = w2.shape[0]
    order = wout.shape[1] // (2 * d)
    feat, t = _filter_features(L, width)
    w1p = jnp.pad(w1, ((0, width - w1.shape[0]), (0, 0)))
    deltas = jnp.asarray(np.abs(np.linspace(MIN_DECAY, MAX_DECAY, d))[None, :], F32)
    ct = FILT_CH_TILE
    per = d // ct
    small = lambda shape: pl.BlockSpec(shape, lambda o, j: (0, 0))
    out_spec = pl.BlockSpec((1, L, ct), lambda o, j: (o, 0, j))
    out_shape = jax.ShapeDtypeStruct((order, L, d), F32)
    return pl.pallas_call(
        functools.partial(_filter_kernel, L=L),
        grid=(order, per),
        in_specs=[
            small((L, width)), small((width, width)), small((1, width)),
            small((width, width)), small((1, width)),
            small((width, width)), small((1, width)), small((3, width)),
            pl.BlockSpec((width, ct), lambda o, j: (0, (2 * o) * per + j)),
            pl.BlockSpec((width, ct), lambda o, j: (0, (2 * o + 1) * per + j)),
            small((L, 1)),
            pl.BlockSpec((1, ct), lambda o, j: (0, j)),
            small((2 * L, L)),
        ],
        out_specs=[out_spec, out_spec, out_spec],
        out_shape=[out_shape, out_shape, out_shape],
        scratch_shapes=[pltpu.VMEM((L, width), F32)],
        compiler_params=_params(("arbitrary", "arbitrary")),
        name=f"hyena_filter_{L}",
    )(feat, w1p, b1.reshape(1, width), w2, b2.reshape(1, width), w3, b3.reshape(1, width),
      freq, wout, wout, t, deltas, fwd)


def _longconv_kernel(v_ref, x1_ref, x2_ref, kr_ref, kiz_ref, krn_ref, skip_ref, fwd_ref, inv_ref,
                     o_ref, *, L, pack):
    def lanes(ref):
        return jnp.concatenate([ref[s * L:(s + 1) * L, :] for s in range(pack)], axis=1)

    def tiled(x):
        return jnp.concatenate([x] * pack, axis=1)

    def conv(u, o):
        spec = _bdot(fwd_ref[...], u.astype(BF16))
        a, b = spec[:L], spec[L:]
        kr, kiz, krn = tiled(kr_ref[o]), tiled(kiz_ref[o]), tiled(krn_ref[o])
        top = (a * kr + b * kiz).astype(BF16)
        bot = (a * kiz - b * krn).astype(BF16)
        return _bdot(inv_ref[:, 0:L], top) + _bdot(inv_ref[:, L:2 * L], bot)

    v = lanes(v_ref)
    z = lanes(x1_ref) * (conv(v, 0) + v * tiled(skip_ref[0:1, :]))
    z = lanes(x2_ref) * (conv(z, 1) + z * tiled(skip_ref[1:2, :]))
    ct = o_ref.shape[1]
    for s in range(pack):
        o_ref[s * L:(s + 1) * L, :] = z[:, s * ct:(s + 1) * ct].astype(BF16)


def _longconv(grp, proj, spectra, skip, fwd, inv):
    L, d = grp.seq_len, grp.d
    ct = min(d, CONV_TILE_ELEMS // L)
    pack = max(1, CONV_MIN_LANES // ct)
    assert grp.n_seq % pack == 0
    per = d // ct
    kr, kiz, krn = spectra
    order = kr.shape[0]
    kspec = pl.BlockSpec((order, L, ct), lambda j, b: (0, 0, j))
    return pl.pallas_call(
        functools.partial(_longconv_kernel, L=L, pack=pack),
        grid=(per, grp.n_seq // pack),
        in_specs=[
            pl.BlockSpec((pack * L, ct), lambda j, b: (b, j)),
            pl.BlockSpec((pack * L, ct), lambda j, b: (b, per + j)),
            pl.BlockSpec((pack * L, ct), lambda j, b: (b, 2 * per + j)),
            kspec, kspec, kspec,
            pl.BlockSpec((order, ct), lambda j, b: (0, j)),
            pl.BlockSpec((2 * L, L), lambda j, b: (0, 0)),
            pl.BlockSpec((L, 2 * L), lambda j, b: (0, 0)),
        ],
        out_specs=pl.BlockSpec((pack * L, ct), lambda j, b: (b, j)),
        out_shape=jax.ShapeDtypeStruct((grp.rows, d), BF16),
        compiler_params=_params(("arbitrary", "arbitrary")),
        name=f"hyena_longconv_{L}",
    )(proj, proj, proj, kr, kiz, krn, skip, fwd, inv)


def _proj_res_kernel(*refs, has_bias):
    if has_bias:
        a_ref, w_ref, b_ref, x_ref, gate_ref, o_ref = refs
    else:
        a_ref, w_ref, x_ref, gate_ref, o_ref = refs
    m = _bdot(a_ref[...], w_ref[...].astype(BF16))
    if has_bias:
        m = m + b_ref[...]
    o_ref[...] = x_ref[...] + gate_ref[0] * m


def _proj_res(grp, a, w, b, x, mod3, layer):
    k, n = w.shape
    tn = RES_COL_TILE
    in_specs = [
        pl.BlockSpec((TOKEN_TILE, k), lambda i, j: (i, 0)),
        pl.BlockSpec((k, tn), lambda i, j: (0, j)),
    ]
    args = [a, w]
    if b is not None:
        in_specs.append(pl.BlockSpec((1, tn), lambda i, j: (0, j)))
        args.append(b.reshape(1, n))
    in_specs += [
        pl.BlockSpec((TOKEN_TILE, tn), lambda i, j: (i, j)),
        grp.mod_spec(layer, 2, tn, True),
    ]
    args += [x, mod3]
    return pl.pallas_call(
        functools.partial(_proj_res_kernel, has_bias=b is not None),
        grid=(grp.tiles, n // tn),
        in_specs=in_specs,
        out_specs=pl.BlockSpec((TOKEN_TILE, tn), lambda i, j: (i, j)),
        out_shape=jax.ShapeDtypeStruct((grp.rows, n), F32),
        compiler_params=_params(("arbitrary", "arbitrary")),
        name=f"proj_res_{k}_{grp.seq_len}",
    )(*args)


def _mlp_kernel(*refs, final_norm):
    if final_norm:
        x_ref, g_ref, sc_ref, sh_ref, gate_ref, w1_ref, w2_ref, fg_ref, o_ref, h_scr = refs
    else:
        x_ref, g_ref, sc_ref, sh_ref, gate_ref, w1_ref, w2_ref, o_ref, h_scr = refs
    f = pl.program_id(1)

    @pl.when(f == 0)
    def _():
        _fill_normed(x_ref, g_ref, sc_ref, sh_ref, h_scr, zero_ref=o_ref)

    w1 = w1_ref[0].astype(BF16)
    w2 = w2_ref[0].astype(BF16)
    for r in range(x_ref.shape[0] // MLP_ROWS):
        rows = slice(r * MLP_ROWS, (r + 1) * MLP_ROWS)
        u = _bdot(h_scr[rows, :], w1)
        u = jnp.square(jnp.maximum(u, 0.0)).astype(BF16)
        o_ref[rows, :] += _bdot(u, w2)

    @pl.when(f == pl.num_programs(1) - 1)
    def _():
        gate = gate_ref[0]

        def fin(r, carry):
            r0 = pl.multiple_of(r * NORM_ROWS, NORM_ROWS)
            y = x_ref[pl.ds(r0, NORM_ROWS), :] + gate * o_ref[pl.ds(r0, NORM_ROWS), :]
            if final_norm:
                ms = jnp.mean(y * y, axis=-1, keepdims=True)
                y = y * lax.rsqrt(ms + EPS) * fg_ref[...]
            o_ref[pl.ds(r0, NORM_ROWS), :] = y
            return carry

        lax.fori_loop(0, x_ref.shape[0] // NORM_ROWS, fin, 0)


def _mlp(grp, x, norm_g, mod3, layer, w1, w2, final_g=None):
    d = grp.d
    ff = w1.shape[2]
    in_specs = [
        pl.BlockSpec((TOKEN_TILE, d), lambda i, f: (i, 0), pipeline_mode=pl.Buffered(1)),
        pl.BlockSpec((1, d), lambda i, f: (0, 0)),
        grp.mod_spec(layer, 4, d, False),
        grp.mod_spec(layer, 3, d, False),
        grp.mod_spec(layer, 5, d, False),
        pl.BlockSpec((1, d, FF_TILE), lambda i, f: (layer, 0, f)),
        pl.BlockSpec((1, FF_TILE, d), lambda i, f: (layer, f, 0)),
    ]
    args = [x, norm_g.reshape(1, d), mod3, mod3, mod3, w1, w2]
    if final_g is not None:
        in_specs.append(pl.BlockSpec((1, d), lambda i, f: (0, 0)))
        args.append(final_g.reshape(1, d))
    return pl.pallas_call(
        functools.partial(_mlp_kernel, final_norm=final_g is not None),
        grid=(grp.tiles, ff // FF_TILE),
        in_specs=in_specs,
        out_specs=pl.BlockSpec((TOKEN_TILE, d), lambda i, f: (i, 0)),
        out_shape=jax.ShapeDtypeStruct((grp.rows, d), F32),
        scratch_shapes=[pltpu.VMEM((TOKEN_TILE, d), BF16)],
        compiler_params=_params(("arbitrary", "arbitrary")),
        name=f"mlp_{layer}_{grp.seq_len}",
    )(*args)


def _rotary_tables(L, dk, width):
    n_rows = L // GRID_W
    rows = np.repeat(np.arange(n_rows, dtype=np.float64), GRID_W)
    cols = np.tile(np.arange(GRID_W, dtype=np.float64), n_rows)
    n_pairs = dk // 4
    inv = ROPE_BASE ** (-np.arange(n_pairs, dtype=np.float64) / n_pairs)
    ang = np.concatenate([rows[:, None] * inv, cols[:, None] * inv], axis=-1)
    cos = np.repeat(np.cos(ang), 2, axis=-1)
    sin = np.repeat(np.sin(ang), 2, axis=-1)
    sin[:, 0::2] *= -1.0
    reps = width // dk
    return (jnp.asarray(np.tile(cos, (1, reps)), F32), jnp.asarray(np.tile(sin, (1, reps)), F32))


def _qkvg_kernel(*refs, rotary, qk_tiles, k_tile0, k_scale):
    if rotary:
        x_ref, g_ref, sc_ref, sh_ref, w_ref, cos_ref, sin_ref, o_ref, h_scr = refs
    else:
        x_ref, g_ref, sc_ref, sh_ref, w_ref, o_ref, h_scr = refs
    j = pl.program_id(1)

    @pl.when(j == 0)
    def _():
        _fill_normed(x_ref, g_ref, sc_ref, sh_ref, h_scr)

    is_qk = j < qk_tiles
    scale = jnp.where(jnp.logical_and(j >= k_tile0, is_qk), k_scale, 1.0)

    def project(rotate):
        if rotate:
            lane = lax.broadcasted_iota(jnp.int32, (h_scr.shape[0], SUB_COLS), 1)
            even = (lane & 1) == 0
        for s in range(o_ref.shape[1] // SUB_COLS):
            cols = slice(s * SUB_COLS, (s + 1) * SUB_COLS)
            acc = _bdot(h_scr[...], w_ref[:, cols].astype(BF16))
            if rotate:
                partner = jnp.where(even, pltpu.roll(acc, SUB_COLS - 1, 1), pltpu.roll(acc, 1, 1))
                acc = acc * cos_ref[...] + partner * sin_ref[...]
            o_ref[:, cols] = (acc * scale).astype(BF16)

    if rotary:
        pl.when(is_qk)(lambda: project(True))
        pl.when(jnp.logical_not(is_qk))(lambda: project(False))
    else:
        project(False)


def _qkvg(grp, x, norm_g, mod3, layer, w, dk, rotary):
    d = grp.d
    n = w.shape[1]
    in_specs = [
        pl.BlockSpec((TOKEN_TILE, d), lambda i, j: (i, 0)),
        pl.BlockSpec((1, d), lambda i, j: (0, 0)),
        grp.mod_spec(layer, 1, d, False),
        grp.mod_spec(layer, 0, d, False),
        pl.BlockSpec((d, COL_TILE), lambda i, j: (0, j)),
    ]
    args = [x, norm_g.reshape(1, d), mod3, mod3, w]
    if rotary:
        assert grp.seq_len == TOKEN_TILE
        cos, sin = _rotary_tables(grp.seq_len, dk, SUB_COLS)
        table = pl.BlockSpec((TOKEN_TILE, SUB_COLS), lambda i, j: (0, 0))
        in_specs += [table, table]
        args += [cos, sin]
    kern = functools.partial(_qkvg_kernel, rotary=rotary, qk_tiles=2 * d // COL_TILE,
                             k_tile0=d // COL_TILE, k_scale=dk ** -0.5)
    return pl.pallas_call(
        kern,
        grid=(grp.tiles, n // COL_TILE),
        in_specs=in_specs,
        out_specs=pl.BlockSpec((TOKEN_TILE, COL_TILE), lambda i, j: (i, j)),
        out_shape=jax.ShapeDtypeStruct((grp.rows, n), BF16),
        scratch_shapes=[pltpu.VMEM((TOKEN_TILE, d), BF16)],
        compiler_params=_params(("arbitrary", "arbitrary")),
        name=f"ret_qkvg_{grp.seq_len}",
    )(*args)


def _log_sigmoid(x):
    return jnp.minimum(x, 0.0) - jnp.log1p(jnp.exp(-jnp.abs(x)))


def _retention_kernel(*refs, L, C, pack, has_s0, emit_state):
    refs = list(refs)
    decay_ref, q_ref, k_ref, v_ref, g_ref, gn_ref = refs[:6]
    refs = refs[6:]
    s0_ref = refs.pop(0) if has_s0 else None
    y_ref = refs.pop(0)
    so_ref = refs.pop(0) if emit_state else None
    mask_scr, dec_scr, cdec_scr, sb_scr = refs

    head = pl.program_id(0)
    dk = q_ref.shape[1]
    dv = v_ref.shape[1]
    nc = L // C

    @pl.when(pl.program_id(1) == 0)
    def _():
        def log_decay(direction, width):
            return _log_sigmoid(jnp.full((1, width), decay_ref[direction, head], F32))

        ri = lax.broadcasted_iota(jnp.int32, (C, C), 0)
        ci = lax.broadcasted_iota(jnp.int32, (C, C), 1)
        diff = (ri - ci).astype(F32)
        lower = diff >= 0
        upper = diff <= 0
        mask_scr[...] = (
            jnp.where(lower, jnp.exp(jnp.where(lower, diff, 0.0) * log_decay(0, C)), 0.0)
            + jnp.where(upper, jnp.exp(jnp.where(upper, -diff, 0.0) * log_decay(1, C)), 0.0))
        rk = lax.broadcasted_iota(jnp.int32, (C, dk), 0).astype(F32)
        dec_scr[0] = jnp.exp((rk + 1.0) * log_decay(0, dk))
        dec_scr[1] = jnp.exp((C - 1.0 - rk) * log_decay(0, dk))
        dec_scr[2] = jnp.exp((C - rk) * log_decay(1, dk))
        dec_scr[3] = jnp.exp(rk * log_decay(1, dk))
        cdec_scr[0:1, :] = jnp.exp(C * log_decay(0, dv))
        cdec_scr[1:2, :] = jnp.exp(C * log_decay(1, dv))

    gn = gn_ref[...]

    def advance(state, dec_row, upd):
        return upd if state is None else state * cdec_scr[dec_row:dec_row + 1, :] + upd

    for s in range(pack):
        base = s * L

        def kv(n, which):
            rows = slice(base + n * C, base + (n + 1) * C)
            kd = (k_ref[rows, :].astype(F32) * dec_scr[which]).astype(BF16)
            return lax.dot_general(kd, v_ref[rows, :], (((0,), (0,)), ((), ())),
                                   preferred_element_type=F32)

        state_b = s0_ref[s, 0, 1, 0] if has_s0 else None
        has_b = [False] * nc
        for n in reversed(range(nc)):
            if state_b is not None:
                sb_scr[n] = state_b.astype(BF16)
                has_b[n] = True
            if n > 0 or emit_state:
                state_b = advance(state_b, 1, kv(n, 3))
        if emit_state:
            so_ref[s, 0, 1, 0] = state_b

        state_f = s0_ref[s, 0, 0, 0] if has_s0 else None
        for n in range(nc):
            rows = slice(base + n * C, base + (n + 1) * C)
            q = q_ref[rows, :]
            scores = lax.dot_general(q, k_ref[rows, :], (((1,), (1,)), ((), ())),
                                     preferred_element_type=F32)
            y = _bdot((scores * mask_scr[...]).astype(BF16), v_ref[rows, :])
            qf = q.astype(F32)
            if state_f is not None:
                y = y + _bdot((qf * dec_scr[0]).astype(BF16), state_f.astype(BF16))
            if has_b[n]:
                y = y + _bdot((qf * dec_scr[2]).astype(BF16), sb_scr[n])
            mu = jnp.mean(y, axis=-1, keepdims=True)
            yc = y - mu
            var = jnp.mean(yc * yc, axis=-1, keepdims=True)
            yn = (yc * lax.rsqrt(var + EPS)) * gn
            gate = g_ref[rows, :].astype(F32)
            y_ref[rows, :] = ((gate * jax.nn.sigmoid(gate)) * yn).astype(BF16)
            if n < nc - 1 or emit_state:
                state_f = advance(state_f, 0, kv(n, 1))
        if emit_state:
            so_ref[s, 0, 0, 0] = state_f


def _retention(grp, qkvg, decay, gn_g, s0, heads, dk, dv, emit_state):
    L = grp.seq_len
    C = min(L, RET_CHUNK)
    pack = max(1, min(grp.n_seq, RET_STEP_ROWS // L))
    assert grp.n_seq % pack == 0
    kq = heads
    kv0 = 2 * heads * dk // dv
    kg0 = kv0 + heads
    rows = pack * L
    in_specs = [
        pl.BlockSpec(memory_space=pltpu.SMEM),
        pl.BlockSpec((rows, dk), lambda h, b: (b, h)),
        pl.BlockSpec((rows, dk), lambda h, b: (b, kq + h)),
        pl.BlockSpec((rows, dv), lambda h, b: (b, kv0 + h)),
        pl.BlockSpec((rows, dv), lambda h, b: (b, kg0 + h)),
        pl.BlockSpec((1, dv), lambda h, b: (0, h)),
    ]
    args = [decay, qkvg, qkvg, qkvg, qkvg, gn_g.reshape(1, heads * dv)]
    state_block = (pack, 1, 2, 1, dk, dv)
    state_index = lambda h, b: (b, 0, 0, h, 0, 0)
    if s0 is not None:
        in_specs.append(pl.BlockSpec(state_block, state_index))
        args.append(s0)
    out_specs = [pl.BlockSpec((rows, dv), lambda h, b: (b, h))]
    out_shape = [jax.ShapeDtypeStruct((grp.rows, heads * dv), BF16)]
    if emit_state:
        out_specs.append(pl.BlockSpec(state_block, state_index))
        out_shape.append(jax.ShapeDtypeStruct((grp.n_seq, 1, 2, heads, dk, dv), F32))
    kern = functools.partial(_retention_kernel, L=L, C=C, pack=pack, has_s0=s0 is not None,
                             emit_state=emit_state)
    return pl.pallas_call(
        kern,
        grid=(heads, grp.n_seq // pack),
        in_specs=in_specs,
        out_specs=out_specs,
        out_shape=out_shape,
        scratch_shapes=[
            pltpu.VMEM((C, C), F32),
            pltpu.VMEM((4, C, dk), F32),
            pltpu.VMEM((8, dv), F32),
            pltpu.VMEM((L // C, dk, dv), BF16),
        ],
        compiler_params=_params(("arbitrary", "arbitrary")),
        name=f"retention_{L}",
    )(*args)


def kernel(x_prompt, x_sample, state_ret, c, c_ctx, w_ada, b_ada, norm_g, final_g, hy_w_in, hy_b_in, hy_conv_w, hy_conv_b, hy_f_w1, hy_f_b1, hy_f_w2, hy_f_b2, hy_f_w3, hy_f_b3, hy_f_freq, hy_f_wout, hy_f_skip, hy_w_out, hy_b_out, ret_w_qkvg, ret_decay, ret_gn_g, ret_w_o, mlp_w1, mlp_w2):
    n_prompt, prompt_len, d = x_prompt.shape
    n_sample, sample_len, _ = x_sample.shape
    heads, dk, dv = state_ret.shape[3:]
    depth = w_ada.shape[0]
    assert depth == 2 and hy_w_in.shape[0] == 1 and ret_w_qkvg.shape[0] == 1
    assert n_sample + 1 <= MOD_ROWS

    cond = jnp.concatenate(
        [c_ctx[None, :], c, jnp.zeros((MOD_ROWS - 1 - n_sample, d), F32)], axis=0)
    mod = _ada_mod(cond, w_ada, b_ada)
    mod3 = mod.reshape(depth * MOD_ROWS, 1, mod.shape[-1])

    w_in, w_out, w_qkvg, w_o, w1, w2 = (hy_w_in[0], hy_w_out[0], ret_w_qkvg[0], ret_w_o[0],
                                        mlp_w1, mlp_w2)

    def run(grp, x, s0, latent):
        L = grp.seq_len
        proj = _hyena_in(grp, x, norm_g[0, 0], mod3, 0, w_in, hy_b_in[0], hy_conv_w[0],
                         hy_conv_b[0])
        fwd, inv = _dft_matrices(L)
        spectra = _filter_spectra(L, d, fwd, hy_f_w1[0], hy_f_b1[0], hy_f_w2[0], hy_f_b2[0],
                                  hy_f_w3[0], hy_f_b3[0], hy_f_freq[0], hy_f_wout[0])
        z = _longconv(grp, proj, spectra, hy_f_skip[0], fwd, inv)
        x = _proj_res(grp, z, w_out, hy_b_out[0], x, mod3, 0)
        x = _mlp(grp, x, norm_g[0, 1], mod3, 0, w1, w2)
        qkvg = _qkvg(grp, x, norm_g[1, 0], mod3, 1, w_qkvg, dk, rotary=latent)
        outs = _retention(grp, qkvg, ret_decay[0], ret_gn_g[0], s0, heads, dk, dv,
                          emit_state=s0 is None)
        x = _proj_res(grp, outs[0], w_o, None, x, mod3, 1)
        y = _mlp(grp, x, norm_g[1, 1], mod3, 1, w1, w2, final_g=final_g)
        return y, (outs[1] if s0 is None else None)

    prompt = _Group(n_prompt, prompt_len, d, mod_row0=0, mod_row_step=0)
    sample = _Group(n_sample, sample_len, d, mod_row0=1, mod_row_step=1)
    y_prompt, new_state = run(prompt, x_prompt.reshape(-1, d), None, False)
    y_sample, _ = run(sample, x_sample.reshape(-1, d), state_ret, True)
    return (y_prompt.reshape(x_prompt.shape), y_sample.reshape(x_sample.shape),
            new_state.astype(x_prompt.dtype))
```

```python
import functools
import math

import numpy as np
import jax
import jax.numpy as jnp
from jax import lax
from jax.experimental import pallas as pl
from jax.experimental.pallas import tpu as pltpu

F32 = jnp.float32
BF16 = jnp.bfloat16

EPS = 1e-6
GRID_W = 64
ROPE_BASE = 10000.0
FILTER_BANDS = 16
DECAY_TARGET = 1e-2
MIN_DECAY = math.log(DECAY_TARGET) / 1.5
MAX_DECAY = math.log(DECAY_TARGET) / 0.3

V7X_VMEM_BYTES = 64 * 1024 * 1024
VMEM_LIMIT = V7X_VMEM_BYTES - 8 * 1024 * 1024

TOKEN_TILE = 1024
COL_TILE = 1024
SUB_COLS = 512
RES_COL_TILE = 512
FF_TILE_F32 = 512
FF_TILE_BF16 = 1024
NORM_ROWS = 128
MLP_ROWS = 256
CONV_TILE_ELEMS = 256 * 1024
CONV_MIN_LANES = 512
FILT_CH_TILE = 512
RET_CHUNK = 256
RET_STEP_ROWS = 1024
MOD_ROWS = 8


def _params(semantics):
    return pltpu.CompilerParams(dimension_semantics=semantics, vmem_limit_bytes=VMEM_LIMIT)


def _bdot(a, b):
    return jnp.dot(a, b, preferred_element_type=F32)


def _fdot(a, b):
    return jnp.dot(a, b, preferred_element_type=F32, precision=lax.Precision.HIGHEST)


def _modulated_norm(x, g, scale, shift):
    ms = jnp.mean(x * x, axis=-1, keepdims=True)
    return (x * lax.rsqrt(ms + EPS) * g) * (1.0 + scale) + shift


def _fill_normed(x_ref, g_ref, sc_ref, sh_ref, h_scr):
    g = g_ref[...]
    sc = sc_ref[0]
    sh = sh_ref[0]

    def body(r, carry):
        r0 = pl.multiple_of(r * NORM_ROWS, NORM_ROWS)
        x = x_ref[pl.ds(r0, NORM_ROWS), :]
        h_scr[pl.ds(r0, NORM_ROWS), :] = _modulated_norm(x, g, sc, sh).astype(BF16)
        return carry

    lax.fori_loop(0, x_ref.shape[0] // NORM_ROWS, body, 0)


class _Group:
    def __init__(self, n_seq, seq_len, d, mod_row0, mod_row_step):
        assert seq_len & (seq_len - 1) == 0
        assert (n_seq * seq_len) % TOKEN_TILE == 0
        assert TOKEN_TILE % seq_len == 0 or (mod_row_step == 0 and seq_len % TOKEN_TILE == 0)
        assert mod_row_step == 0 or seq_len == TOKEN_TILE
        self.n_seq, self.seq_len, self.d = n_seq, seq_len, d
        self.rows = n_seq * seq_len
        self.tiles = self.rows // TOKEN_TILE
        self.mod_row0, self.mod_row_step = mod_row0, mod_row_step

    def mod_spec(self, layer, chunk, width, tiled):
        per = self.d // width
        base = layer * MOD_ROWS + self.mod_row0
        step = self.mod_row_step

        def index(i, j):
            return (base + step * i, 0, chunk * per + (j if tiled else 0))

        return pl.BlockSpec((1, 1, width), index)


def _ada_kernel(cond_ref, w_ref, b_ref, o_ref):
    c = cond_ref[...]
    a = (c * jax.nn.sigmoid(c)).astype(BF16)
    o_ref[0] = _bdot(a, w_ref[0].astype(BF16)) + b_ref[0]


def _ada_mod(cond, w_ada, b_ada):
    depth, d, n = w_ada.shape
    tn = 1024
    return pl.pallas_call(
        _ada_kernel,
        grid=(depth, n // tn),
        in_specs=[
            pl.BlockSpec((MOD_ROWS, d), lambda l, j: (0, 0)),
            pl.BlockSpec((1, d, tn), lambda l, j: (l, 0, j)),
            pl.BlockSpec((1, 1, tn), lambda l, j: (l, 0, j)),
        ],
        out_specs=pl.BlockSpec((1, MOD_ROWS, tn), lambda l, j: (l, 0, j)),
        out_shape=jax.ShapeDtypeStruct((depth, MOD_ROWS, n), F32),
        compiler_params=_params(("arbitrary", "arbitrary")),
        name="ada_mod",
    )(cond, w_ada, b_ada.reshape(depth, 1, n))


def _hyena_in_kernel(x_ref, g_ref, sc_ref, sh_ref, w_ref, b_ref, cw_ref, cb_ref, o_ref, h_scr,
                     *, seq_len):
    @pl.when(pl.program_id(1) == 0)
    def _():
        _fill_normed(x_ref, g_ref, sc_ref, sh_ref, h_scr)

    rows = h_scr.shape[0]
    pos = lax.broadcasted_iota(jnp.int32, (rows, SUB_COLS), 0) & (seq_len - 1)
    first = pos == 0
    last = pos == seq_len - 1
    for s in range(o_ref.shape[1] // SUB_COLS):
        cols = slice(s * SUB_COLS, (s + 1) * SUB_COLS)
        p = _bdot(h_scr[...], w_ref[:, cols].astype(BF16)) + b_ref[:, cols]
        prev = jnp.where(first, 0.0, pltpu.roll(p, 1, 0))
        nxt = jnp.where(last, 0.0, pltpu.roll(p, rows - 1, 0))
        o_ref[:, cols] = (prev * cw_ref[0:1, cols] + p * cw_ref[1:2, cols]
                          + nxt * cw_ref[2:3, cols] + cb_ref[:, cols])


def _hyena_in(grp, x, norm_g, mod3, layer, w, b, conv_w, conv_b):
    d = grp.d
    n = w.shape[1]
    return pl.pallas_call(
        functools.partial(_hyena_in_kernel, seq_len=grp.seq_len),
        grid=(grp.tiles, n // COL_TILE),
        in_specs=[
            pl.BlockSpec((TOKEN_TILE, d), lambda i, j: (i, 0)),
            pl.BlockSpec((1, d), lambda i, j: (0, 0)),
            grp.mod_spec(layer, 1, d, False),
            grp.mod_spec(layer, 0, d, False),
            pl.BlockSpec((d, COL_TILE), lambda i, j: (0, j)),
            pl.BlockSpec((1, COL_TILE), lambda i, j: (0, j)),
            pl.BlockSpec((3, COL_TILE), lambda i, j: (0, j)),
            pl.BlockSpec((1, COL_TILE), lambda i, j: (0, j)),
        ],
        out_specs=pl.BlockSpec((TOKEN_TILE, COL_TILE), lambda i, j: (i, j)),
        out_shape=jax.ShapeDtypeStruct((grp.rows, n), F32),
        scratch_shapes=[pltpu.VMEM((TOKEN_TILE, d), BF16)],
        compiler_params=_params(("arbitrary", "arbitrary")),
        name=f"hyena_in_{grp.seq_len}",
    )(x, norm_g.reshape(1, d), mod3, mod3, w, b.reshape(1, n), conv_w, conv_b.reshape(1, n))


def _dft_matrices(L):
    f = np.arange(L, dtype=np.int64)[:, None]
    t = np.arange(L, dtype=np.int64)[None, :]
    ang = ((f * t) % (2 * L)).astype(np.float64) * (np.pi / L)
    cos, sin = np.cos(ang), np.sin(ang)
    nyq = np.where(np.arange(L) % 2 == 0, 1.0, -1.0)
    bot = sin.copy()
    bot[0, :] = nyq
    fwd = np.concatenate([cos, bot], axis=0)
    wgt = np.full((L,), 2.0)
    wgt[0] = 1.0
    inv_cos = (cos * wgt[:, None]).T / (2 * L)
    inv_sin = (-2.0 * sin).T / (2 * L)
    inv_sin[:, 0] = nyq / (2 * L)
    inv = np.concatenate([inv_cos, inv_sin], axis=1)
    return jnp.asarray(fwd, F32).astype(BF16), jnp.asarray(inv, F32).astype(BF16)


def _filter_features(L, width):
    pos = np.arange(L, dtype=np.float64)
    t = pos / L
    omega = 2.0 * math.pi * pos / L
    bands = np.linspace(1e-4, FILTER_BANDS - 1, FILTER_BANDS)
    ang = omega[:, None] * bands[None, :]
    feat = np.concatenate([t[:, None], np.cos(ang), -np.sin(ang)], axis=-1)
    feat = np.pad(feat, ((0, 0), (0, width - feat.shape[1])))
    return jnp.asarray(feat, F32), jnp.asarray(t[:, None], F32)


def _filter_kernel(feat_ref, w1_ref, b1_ref, w2_ref, b2_ref, w3_ref, b3_ref, freq_ref,
                   wof_ref, wob_ref, t_ref, delta_ref, fwd_ref,
                   kr_ref, kiz_ref, krn_ref, h_scr, *, L):
    first = jnp.logical_and(pl.program_id(0) == 0, pl.program_id(1) == 0)

    @pl.when(first)
    def _():
        h = jnp.sin(freq_ref[0:1, :] * (_fdot(feat_ref[...], w1_ref[...]) + b1_ref[...]))
        h = jnp.sin(freq_ref[1:2, :] * (_fdot(h, w2_ref[...]) + b2_ref[...]))
        h = jnp.sin(freq_ref[2:3, :] * (_fdot(h, w3_ref[...]) + b3_ref[...]))
        h_scr[...] = h

    h = h_scr[...]
    window = jnp.exp(-t_ref[...] * delta_ref[...])
    hf = _fdot(h, wof_ref[...]) * window
    hb = _fdot(h, wob_ref[...]) * window
    row = lax.broadcasted_iota(jnp.int32, hf.shape, 0)
    hb = jnp.where(row == 0, 0.0, hb)
    tap_sum = hf + hb
    tap_dif = hb - hf
    kr = _bdot(fwd_ref[0:L, :], tap_sum.astype(BF16))
    ki = _bdot(fwd_ref[L:2 * L, :], tap_dif.astype(BF16))
    sign = jnp.where((row & 1) == 0, 1.0, -1.0)
    k_nyq = jnp.sum(tap_sum * sign, axis=0, keepdims=True)
    kr_ref[0] = kr
    kiz_ref[0] = jnp.where(row == 0, 0.0, ki)
    krn_ref[0] = jnp.where(row == 0, -k_nyq, kr)


def _filter_spectra(L, d, fwd, w1, b1, w2, b2, w3, b3, freq, wout):
    width = w2.shape[0]
    order = wout.shape[1] // (2 * d)
    feat, t = _filter_features(L, width)
    w1p = jnp.pad(w1, ((0, width - w1.shape[0]), (0, 0)))
    deltas = jnp.asarray(np.abs(np.linspace(MIN_DECAY, MAX_DECAY, d))[None, :], F32)
    ct = FILT_CH_TILE
    per = d // ct
    small = lambda shape: pl.BlockSpec(shape, lambda o, j: (0, 0))
    out_spec = pl.BlockSpec((1, L, ct), lambda o, j: (o, 0, j))
    out_shape = jax.ShapeDtypeStruct((order, L, d), F32)
    return pl.pallas_call(
        functools.partial(_filter_kernel, L=L),
        grid=(order, per),
        in_specs=[
            small((L, width)), small((width, width)), small((1, width)),
            small((width, width)), small((1, width)),
            small((width, width)), small((1, width)), small((3, width)),
            pl.BlockSpec((width, ct), lambda o, j: (0, (2 * o) * per + j)),
            pl.BlockSpec((width, ct), lambda o, j: (0, (2 * o + 1) * per + j)),
            small((L, 1)),
            pl.BlockSpec((1, ct), lambda o, j: (0, j)),
            small((2 * L, L)),
        ],
        out_specs=[out_spec, out_spec, out_spec],
        out_shape=[out_shape, out_shape, out_shape],
        scratch_shapes=[pltpu.VMEM((L, width), F32)],
        compiler_params=_params(("arbitrary", "arbitrary")),
        name=f"hyena_filter_{L}",
    )(feat, w1p, b1.reshape(1, width), w2, b2.reshape(1, width), w3, b3.reshape(1, width),
      freq, wout, wout, t, deltas, fwd)


def _longconv_kernel(v_ref, x1_ref, x2_ref, kr_ref, kiz_ref, krn_ref, skip_ref, fwd_ref, inv_ref,
                     o_ref, *, L, pack):
    def lanes(ref):
        return jnp.concatenate([ref[s * L:(s + 1) * L, :] for s in range(pack)], axis=1)

    def tiled(x):
        return jnp.concatenate([x] * pack, axis=1)

    def conv(u, o):
        spec = _bdot(fwd_ref[...], u.astype(BF16))
        a, b = spec[:L], spec[L:]
        kr, kiz, krn = tiled(kr_ref[o]), tiled(kiz_ref[o]), tiled(krn_ref[o])
        top = (a * kr + b * kiz).astype(BF16)
        bot = (a * kiz - b * krn).astype(BF16)
        return _bdot(inv_ref[:, 0:L], top) + _bdot(inv_ref[:, L:2 * L], bot)

    v = lanes(v_ref)
    z = lanes(x1_ref) * (conv(v, 0) + v * tiled(skip_ref[0:1, :]))
    z = lanes(x2_ref) * (conv(z, 1) + z * tiled(skip_ref[1:2, :]))
    ct = o_ref.shape[1]
    for s in range(pack):
        o_ref[s * L:(s + 1) * L, :] = z[:, s * ct:(s + 1) * ct].astype(BF16)


def _longconv(grp, proj, spectra, skip, fwd, inv):
    L, d = grp.seq_len, grp.d
    ct = min(d, CONV_TILE_ELEMS // L)
    pack = max(1, CONV_MIN_LANES // ct)
    assert grp.n_seq % pack == 0
    per = d // ct
    kr, kiz, krn = spectra
    order = kr.shape[0]
    kspec = pl.BlockSpec((order, L, ct), lambda j, b: (0, 0, j))
    return pl.pallas_call(
        functools.partial(_longconv_kernel, L=L, pack=pack),
        grid=(per, grp.n_seq // pack),
        in_specs=[
            pl.BlockSpec((pack * L, ct), lambda j, b: (b, j)),
            pl.BlockSpec((pack * L, ct), lambda j, b: (b, per + j)),
            pl.BlockSpec((pack * L, ct), lambda j, b: (b, 2 * per + j)),
            kspec, kspec, kspec,
            pl.BlockSpec((order, ct), lambda j, b: (0, j)),
            pl.BlockSpec((2 * L, L), lambda j, b: (0, 0)),
            pl.BlockSpec((L, 2 * L), lambda j, b: (0, 0)),
        ],
        out_specs=pl.BlockSpec((pack * L, ct), lambda j, b: (b, j)),
        out_shape=jax.ShapeDtypeStruct((grp.rows, d), BF16),
        compiler_params=_params(("arbitrary", "arbitrary")),
        name=f"hyena_longconv_{L}",
    )(proj, proj, proj, kr, kiz, krn, skip, fwd, inv)


def _proj_res_kernel(*refs, has_bias):
    if has_bias:
        a_ref, w_ref, b_ref, x_ref, gate_ref, o_ref = refs
    else:
        a_ref, w_ref, x_ref, gate_ref, o_ref = refs
    m = _bdot(a_ref[...], w_ref[...].astype(BF16))
    if has_bias:
        m = m + b_ref[...]
    o_ref[...] = x_ref[...] + gate_ref[0] * m


def _proj_res(grp, a, w, b, x, mod3, layer):
    k, n = w.shape
    tn = RES_COL_TILE
    in_specs = [
        pl.BlockSpec((TOKEN_TILE, k), lambda i, j: (i, 0)),
        pl.BlockSpec((k, tn), lambda i, j: (0, j)),
    ]
    args = [a, w]
    if b is not None:
        in_specs.append(pl.BlockSpec((1, tn), lambda i, j: (0, j)))
        args.append(b.reshape(1, n))
    in_specs += [
        pl.BlockSpec((TOKEN_TILE, tn), lambda i, j: (i, j)),
        grp.mod_spec(layer, 2, tn, True),
    ]
    args += [x, mod3]
    return pl.pallas_call(
        functools.partial(_proj_res_kernel, has_bias=b is not None),
        grid=(grp.tiles, n // tn),
        in_specs=in_specs,
        out_specs=pl.BlockSpec((TOKEN_TILE, tn), lambda i, j: (i, j)),
        out_shape=jax.ShapeDtypeStruct((grp.rows, n), F32),
        compiler_params=_params(("arbitrary", "arbitrary")),
        name=f"proj_res_{k}_{grp.seq_len}",
    )(*args)


def _mlp_kernel(*refs, final_norm, cast, emit_cache):
    refs = list(refs)
    x_ref, g_ref, sc_ref, sh_ref, gate_ref, w1_ref, w2_ref = refs[:7]
    refs = refs[7:]
    fg_ref = refs.pop(0) if final_norm else None
    o_ref = refs.pop(0)
    w1c_ref, w2c_ref = (refs.pop(0), refs.pop(0)) if emit_cache else (None, None)
    (h_scr,) = refs
    f = pl.program_id(1)
    last_f = pl.num_programs(1) - 1

    def step(first, last):
        w1 = w1_ref[0].astype(BF16) if cast else w1_ref[...]
        w2 = w2_ref[0].astype(BF16) if cast else w2_ref[...]
        for r in range(x_ref.shape[0] // MLP_ROWS):
            rows = slice(r * MLP_ROWS, (r + 1) * MLP_ROWS)
            if first:
                h = _modulated_norm(x_ref[rows, :], g_ref[...], sc_ref[0], sh_ref[0]).astype(BF16)
                h_scr[rows, :] = h
            else:
                h = h_scr[rows, :]
            u = jnp.square(jnp.maximum(_bdot(h, w1), 0.0)).astype(BF16)
            acc = _bdot(u, w2)
            if not first:
                acc = o_ref[rows, :] + acc
            if last:
                acc = x_ref[rows, :] + gate_ref[0] * acc
                if final_norm:
                    ms = jnp.mean(acc * acc, axis=-1, keepdims=True)
                    acc = acc * lax.rsqrt(ms + EPS) * fg_ref[...]
            o_ref[rows, :] = acc
        if emit_cache:
            @pl.when(pl.program_id(0) == 0)
            def _():
                w1c_ref[...] = w1
                w2c_ref[...] = w2

    pl.when(f == 0)(lambda: step(True, False))
    pl.when(jnp.logical_and(f > 0, f < last_f))(lambda: step(False, False))
    pl.when(f == last_f)(lambda: step(False, True))


def _mlp(grp, x, norm_g, mod3, layer, w1, w2, final_g=None, emit_cache=False):
    d = grp.d
    cast = w1.dtype != BF16
    assert cast or not emit_cache
    ff = w1.shape[-1]
    tf = FF_TILE_F32 if cast else FF_TILE_BF16
    nf = ff // tf
    assert nf >= 2
    if cast:
        w_specs = [pl.BlockSpec((1, d, tf), lambda i, f: (layer, 0, f)),
                   pl.BlockSpec((1, tf, d), lambda i, f: (layer, f, 0))]
    else:
        w_specs = [pl.BlockSpec((d, tf), lambda i, f: (0, f)),
                   pl.BlockSpec((tf, d), lambda i, f: (f, 0))]
    in_specs = [
        pl.BlockSpec((TOKEN_TILE, d), lambda i, f: (i, 0), pipeline_mode=pl.Buffered(1)),
        pl.BlockSpec((1, d), lambda i, f: (0, 0)),
        grp.mod_spec(layer, 4, d, False),
        grp.mod_spec(layer, 3, d, False),
        grp.mod_spec(layer, 5, d, False),
    ] + w_specs
    args = [x, norm_g.reshape(1, d), mod3, mod3, mod3, w1, w2]
    if final_g is not None:
        in_specs.append(pl.BlockSpec((1, d), lambda i, f: (0, 0)))
        args.append(final_g.reshape(1, d))
    out_mode = pl.Buffered(1) if emit_cache else None
    out_specs = [pl.BlockSpec((TOKEN_TILE, d), lambda i, f: (i, 0), pipeline_mode=out_mode)]
    out_shape = [jax.ShapeDtypeStruct((grp.rows, d), F32)]
    if emit_cache:
        frozen = lambda i, f: jnp.where(i == 0, f, nf - 1)
        single = pl.Buffered(1)
        out_specs += [
            pl.BlockSpec((d, tf), lambda i, f: (0, frozen(i, f)), pipeline_mode=single),
            pl.BlockSpec((tf, d), lambda i, f: (frozen(i, f), 0), pipeline_mode=single)]
        out_shape += [jax.ShapeDtypeStruct((d, ff), BF16), jax.ShapeDtypeStruct((ff, d), BF16)]
    outs = pl.pallas_call(
        functools.partial(_mlp_kernel, final_norm=final_g is not None, cast=cast,
                          emit_cache=emit_cache),
        grid=(grp.tiles, nf),
        in_specs=in_specs,
        out_specs=out_specs,
        out_shape=out_shape,
        scratch_shapes=[pltpu.VMEM((TOKEN_TILE, d), BF16)],
        compiler_params=_params(("arbitrary", "arbitrary")),
        name=f"mlp_{layer}_{grp.seq_len}",
    )(*args)
    return outs[0], tuple(outs[1:])


def _rotary_tables(L, dk, width):
    n_rows = L // GRID_W
    rows = np.repeat(np.arange(n_rows, dtype=np.float64), GRID_W)
    cols = np.tile(np.arange(GRID_W, dtype=np.float64), n_rows)
    n_pairs = dk // 4
    inv = ROPE_BASE ** (-np.arange(n_pairs, dtype=np.float64) / n_pairs)
    ang = np.concatenate([rows[:, None] * inv, cols[:, None] * inv], axis=-1)
    cos = np.repeat(np.cos(ang), 2, axis=-1)
    sin = np.repeat(np.sin(ang), 2, axis=-1)
    sin[:, 0::2] *= -1.0
    reps = width // dk
    return (jnp.asarray(np.tile(cos, (1, reps)), F32), jnp.asarray(np.tile(sin, (1, reps)), F32))


def _qkvg_kernel(*refs, rotary, qk_tiles, k_tile0, k_scale):
    if rotary:
        x_ref, g_ref, sc_ref, sh_ref, w_ref, cos_ref, sin_ref, o_ref, h_scr = refs
    else:
        x_ref, g_ref, sc_ref, sh_ref, w_ref, o_ref, h_scr = refs
    j = pl.program_id(1)

    @pl.when(j == 0)
    def _():
        _fill_normed(x_ref, g_ref, sc_ref, sh_ref, h_scr)

    is_qk = j < qk_tiles
    scale = jnp.where(jnp.logical_and(j >= k_tile0, is_qk), k_scale, 1.0)

    def project(rotate):
        if rotate:
            lane = lax.broadcasted_iota(jnp.int32, (h_scr.shape[0], SUB_COLS), 1)
            even = (lane & 1) == 0
        for s in range(o_ref.shape[1] // SUB_COLS):
            cols = slice(s * SUB_COLS, (s + 1) * SUB_COLS)
            acc = _bdot(h_scr[...], w_ref[:, cols].astype(BF16))
            if rotate:
                partner = jnp.where(even, pltpu.roll(acc, SUB_COLS - 1, 1), pltpu.roll(acc, 1, 1))
                acc = acc * cos_ref[...] + partner * sin_ref[...]
            o_ref[:, cols] = (acc * scale).astype(BF16)

    if rotary:
        pl.when(is_qk)(lambda: project(True))
        pl.when(jnp.logical_not(is_qk))(lambda: project(False))
    else:
        project(False)


def _qkvg(grp, x, norm_g, mod3, layer, w, dk, rotary):
    d = grp.d
    n = w.shape[1]
    in_specs = [
        pl.BlockSpec((TOKEN_TILE, d), lambda i, j: (i, 0)),
        pl.BlockSpec((1, d), lambda i, j: (0, 0)),
        grp.mod_spec(layer, 1, d, False),
        grp.mod_spec(layer, 0, d, False),
        pl.BlockSpec((d, COL_TILE), lambda i, j: (0, j)),
    ]
    args = [x, norm_g.reshape(1, d), mod3, mod3, w]
    if rotary:
        assert grp.seq_len == TOKEN_TILE
        cos, sin = _rotary_tables(grp.seq_len, dk, SUB_COLS)
        table = pl.BlockSpec((TOKEN_TILE, SUB_COLS), lambda i, j: (0, 0))
        in_specs += [table, table]
        args += [cos, sin]
    kern = functools.partial(_qkvg_kernel, rotary=rotary, qk_tiles=2 * d // COL_TILE,
                             k_tile0=d // COL_TILE, k_scale=dk ** -0.5)
    return pl.pallas_call(
        kern,
        grid=(grp.tiles, n // COL_TILE),
        in_specs=in_specs,
        out_specs=pl.BlockSpec((TOKEN_TILE, COL_TILE), lambda i, j: (i, j)),
        out_shape=jax.ShapeDtypeStruct((grp.rows, n), BF16),
        scratch_shapes=[pltpu.VMEM((TOKEN_TILE, d), BF16)],
        compiler_params=_params(("arbitrary", "arbitrary")),
        name=f"ret_qkvg_{grp.seq_len}",
    )(*args)


def _log_sigmoid(x):
    return jnp.minimum(x, 0.0) - jnp.log1p(jnp.exp(-jnp.abs(x)))


def _retention_kernel(*refs, L, C, pack, has_s0, emit_state):
    refs = list(refs)
    decay_ref, q_ref, k_ref, v_ref, g_ref, gn_ref = refs[:6]
    refs = refs[6:]
    s0_ref = refs.pop(0) if has_s0 else None
    y_ref = refs.pop(0)
    so_ref = refs.pop(0) if emit_state else None
    mask_scr, dec_scr, cdec_scr, sb_scr = refs

    head = pl.program_id(0)
    dk = q_ref.shape[1]
    dv = v_ref.shape[1]
    nc = L // C

    @pl.when(pl.program_id(1) == 0)
    def _():
        def log_decay(direction, width):
            return _log_sigmoid(jnp.full((1, width), decay_ref[direction, head], F32))

        ri = lax.broadcasted_iota(jnp.int32, (C, C), 0)
        ci = lax.broadcasted_iota(jnp.int32, (C, C), 1)
        diff = (ri - ci).astype(F32)
        lower = diff >= 0
        upper = diff <= 0
        mask_scr[...] = (
            jnp.where(lower, jnp.exp(jnp.where(lower, diff, 0.0) * log_decay(0, C)), 0.0)
            + jnp.where(upper, jnp.exp(jnp.where(upper, -diff, 0.0) * log_decay(1, C)), 0.0))
        rk = lax.broadcasted_iota(jnp.int32, (C, dk), 0).astype(F32)
        dec_scr[0] = jnp.exp((rk + 1.0) * log_decay(0, dk))
        dec_scr[1] = jnp.exp((C - 1.0 - rk) * log_decay(0, dk))
        dec_scr[2] = jnp.exp((C - rk) * log_decay(1, dk))
        dec_scr[3] = jnp.exp(rk * log_decay(1, dk))
        cdec_scr[0:1, :] = jnp.exp(C * log_decay(0, dv))
        cdec_scr[1:2, :] = jnp.exp(C * log_decay(1, dv))

    gn = gn_ref[...]

    def advance(state, dec_row, upd):
        return upd if state is None else state * cdec_scr[dec_row:dec_row + 1, :] + upd

    for s in range(pack):
        base = s * L

        def kv(n, which):
            rows = slice(base + n * C, base + (n + 1) * C)
            kd = (k_ref[rows, :].astype(F32) * dec_scr[which]).astype(BF16)
            return lax.dot_general(kd, v_ref[rows, :], (((0,), (0,)), ((), ())),
                                   preferred_element_type=F32)

        state_b = s0_ref[s, 0, 1, 0] if has_s0 else None
        has_b = [False] * nc
        for n in reversed(range(nc)):
            if state_b is not None:
                sb_scr[n] = state_b.astype(BF16)
                has_b[n] = True
            if n > 0 or emit_state:
                state_b = advance(state_b, 1, kv(n, 3))
        if emit_state:
            so_ref[s, 0, 1, 0] = state_b

        state_f = s0_ref[s, 0, 0, 0] if has_s0 else None
        for n in range(nc):
            rows = slice(base + n * C, base + (n + 1) * C)
            q = q_ref[rows, :]
            scores = lax.dot_general(q, k_ref[rows, :], (((1,), (1,)), ((), ())),
                                     preferred_element_type=F32)
            y = _bdot((scores * mask_scr[...]).astype(BF16), v_ref[rows, :])
            qf = q.astype(F32)
            if state_f is not None:
                y = y + _bdot((qf * dec_scr[0]).astype(BF16), state_f.astype(BF16))
            if has_b[n]:
                y = y + _bdot((qf * dec_scr[2]).astype(BF16), sb_scr[n])
            mu = jnp.mean(y, axis=-1, keepdims=True)
            yc = y - mu
            var = jnp.mean(yc * yc, axis=-1, keepdims=True)
            yn = (yc * lax.rsqrt(var + EPS)) * gn
            gate = g_ref[rows, :].astype(F32)
            y_ref[rows, :] = ((gate * jax.nn.sigmoid(gate)) * yn).astype(BF16)
            if n < nc - 1 or emit_state:
                state_f = advance(state_f, 0, kv(n, 1))
        if emit_state:
            so_ref[s, 0, 0, 0] = state_f


def _retention(grp, qkvg, decay, gn_g, s0, heads, dk, dv, emit_state):
    L = grp.seq_len
    C = min(L, RET_CHUNK)
    pack = max(1, min(grp.n_seq, RET_STEP_ROWS // L))
    assert grp.n_seq % pack == 0
    kq = heads
    kv0 = 2 * heads * dk // dv
    kg0 = kv0 + heads
    rows = pack * L
    in_specs = [
        pl.BlockSpec(memory_space=pltpu.SMEM),
        pl.BlockSpec((rows, dk), lambda h, b: (b, h)),
        pl.BlockSpec((rows, dk), lambda h, b: (b, kq + h)),
        pl.BlockSpec((rows, dv), lambda h, b: (b, kv0 + h)),
        pl.BlockSpec((rows, dv), lambda h, b: (b, kg0 + h)),
        pl.BlockSpec((1, dv), lambda h, b: (0, h)),
    ]
    args = [decay, qkvg, qkvg, qkvg, qkvg, gn_g.reshape(1, heads * dv)]
    state_block = (pack, 1, 2, 1, dk, dv)
    state_index = lambda h, b: (b, 0, 0, h, 0, 0)
    if s0 is not None:
        in_specs.append(pl.BlockSpec(state_block, state_index))
        args.append(s0)
    out_specs = [pl.BlockSpec((rows, dv), lambda h, b: (b, h))]
    out_shape = [jax.ShapeDtypeStruct((grp.rows, heads * dv), BF16)]
    if emit_state:
        out_specs.append(pl.BlockSpec(state_block, state_index))
        out_shape.append(jax.ShapeDtypeStruct((grp.n_seq, 1, 2, heads, dk, dv), F32))
    kern = functools.partial(_retention_kernel, L=L, C=C, pack=pack, has_s0=s0 is not None,
                             emit_state=emit_state)
    return pl.pallas_call(
        kern,
        grid=(heads, grp.n_seq // pack),
        in_specs=in_specs,
        out_specs=out_specs,
        out_shape=out_shape,
        scratch_shapes=[
            pltpu.VMEM((C, C), F32),
            pltpu.VMEM((4, C, dk), F32),
            pltpu.VMEM((8, dv), F32),
            pltpu.VMEM((L // C, dk, dv), BF16),
        ],
        compiler_params=_params(("arbitrary", "arbitrary")),
        name=f"retention_{L}",
    )(*args)


def kernel(x_prompt, x_sample, state_ret, c, c_ctx, w_ada, b_ada, norm_g, final_g, hy_w_in, hy_b_in, hy_conv_w, hy_conv_b, hy_f_w1, hy_f_b1, hy_f_w2, hy_f_b2, hy_f_w3, hy_f_b3, hy_f_freq, hy_f_wout, hy_f_skip, hy_w_out, hy_b_out, ret_w_qkvg, ret_decay, ret_gn_g, ret_w_o, mlp_w1, mlp_w2):
    n_prompt, prompt_len, d = x_prompt.shape
    n_sample, sample_len, _ = x_sample.shape
    heads, dk, dv = state_ret.shape[3:]
    depth = w_ada.shape[0]
    assert depth == 2 and hy_w_in.shape[0] == 1 and ret_w_qkvg.shape[0] == 1
    assert n_sample + 1 <= MOD_ROWS

    cond = jnp.concatenate(
        [c_ctx[None, :], c, jnp.zeros((MOD_ROWS - 1 - n_sample, d), F32)], axis=0)
    mod = _ada_mod(cond, w_ada, b_ada)
    mod3 = mod.reshape(depth * MOD_ROWS, 1, mod.shape[-1])

    w_in, w_out, w_qkvg, w_o, w1, w2 = (hy_w_in[0], hy_w_out[0], ret_w_qkvg[0], ret_w_o[0],
                                        mlp_w1, mlp_w2)

    def run(grp, x, s0, latent, mlp_w):
        def mlp(x, layer, **kw):
            if mlp_w is None:
                return _mlp(grp, x, norm_g[layer, 1], mod3, layer, w1, w2, emit_cache=True, **kw)
            return _mlp(grp, x, norm_g[layer, 1], mod3, layer, *mlp_w[layer], **kw)

        L = grp.seq_len
        proj = _hyena_in(grp, x, norm_g[0, 0], mod3, 0, w_in, hy_b_in[0], hy_conv_w[0],
                         hy_conv_b[0])
        fwd, inv = _dft_matrices(L)
        spectra = _filter_spectra(L, d, fwd, hy_f_w1[0], hy_f_b1[0], hy_f_w2[0], hy_f_b2[0],
                                  hy_f_w3[0], hy_f_b3[0], hy_f_freq[0], hy_f_wout[0])
        z = _longconv(grp, proj, spectra, hy_f_skip[0], fwd, inv)
        x = _proj_res(grp, z, w_out, hy_b_out[0], x, mod3, 0)
        x, cache0 = mlp(x, 0)
        qkvg = _qkvg(grp, x, norm_g[1, 0], mod3, 1, w_qkvg, dk, rotary=latent)
        outs = _retention(grp, qkvg, ret_decay[0], ret_gn_g[0], s0, heads, dk, dv,
                          emit_state=s0 is None)
        x = _proj_res(grp, outs[0], w_o, None, x, mod3, 1)
        y, cache1 = mlp(x, 1, final_g=final_g)
        return y, (outs[1] if s0 is None else None), (cache0, cache1)

    prompt = _Group(n_prompt, prompt_len, d, mod_row0=0, mod_row_step=0)
    sample = _Group(n_sample, sample_len, d, mod_row0=1, mod_row_step=1)
    y_prompt, new_state, mlp_bf16 = run(prompt, x_prompt.reshape(-1, d), None, False, None)
    y_sample, _, _ = run(sample, x_sample.reshape(-1, d), state_ret, True, mlp_bf16)
    return (y_prompt.reshape(x_prompt.shape), y_sample.reshape(x_sample.shape),
            new_state.astype(x_prompt.dtype))
```

```python
import functools
import math

import numpy as np
import jax
import jax.numpy as jnp
from jax import lax
from jax.experimental import pallas as pl
from jax.experimental.pallas import tpu as pltpu

F32 = jnp.float32
BF16 = jnp.bfloat16

EPS = 1e-6
GRID_W = 64
ROPE_BASE = 10000.0
FILTER_BANDS = 16
DECAY_TARGET = 1e-2
MIN_DECAY = math.log(DECAY_TARGET) / 1.5
MAX_DECAY = math.log(DECAY_TARGET) / 0.3

V7X_VMEM_BYTES = 64 * 1024 * 1024
VMEM_LIMIT = V7X_VMEM_BYTES - 8 * 1024 * 1024

TOKEN_TILE = 1024
COL_TILE = 1024
SUB_COLS = 512
RES_COL_TILE = 512
RES_WEIGHT_BYTES_FOR_WIDE_TILE = 8 * 1024 * 1024
FF_TILE_F32 = 512
FF_TILE_BF16 = 1024
MLP_ROWS = 256
CONV_TILE_ELEMS = 256 * 1024
CONV_MIN_LANES = 512
FILT_CH_TILE = 512
RET_CHUNK = 256
RET_STEP_ROWS = 1024
MOD_ROWS = 8


def _params(semantics):
    return pltpu.CompilerParams(dimension_semantics=semantics, vmem_limit_bytes=VMEM_LIMIT)


def _bdot(a, b):
    return jnp.dot(a, b, preferred_element_type=F32)


def _fdot(a, b):
    return jnp.dot(a, b, preferred_element_type=F32, precision=lax.Precision.HIGHEST)


def _modulated_norm(x, g, scale, shift):
    ms = jnp.mean(x * x, axis=-1, keepdims=True)
    return (x * lax.rsqrt(ms + EPS) * g) * (1.0 + scale) + shift


def _normed_dot(fill, x_ref, g_ref, sc_ref, sh_ref, h_scr, w):
    if not fill:
        return _bdot(h_scr[...], w)
    parts = []
    for r in range(x_ref.shape[0] // MLP_ROWS):
        rows = slice(r * MLP_ROWS, (r + 1) * MLP_ROWS)
        h = _modulated_norm(x_ref[rows, :], g_ref[...], sc_ref[0], sh_ref[0]).astype(BF16)
        h_scr[rows, :] = h
        parts.append(_bdot(h, w))
    return jnp.concatenate(parts, axis=0)


class _Group:
    def __init__(self, n_seq, seq_len, d, mod_row0, mod_row_step):
        assert seq_len & (seq_len - 1) == 0
        assert (n_seq * seq_len) % TOKEN_TILE == 0
        assert TOKEN_TILE % seq_len == 0 or (mod_row_step == 0 and seq_len % TOKEN_TILE == 0)
        assert mod_row_step == 0 or seq_len == TOKEN_TILE
        self.n_seq, self.seq_len, self.d = n_seq, seq_len, d
        self.rows = n_seq * seq_len
        self.tiles = self.rows // TOKEN_TILE
        self.mod_row0, self.mod_row_step = mod_row0, mod_row_step

    def mod_spec(self, layer, chunk, width, tiled):
        per = self.d // width
        base = layer * MOD_ROWS + self.mod_row0
        step = self.mod_row_step

        def index(i, j):
            return (base + step * i, 0, chunk * per + (j if tiled else 0))

        return pl.BlockSpec((1, 1, width), index)


def _ada_kernel(cond_ref, w_ref, b_ref, o_ref):
    c = cond_ref[...]
    a = (c * jax.nn.sigmoid(c)).astype(BF16)
    o_ref[0] = _bdot(a, w_ref[0].astype(BF16)) + b_ref[0]


def _ada_mod(cond, w_ada, b_ada):
    depth, d, n = w_ada.shape
    tn = 1024
    return pl.pallas_call(
        _ada_kernel,
        grid=(depth, n // tn),
        in_specs=[
            pl.BlockSpec((MOD_ROWS, d), lambda l, j: (0, 0)),
            pl.BlockSpec((1, d, tn), lambda l, j: (l, 0, j)),
            pl.BlockSpec((1, 1, tn), lambda l, j: (l, 0, j)),
        ],
        out_specs=pl.BlockSpec((1, MOD_ROWS, tn), lambda l, j: (l, 0, j)),
        out_shape=jax.ShapeDtypeStruct((depth, MOD_ROWS, n), F32),
        compiler_params=_params(("arbitrary", "arbitrary")),
        name="ada_mod",
    )(cond, w_ada, b_ada.reshape(depth, 1, n))


def _hyena_in_kernel(x_ref, g_ref, sc_ref, sh_ref, w_ref, b_ref, cw_ref, cb_ref, o_ref, h_scr,
                     *, seq_len):
    def project(fill):
        rows = h_scr.shape[0]
        pos = lax.broadcasted_iota(jnp.int32, (rows, SUB_COLS), 0) & (seq_len - 1)
        first = pos == 0
        last = pos == seq_len - 1
        for s in range(o_ref.shape[1] // SUB_COLS):
            cols = slice(s * SUB_COLS, (s + 1) * SUB_COLS)
            p = _normed_dot(fill and s == 0, x_ref, g_ref, sc_ref, sh_ref, h_scr,
                            w_ref[:, cols].astype(BF16)) + b_ref[:, cols]
            prev = jnp.where(first, 0.0, pltpu.roll(p, 1, 0))
            nxt = jnp.where(last, 0.0, pltpu.roll(p, rows - 1, 0))
            o_ref[:, cols] = (prev * cw_ref[0:1, cols] + p * cw_ref[1:2, cols]
                              + nxt * cw_ref[2:3, cols] + cb_ref[:, cols])

    j = pl.program_id(1)
    pl.when(j == 0)(lambda: project(True))
    pl.when(j > 0)(lambda: project(False))


def _hyena_in(grp, x, norm_g, mod3, layer, w, b, conv_w, conv_b):
    d = grp.d
    n = w.shape[1]
    return pl.pallas_call(
        functools.partial(_hyena_in_kernel, seq_len=grp.seq_len),
        grid=(grp.tiles, n // COL_TILE),
        in_specs=[
            pl.BlockSpec((TOKEN_TILE, d), lambda i, j: (i, 0)),
            pl.BlockSpec((1, d), lambda i, j: (0, 0)),
            grp.mod_spec(layer, 1, d, False),
            grp.mod_spec(layer, 0, d, False),
            pl.BlockSpec((d, COL_TILE), lambda i, j: (0, j)),
            pl.BlockSpec((1, COL_TILE), lambda i, j: (0, j)),
            pl.BlockSpec((3, COL_TILE), lambda i, j: (0, j)),
            pl.BlockSpec((1, COL_TILE), lambda i, j: (0, j)),
        ],
        out_specs=pl.BlockSpec((TOKEN_TILE, COL_TILE), lambda i, j: (i, j)),
        out_shape=jax.ShapeDtypeStruct((grp.rows, n), F32),
        scratch_shapes=[pltpu.VMEM((TOKEN_TILE, d), BF16)],
        compiler_params=_params(("arbitrary", "arbitrary")),
        name=f"hyena_in_{grp.seq_len}",
    )(x, norm_g.reshape(1, d), mod3, mod3, w, b.reshape(1, n), conv_w, conv_b.reshape(1, n))


def _dft_matrices(L):
    f = np.arange(L, dtype=np.int64)[:, None]
    t = np.arange(L, dtype=np.int64)[None, :]
    ang = ((f * t) % (2 * L)).astype(np.float64) * (np.pi / L)
    cos, sin = np.cos(ang), np.sin(ang)
    nyq = np.where(np.arange(L) % 2 == 0, 1.0, -1.0)
    bot = sin.copy()
    bot[0, :] = nyq
    fwd = np.concatenate([cos, bot], axis=0)
    wgt = np.full((L,), 2.0)
    wgt[0] = 1.0
    inv_cos = (cos * wgt[:, None]).T / (2 * L)
    inv_sin = (-2.0 * sin).T / (2 * L)
    inv_sin[:, 0] = nyq / (2 * L)
    inv = np.concatenate([inv_cos, inv_sin], axis=1)
    return jnp.asarray(fwd, F32).astype(BF16), jnp.asarray(inv, F32).astype(BF16)


def _filter_features(L, width):
    pos = np.arange(L, dtype=np.float64)
    t = pos / L
    omega = 2.0 * math.pi * pos / L
    bands = np.linspace(1e-4, FILTER_BANDS - 1, FILTER_BANDS)
    ang = omega[:, None] * bands[None, :]
    feat = np.concatenate([t[:, None], np.cos(ang), -np.sin(ang)], axis=-1)
    feat = np.pad(feat, ((0, 0), (0, width - feat.shape[1])))
    return jnp.asarray(feat, F32), jnp.asarray(t[:, None], F32)


def _filter_kernel(feat_ref, w1_ref, b1_ref, w2_ref, b2_ref, w3_ref, b3_ref, freq_ref,
                   wof_ref, wob_ref, t_ref, delta_ref, fwd_ref,
                   kr_ref, kiz_ref, krn_ref, h_scr, *, L):
    first = jnp.logical_and(pl.program_id(0) == 0, pl.program_id(1) == 0)

    @pl.when(first)
    def _():
        h = jnp.sin(freq_ref[0:1, :] * (_fdot(feat_ref[...], w1_ref[...]) + b1_ref[...]))
        h = jnp.sin(freq_ref[1:2, :] * (_fdot(h, w2_ref[...]) + b2_ref[...]))
        h = jnp.sin(freq_ref[2:3, :] * (_fdot(h, w3_ref[...]) + b3_ref[...]))
        h_scr[...] = h

    h = h_scr[...]
    window = jnp.exp(-t_ref[...] * delta_ref[...])
    hf = _fdot(h, wof_ref[...]) * window
    hb = _fdot(h, wob_ref[...]) * window
    row = lax.broadcasted_iota(jnp.int32, hf.shape, 0)
    hb = jnp.where(row == 0, 0.0, hb)
    tap_sum = hf + hb
    tap_dif = hb - hf
    kr = _bdot(fwd_ref[0:L, :], tap_sum.astype(BF16))
    ki = _bdot(fwd_ref[L:2 * L, :], tap_dif.astype(BF16))
    sign = jnp.where((row & 1) == 0, 1.0, -1.0)
    k_nyq = jnp.sum(tap_sum * sign, axis=0, keepdims=True)
    kr_ref[0] = kr
    kiz_ref[0] = jnp.where(row == 0, 0.0, ki)
    krn_ref[0] = jnp.where(row == 0, -k_nyq, kr)


def _filter_spectra(L, d, fwd, w1, b1, w2, b2, w3, b3, freq, wout):
    width = w2.shape[0]
    order = wout.shape[1] // (2 * d)
    feat, t = _filter_features(L, width)
    w1p = jnp.pad(w1, ((0, width - w1.shape[0]), (0, 0)))
    deltas = jnp.asarray(np.abs(np.linspace(MIN_DECAY, MAX_DECAY, d))[None, :], F32)
    ct = FILT_CH_TILE
    per = d // ct
    small = lambda shape: pl.BlockSpec(shape, lambda o, j: (0, 0))
    out_spec = pl.BlockSpec((1, L, ct), lambda o, j: (o, 0, j))
    out_shape = jax.ShapeDtypeStruct((order, L, d), F32)
    return pl.pallas_call(
        functools.partial(_filter_kernel, L=L),
        grid=(order, per),
        in_specs=[
            small((L, width)), small((width, width)), small((1, width)),
            small((width, width)), small((1, width)),
            small((width, width)), small((1, width)), small((3, width)),
            pl.BlockSpec((width, ct), lambda o, j: (0, (2 * o) * per + j)),
            pl.BlockSpec((width, ct), lambda o, j: (0, (2 * o + 1) * per + j)),
            small((L, 1)),
            pl.BlockSpec((1, ct), lambda o, j: (0, j)),
            small((2 * L, L)),
        ],
        out_specs=[out_spec, out_spec, out_spec],
        out_shape=[out_shape, out_shape, out_shape],
        scratch_shapes=[pltpu.VMEM((L, width), F32)],
        compiler_params=_params(("arbitrary", "arbitrary")),
        name=f"hyena_filter_{L}",
    )(feat, w1p, b1.reshape(1, width), w2, b2.reshape(1, width), w3, b3.reshape(1, width),
      freq, wout, wout, t, deltas, fwd)


def _longconv_kernel(v_ref, x1_ref, x2_ref, kr_ref, kiz_ref, krn_ref, skip_ref, fwd_ref, inv_ref,
                     o_ref, *, L, pack):
    def lanes(ref):
        return jnp.concatenate([ref[s * L:(s + 1) * L, :] for s in range(pack)], axis=1)

    def tiled(x):
        return jnp.concatenate([x] * pack, axis=1)

    def conv(u, o):
        spec = _bdot(fwd_ref[...], u.astype(BF16))
        a, b = spec[:L], spec[L:]
        kr, kiz, krn = tiled(kr_ref[o]), tiled(kiz_ref[o]), tiled(krn_ref[o])
        top = (a * kr + b * kiz).astype(BF16)
        bot = (a * kiz - b * krn).astype(BF16)
        return _bdot(inv_ref[:, 0:L], top) + _bdot(inv_ref[:, L:2 * L], bot)

    v = lanes(v_ref)
    z = lanes(x1_ref) * (conv(v, 0) + v * tiled(skip_ref[0:1, :]))
    z = lanes(x2_ref) * (conv(z, 1) + z * tiled(skip_ref[1:2, :]))
    ct = o_ref.shape[1]
    for s in range(pack):
        o_ref[s * L:(s + 1) * L, :] = z[:, s * ct:(s + 1) * ct].astype(BF16)


def _longconv(grp, proj, spectra, skip, fwd, inv):
    L, d = grp.seq_len, grp.d
    ct = min(d, CONV_TILE_ELEMS // L)
    pack = max(1, CONV_MIN_LANES // ct)
    assert grp.n_seq % pack == 0
    per = d // ct
    kr, kiz, krn = spectra
    order = kr.shape[0]
    kspec = pl.BlockSpec((order, L, ct), lambda j, b: (0, 0, j))
    return pl.pallas_call(
        functools.partial(_longconv_kernel, L=L, pack=pack),
        grid=(per, grp.n_seq // pack),
        in_specs=[
            pl.BlockSpec((pack * L, ct), lambda j, b: (b, j)),
            pl.BlockSpec((pack * L, ct), lambda j, b: (b, per + j)),
            pl.BlockSpec((pack * L, ct), lambda j, b: (b, 2 * per + j)),
            kspec, kspec, kspec,
            pl.BlockSpec((order, ct), lambda j, b: (0, j)),
            pl.BlockSpec((2 * L, L), lambda j, b: (0, 0)),
            pl.BlockSpec((L, 2 * L), lambda j, b: (0, 0)),
        ],
        out_specs=pl.BlockSpec((pack * L, ct), lambda j, b: (b, j)),
        out_shape=jax.ShapeDtypeStruct((grp.rows, d), BF16),
        compiler_params=_params(("arbitrary", "arbitrary")),
        name=f"hyena_longconv_{L}",
    )(proj, proj, proj, kr, kiz, krn, skip, fwd, inv)


def _proj_res_kernel(*refs, has_bias):
    if has_bias:
        a_ref, w_ref, b_ref, x_ref, gate_ref, o_ref, wc_scr = refs
    else:
        a_ref, w_ref, x_ref, gate_ref, o_ref, wc_scr = refs
    j = pl.program_id(1)

    @pl.when(pl.program_id(0) == 0)
    def _():
        wc_scr[j] = w_ref[...].astype(BF16)

    m = _bdot(a_ref[...], wc_scr[j])
    if has_bias:
        m = m + b_ref[...]
    o_ref[...] = x_ref[...] + gate_ref[0] * m


def _proj_res(grp, a, w, b, x, mod3, layer):
    k, n = w.shape
    big = k * n * 2 > RES_WEIGHT_BYTES_FOR_WIDE_TILE
    tn = RES_COL_TILE if big else 2 * RES_COL_TILE
    nj = n // tn
    in_specs = [
        pl.BlockSpec((TOKEN_TILE, k), lambda i, j: (i, 0),
                     pipeline_mode=pl.Buffered(1) if big else None),
        pl.BlockSpec((k, tn), lambda i, j: (0, jnp.where(i == 0, j, nj - 1))),
    ]
    args = [a, w]
    if b is not None:
        in_specs.append(pl.BlockSpec((1, tn), lambda i, j: (0, j)))
        args.append(b.reshape(1, n))
    in_specs += [
        pl.BlockSpec((TOKEN_TILE, tn), lambda i, j: (i, j)),
        grp.mod_spec(layer, 2, tn, True),
    ]
    args += [x, mod3]
    return pl.pallas_call(
        functools.partial(_proj_res_kernel, has_bias=b is not None),
        grid=(grp.tiles, nj),
        in_specs=in_specs,
        out_specs=pl.BlockSpec((TOKEN_TILE, tn), lambda i, j: (i, j)),
        out_shape=jax.ShapeDtypeStruct((grp.rows, n), F32),
        scratch_shapes=[pltpu.VMEM((nj, k, tn), BF16)],
        compiler_params=_params(("arbitrary", "arbitrary")),
        name=f"proj_res_{k}_{grp.seq_len}",
    )(*args)


def _mlp_kernel(*refs, final_norm, cast, emit_cache):
    refs = list(refs)
    x_ref, g_ref, sc_ref, sh_ref, gate_ref, w1_ref, w2_ref = refs[:7]
    refs = refs[7:]
    fg_ref = refs.pop(0) if final_norm else None
    o_ref = refs.pop(0)
    w1c_ref, w2c_ref = (refs.pop(0), refs.pop(0)) if emit_cache else (None, None)
    (h_scr,) = refs
    f = pl.program_id(1)
    last_f = pl.num_programs(1) - 1

    def step(first, last):
        w1 = w1_ref[0].astype(BF16) if cast else w1_ref[...]
        w2 = w2_ref[0].astype(BF16) if cast else w2_ref[...]
        for r in range(x_ref.shape[0] // MLP_ROWS):
            rows = slice(r * MLP_ROWS, (r + 1) * MLP_ROWS)
            if first:
                h = _modulated_norm(x_ref[rows, :], g_ref[...], sc_ref[0], sh_ref[0]).astype(BF16)
                h_scr[rows, :] = h
            else:
                h = h_scr[rows, :]
            u = jnp.square(jnp.maximum(_bdot(h, w1), 0.0)).astype(BF16)
            acc = _bdot(u, w2)
            if not first:
                acc = o_ref[rows, :] + acc
            if last:
                acc = x_ref[rows, :] + gate_ref[0] * acc
                if final_norm:
                    ms = jnp.mean(acc * acc, axis=-1, keepdims=True)
                    acc = acc * lax.rsqrt(ms + EPS) * fg_ref[...]
            o_ref[rows, :] = acc
        if emit_cache:
            @pl.when(pl.program_id(0) == 0)
            def _():
                w1c_ref[...] = w1
                w2c_ref[...] = w2

    pl.when(f == 0)(lambda: step(True, False))
    pl.when(jnp.logical_and(f > 0, f < last_f))(lambda: step(False, False))
    pl.when(f == last_f)(lambda: step(False, True))


def _mlp(grp, x, norm_g, mod3, layer, w1, w2, final_g=None, emit_cache=False):
    d = grp.d
    cast = w1.dtype != BF16
    assert cast or not emit_cache
    ff = w1.shape[-1]
    tf = FF_TILE_F32 if cast else FF_TILE_BF16
    nf = ff // tf
    assert nf >= 2
    if cast:
        w_specs = [pl.BlockSpec((1, d, tf), lambda i, f: (layer, 0, f)),
                   pl.BlockSpec((1, tf, d), lambda i, f: (layer, f, 0))]
    else:
        w_specs = [pl.BlockSpec((d, tf), lambda i, f: (0, f)),
                   pl.BlockSpec((tf, d), lambda i, f: (f, 0))]
    in_specs = [
        pl.BlockSpec((TOKEN_TILE, d), lambda i, f: (i, 0), pipeline_mode=pl.Buffered(1)),
        pl.BlockSpec((1, d), lambda i, f: (0, 0)),
        grp.mod_spec(layer, 4, d, False),
        grp.mod_spec(layer, 3, d, False),
        grp.mod_spec(layer, 5, d, False),
    ] + w_specs
    args = [x, norm_g.reshape(1, d), mod3, mod3, mod3, w1, w2]
    if final_g is not None:
        in_specs.append(pl.BlockSpec((1, d), lambda i, f: (0, 0)))
        args.append(final_g.reshape(1, d))
    out_mode = pl.Buffered(1) if emit_cache else None
    out_specs = [pl.BlockSpec((TOKEN_TILE, d), lambda i, f: (i, 0), pipeline_mode=out_mode)]
    out_shape = [jax.ShapeDtypeStruct((grp.rows, d), F32)]
    if emit_cache:
        frozen = lambda i, f: jnp.where(i == 0, f, nf - 1)
        single = pl.Buffered(1)
        out_specs += [
            pl.BlockSpec((d, tf), lambda i, f: (0, frozen(i, f)), pipeline_mode=single),
            pl.BlockSpec((tf, d), lambda i, f: (frozen(i, f), 0), pipeline_mode=single)]
        out_shape += [jax.ShapeDtypeStruct((d, ff), BF16), jax.ShapeDtypeStruct((ff, d), BF16)]
    outs = pl.pallas_call(
        functools.partial(_mlp_kernel, final_norm=final_g is not None, cast=cast,
                          emit_cache=emit_cache),
        grid=(grp.tiles, nf),
        in_specs=in_specs,
        out_specs=out_specs,
        out_shape=out_shape,
        scratch_shapes=[pltpu.VMEM((TOKEN_TILE, d), BF16)],
        compiler_params=_params(("arbitrary", "arbitrary")),
        name=f"mlp_{layer}_{grp.seq_len}",
    )(*args)
    return outs[0], tuple(outs[1:])


def _rotary_tables(L, dk, width):
    n_rows = L // GRID_W
    rows = np.repeat(np.arange(n_rows, dtype=np.float64), GRID_W)
    cols = np.tile(np.arange(GRID_W, dtype=np.float64), n_rows)
    n_pairs = dk // 4
    inv = ROPE_BASE ** (-np.arange(n_pairs, dtype=np.float64) / n_pairs)
    ang = np.concatenate([rows[:, None] * inv, cols[:, None] * inv], axis=-1)
    cos = np.repeat(np.cos(ang), 2, axis=-1)
    sin = np.repeat(np.sin(ang), 2, axis=-1)
    sin[:, 0::2] *= -1.0
    reps = width // dk
    return (jnp.asarray(np.tile(cos, (1, reps)), F32), jnp.asarray(np.tile(sin, (1, reps)), F32))


def _qkvg_kernel(*refs, rotary, qk_tiles, k_tile0, k_scale):
    if rotary:
        x_ref, g_ref, sc_ref, sh_ref, w_ref, cos_ref, sin_ref, o_ref, h_scr = refs
    else:
        x_ref, g_ref, sc_ref, sh_ref, w_ref, o_ref, h_scr = refs
    j = pl.program_id(1)
    is_qk = j < qk_tiles
    scale = jnp.where(jnp.logical_and(j >= k_tile0, is_qk), k_scale, 1.0)

    def project(fill, rotate):
        if rotate:
            lane = lax.broadcasted_iota(jnp.int32, (h_scr.shape[0], SUB_COLS), 1)
            even = (lane & 1) == 0
        for s in range(o_ref.shape[1] // SUB_COLS):
            cols = slice(s * SUB_COLS, (s + 1) * SUB_COLS)
            acc = _normed_dot(fill and s == 0, x_ref, g_ref, sc_ref, sh_ref, h_scr,
                              w_ref[:, cols].astype(BF16))
            if rotate:
                partner = jnp.where(even, pltpu.roll(acc, SUB_COLS - 1, 1), pltpu.roll(acc, 1, 1))
                acc = acc * cos_ref[...] + partner * sin_ref[...]
            o_ref[:, cols] = (acc * scale).astype(BF16)

    pl.when(j == 0)(lambda: project(True, rotary))
    if rotary:
        pl.when(jnp.logical_and(j > 0, is_qk))(lambda: project(False, True))
        pl.when(jnp.logical_not(is_qk))(lambda: project(False, False))
    else:
        pl.when(j > 0)(lambda: project(False, False))


def _qkvg(grp, x, norm_g, mod3, layer, w, dk, rotary):
    d = grp.d
    n = w.shape[1]
    in_specs = [
        pl.BlockSpec((TOKEN_TILE, d), lambda i, j: (i, 0)),
        pl.BlockSpec((1, d), lambda i, j: (0, 0)),
        grp.mod_spec(layer, 1, d, False),
        grp.mod_spec(layer, 0, d, False),
        pl.BlockSpec((d, COL_TILE), lambda i, j: (0, j)),
    ]
    args = [x, norm_g.reshape(1, d), mod3, mod3, w]
    if rotary:
        assert grp.seq_len == TOKEN_TILE
        cos, sin = _rotary_tables(grp.seq_len, dk, SUB_COLS)
        table = pl.BlockSpec((TOKEN_TILE, SUB_COLS), lambda i, j: (0, 0))
        in_specs += [table, table]
        args += [cos, sin]
    kern = functools.partial(_qkvg_kernel, rotary=rotary, qk_tiles=2 * d // COL_TILE,
                             k_tile0=d // COL_TILE, k_scale=dk ** -0.5)
    return pl.pallas_call(
        kern,
        grid=(grp.tiles, n // COL_TILE),
        in_specs=in_specs,
        out_specs=pl.BlockSpec((TOKEN_TILE, COL_TILE), lambda i, j: (i, j)),
        out_shape=jax.ShapeDtypeStruct((grp.rows, n), BF16),
        scratch_shapes=[pltpu.VMEM((TOKEN_TILE, d), BF16)],
        compiler_params=_params(("arbitrary", "arbitrary")),
        name=f"ret_qkvg_{grp.seq_len}",
    )(*args)


def _log_sigmoid(x):
    return jnp.minimum(x, 0.0) - jnp.log1p(jnp.exp(-jnp.abs(x)))


def _retention_kernel(*refs, L, C, pack, has_s0, emit_state):
    refs = list(refs)
    decay_ref, q_ref, k_ref, v_ref, g_ref, gn_ref = refs[:6]
    refs = refs[6:]
    s0_ref = refs.pop(0) if has_s0 else None
    y_ref = refs.pop(0)
    so_ref = refs.pop(0) if emit_state else None
    mask_scr, dec_scr, cdec_scr, sb_scr = refs

    head = pl.program_id(0)
    dk = q_ref.shape[1]
    dv = v_ref.shape[1]
    nc = L // C

    @pl.when(pl.program_id(1) == 0)
    def _():
        def log_decay(direction, width):
            return _log_sigmoid(jnp.full((1, width), decay_ref[direction, head], F32))

        ri = lax.broadcasted_iota(jnp.int32, (C, C), 0)
        ci = lax.broadcasted_iota(jnp.int32, (C, C), 1)
        diff = (ri - ci).astype(F32)
        lower = diff >= 0
        upper = diff <= 0
        mask_scr[...] = (
            jnp.where(lower, jnp.exp(jnp.where(lower, diff, 0.0) * log_decay(0, C)), 0.0)
            + jnp.where(upper, jnp.exp(jnp.where(upper, -diff, 0.0) * log_decay(1, C)), 0.0))
        rk = lax.broadcasted_iota(jnp.int32, (C, dk), 0).astype(F32)
        dec_scr[0] = jnp.exp((rk + 1.0) * log_decay(0, dk))
        dec_scr[1] = jnp.exp((C - 1.0 - rk) * log_decay(0, dk))
        dec_scr[2] = jnp.exp((C - rk) * log_decay(1, dk))
        dec_scr[3] = jnp.exp(rk * log_decay(1, dk))
        cdec_scr[0:1, :] = jnp.exp(C * log_decay(0, dv))
        cdec_scr[1:2, :] = jnp.exp(C * log_decay(1, dv))

    gn = gn_ref[...]

    def advance(state, dec_row, upd):
        return upd if state is None else state * cdec_scr[dec_row:dec_row + 1, :] + upd

    for s in range(pack):
        base = s * L

        def kv(n, which):
            rows = slice(base + n * C, base + (n + 1) * C)
            kd = (k_ref[rows, :].astype(F32) * dec_scr[which]).astype(BF16)
            return lax.dot_general(kd, v_ref[rows, :], (((0,), (0,)), ((), ())),
                                   preferred_element_type=F32)

        state_b = s0_ref[s, 0, 1, 0] if has_s0 else None
        has_b = [False] * nc
        for n in reversed(range(nc)):
            if state_b is not None:
                sb_scr[n] = state_b.astype(BF16)
                has_b[n] = True
            if n > 0 or emit_state:
                state_b = advance(state_b, 1, kv(n, 3))
        if emit_state:
            so_ref[s, 0, 1, 0] = state_b

        state_f = s0_ref[s, 0, 0, 0] if has_s0 else None
        for n in range(nc):
            rows = slice(base + n * C, base + (n + 1) * C)
            q = q_ref[rows, :]
            scores = lax.dot_general(q, k_ref[rows, :], (((1,), (1,)), ((), ())),
                                     preferred_element_type=F32)
            y = _bdot((scores * mask_scr[...]).astype(BF16), v_ref[rows, :])
            qf = q.astype(F32)
            if state_f is not None:
                y = y + _bdot((qf * dec_scr[0]).astype(BF16), state_f.astype(BF16))
            if has_b[n]:
                y = y + _bdot((qf * dec_scr[2]).astype(BF16), sb_scr[n])
            mu = jnp.mean(y, axis=-1, keepdims=True)
            yc = y - mu
            var = jnp.mean(yc * yc, axis=-1, keepdims=True)
            yn = (yc * lax.rsqrt(var + EPS)) * gn
            gate = g_ref[rows, :].astype(F32)
            y_ref[rows, :] = ((gate * jax.nn.sigmoid(gate)) * yn).astype(BF16)
            if n < nc - 1 or emit_state:
                state_f = advance(state_f, 0, kv(n, 1))
        if emit_state:
            so_ref[s, 0, 0, 0] = state_f


def _retention(grp, qkvg, decay, gn_g, s0, heads, dk, dv, emit_state):
    L = grp.seq_len
    C = min(L, RET_CHUNK)
    pack = max(1, min(grp.n_seq, RET_STEP_ROWS // L))
    assert grp.n_seq % pack == 0
    kq = heads
    kv0 = 2 * heads * dk // dv
    kg0 = kv0 + heads
    rows = pack * L
    in_specs = [
        pl.BlockSpec(memory_space=pltpu.SMEM),
        pl.BlockSpec((rows, dk), lambda h, b: (b, h)),
        pl.BlockSpec((rows, dk), lambda h, b: (b, kq + h)),
        pl.BlockSpec((rows, dv), lambda h, b: (b, kv0 + h)),
        pl.BlockSpec((rows, dv), lambda h, b: (b, kg0 + h)),
        pl.BlockSpec((1, dv), lambda h, b: (0, h)),
    ]
    args = [decay, qkvg, qkvg, qkvg, qkvg, gn_g.reshape(1, heads * dv)]
    state_block = (pack, 1, 2, 1, dk, dv)
    state_index = lambda h, b: (b, 0, 0, h, 0, 0)
    if s0 is not None:
        in_specs.append(pl.BlockSpec(state_block, state_index))
        args.append(s0)
    out_specs = [pl.BlockSpec((rows, dv), lambda h, b: (b, h))]
    out_shape = [jax.ShapeDtypeStruct((grp.rows, heads * dv), BF16)]
    if emit_state:
        out_specs.append(pl.BlockSpec(state_block, state_index))
        out_shape.append(jax.ShapeDtypeStruct((grp.n_seq, 1, 2, heads, dk, dv), F32))
    kern = functools.partial(_retention_kernel, L=L, C=C, pack=pack, has_s0=s0 is not None,
                             emit_state=emit_state)
    return pl.pallas_call(
        kern,
        grid=(heads, grp.n_seq // pack),
        in_specs=in_specs,
        out_specs=out_specs,
        out_shape=out_shape,
        scratch_shapes=[
            pltpu.VMEM((C, C), F32),
            pltpu.VMEM((4, C, dk), F32),
            pltpu.VMEM((8, dv), F32),
            pltpu.VMEM((L // C, dk, dv), BF16),
        ],
        compiler_params=_params(("arbitrary", "arbitrary")),
        name=f"retention_{L}",
    )(*args)


def kernel(x_prompt, x_sample, state_ret, c, c_ctx, w_ada, b_ada, norm_g, final_g, hy_w_in, hy_b_in, hy_conv_w, hy_conv_b, hy_f_w1, hy_f_b1, hy_f_w2, hy_f_b2, hy_f_w3, hy_f_b3, hy_f_freq, hy_f_wout, hy_f_skip, hy_w_out, hy_b_out, ret_w_qkvg, ret_decay, ret_gn_g, ret_w_o, mlp_w1, mlp_w2):
    n_prompt, prompt_len, d = x_prompt.shape
    n_sample, sample_len, _ = x_sample.shape
    heads, dk, dv = state_ret.shape[3:]
    depth = w_ada.shape[0]
    assert depth == 2 and hy_w_in.shape[0] == 1 and ret_w_qkvg.shape[0] == 1
    assert n_sample + 1 <= MOD_ROWS

    cond = jnp.concatenate(
        [c_ctx[None, :], c, jnp.zeros((MOD_ROWS - 1 - n_sample, d), F32)], axis=0)
    mod = _ada_mod(cond, w_ada, b_ada)
    mod3 = mod.reshape(depth * MOD_ROWS, 1, mod.shape[-1])

    w_in, w_out, w_qkvg, w_o, w1, w2 = (hy_w_in[0], hy_w_out[0], ret_w_qkvg[0], ret_w_o[0],
                                        mlp_w1, mlp_w2)

    def run(grp, x, s0, latent, mlp_w):
        def mlp(x, layer, **kw):
            if mlp_w is None:
                return _mlp(grp, x, norm_g[layer, 1], mod3, layer, w1, w2, emit_cache=True, **kw)
            return _mlp(grp, x, norm_g[layer, 1], mod3, layer, *mlp_w[layer], **kw)

        L = grp.seq_len
        proj = _hyena_in(grp, x, norm_g[0, 0], mod3, 0, w_in, hy_b_in[0], hy_conv_w[0],
                         hy_conv_b[0])
        fwd, inv = _dft_matrices(L)
        spectra = _filter_spectra(L, d, fwd, hy_f_w1[0], hy_f_b1[0], hy_f_w2[0], hy_f_b2[0],
                                  hy_f_w3[0], hy_f_b3[0], hy_f_freq[0], hy_f_wout[0])
        z = _longconv(grp, proj, spectra, hy_f_skip[0], fwd, inv)
        x = _proj_res(grp, z, w_out, hy_b_out[0], x, mod3, 0)
        x, cache0 = mlp(x, 0)
        qkvg = _qkvg(grp, x, norm_g[1, 0], mod3, 1, w_qkvg, dk, rotary=latent)
        outs = _retention(grp, qkvg, ret_decay[0], ret_gn_g[0], s0, heads, dk, dv,
                          emit_state=s0 is None)
        x = _proj_res(grp, outs[0], w_o, None, x, mod3, 1)
        y, cache1 = mlp(x, 1, final_g=final_g)
        return y, (outs[1] if s0 is None else None), (cache0, cache1)

    prompt = _Group(n_prompt, prompt_len, d, mod_row0=0, mod_row_step=0)
    sample = _Group(n_sample, sample_len, d, mod_row0=1, mod_row_step=1)
    y_prompt, new_state, mlp_bf16 = run(prompt, x_prompt.reshape(-1, d), None, False, None)
    y_sample, _, _ = run(sample, x_sample.reshape(-1, d), state_ret, True, mlp_bf16)
    return (y_prompt.reshape(x_prompt.shape), y_sample.reshape(x_sample.shape),
            new_state.astype(x_prompt.dtype))
```

```python
import functools
import math

import numpy as np
import jax
import jax.numpy as jnp
from jax import lax
from jax.experimental import pallas as pl
from jax.experimental.pallas import tpu as pltpu

F32 = jnp.float32
BF16 = jnp.bfloat16

EPS = 1e-6
GRID_W = 64
ROPE_BASE = 10000.0
FILTER_BANDS = 16
DECAY_TARGET = 1e-2
MIN_DECAY = math.log(DECAY_TARGET) / 1.5
MAX_DECAY = math.log(DECAY_TARGET) / 0.3

V7X_VMEM_BYTES = 64 * 1024 * 1024
VMEM_LIMIT = V7X_VMEM_BYTES - 8 * 1024 * 1024

TOKEN_TILE = 1024
COL_TILE = 1024
SUB_COLS = 512
RES_COL_TILE = 512
RES_WEIGHT_CACHE_BYTES = 8 * 1024 * 1024
FF_TILE = 1024
MLP_ROWS = 256
CONV_TILE_ELEMS = 256 * 1024
CONV_MIN_LANES = 512
FILT_CH_TILE = 512
RET_CHUNK = 256
RET_STEP_ROWS = 1024
MOD_ROWS = 8


def _params(semantics):
    return pltpu.CompilerParams(dimension_semantics=semantics, vmem_limit_bytes=VMEM_LIMIT)


def _bdot(a, b):
    return jnp.dot(a, b, preferred_element_type=F32)


def _fdot(a, b):
    return jnp.dot(a, b, preferred_element_type=F32, precision=lax.Precision.HIGHEST)


def _modulated_norm(x, g, scale, shift):
    ms = jnp.mean(x * x, axis=-1, keepdims=True)
    return (x * lax.rsqrt(ms + EPS) * g) * (1.0 + scale) + shift


def _normed_dot(fill, x_ref, g_ref, sc_ref, sh_ref, h_scr, w):
    if not fill:
        return _bdot(h_scr[...], w)
    parts = []
    for r in range(x_ref.shape[0] // MLP_ROWS):
        rows = slice(r * MLP_ROWS, (r + 1) * MLP_ROWS)
        h = _modulated_norm(x_ref[rows, :], g_ref[...], sc_ref[0], sh_ref[0]).astype(BF16)
        h_scr[rows, :] = h
        parts.append(_bdot(h, w))
    return jnp.concatenate(parts, axis=0)


class _Group:
    def __init__(self, n_seq, seq_len, d, mod_row0, mod_row_step):
        assert seq_len & (seq_len - 1) == 0
        assert (n_seq * seq_len) % TOKEN_TILE == 0
        assert TOKEN_TILE % seq_len == 0 or (mod_row_step == 0 and seq_len % TOKEN_TILE == 0)
        assert mod_row_step == 0 or seq_len == TOKEN_TILE
        self.n_seq, self.seq_len, self.d = n_seq, seq_len, d
        self.rows = n_seq * seq_len
        self.tiles = self.rows // TOKEN_TILE
        self.mod_row0, self.mod_row_step = mod_row0, mod_row_step

    def mod_spec(self, layer, chunk, width, tiled):
        per = self.d // width
        base = layer * MOD_ROWS + self.mod_row0
        step = self.mod_row_step

        def index(i, j):
            return (base + step * i, 0, chunk * per + (j if tiled else 0))

        return pl.BlockSpec((1, 1, width), index)


def _ada_kernel(cond_ref, w_ref, b_ref, o_ref):
    c = cond_ref[...]
    a = (c * jax.nn.sigmoid(c)).astype(BF16)
    o_ref[0] = _bdot(a, w_ref[0].astype(BF16)) + b_ref[0]


def _ada_mod(cond, w_ada, b_ada):
    depth, d, n = w_ada.shape
    tn = 1024
    return pl.pallas_call(
        _ada_kernel,
        grid=(depth, n // tn),
        in_specs=[
            pl.BlockSpec((MOD_ROWS, d), lambda l, j: (0, 0)),
            pl.BlockSpec((1, d, tn), lambda l, j: (l, 0, j)),
            pl.BlockSpec((1, 1, tn), lambda l, j: (l, 0, j)),
        ],
        out_specs=pl.BlockSpec((1, MOD_ROWS, tn), lambda l, j: (l, 0, j)),
        out_shape=jax.ShapeDtypeStruct((depth, MOD_ROWS, n), F32),
        compiler_params=_params(("arbitrary", "arbitrary")),
        name="ada_mod",
    )(cond, w_ada, b_ada.reshape(depth, 1, n))


def _hyena_in_kernel(x_ref, g_ref, sc_ref, sh_ref, w_ref, b_ref, cw_ref, cb_ref, o_ref, h_scr,
                     *, seq_len):
    def project(fill):
        rows = h_scr.shape[0]
        pos = lax.broadcasted_iota(jnp.int32, (rows, SUB_COLS), 0) & (seq_len - 1)
        first = pos == 0
        last = pos == seq_len - 1
        for s in range(o_ref.shape[1] // SUB_COLS):
            cols = slice(s * SUB_COLS, (s + 1) * SUB_COLS)
            p = _normed_dot(fill and s == 0, x_ref, g_ref, sc_ref, sh_ref, h_scr,
                            w_ref[:, cols].astype(BF16)) + b_ref[:, cols]
            prev = jnp.where(first, 0.0, pltpu.roll(p, 1, 0))
            nxt = jnp.where(last, 0.0, pltpu.roll(p, rows - 1, 0))
            o_ref[:, cols] = (prev * cw_ref[0:1, cols] + p * cw_ref[1:2, cols]
                              + nxt * cw_ref[2:3, cols] + cb_ref[:, cols])

    j = pl.program_id(1)
    pl.when(j == 0)(lambda: project(True))
    pl.when(j > 0)(lambda: project(False))


def _hyena_in(grp, x, norm_g, mod3, layer, w, b, conv_w, conv_b):
    d = grp.d
    n = w.shape[1]
    return pl.pallas_call(
        functools.partial(_hyena_in_kernel, seq_len=grp.seq_len),
        grid=(grp.tiles, n // COL_TILE),
        in_specs=[
            pl.BlockSpec((TOKEN_TILE, d), lambda i, j: (i, 0)),
            pl.BlockSpec((1, d), lambda i, j: (0, 0)),
            grp.mod_spec(layer, 1, d, False),
            grp.mod_spec(layer, 0, d, False),
            pl.BlockSpec((d, COL_TILE), lambda i, j: (0, j)),
            pl.BlockSpec((1, COL_TILE), lambda i, j: (0, j)),
            pl.BlockSpec((3, COL_TILE), lambda i, j: (0, j)),
            pl.BlockSpec((1, COL_TILE), lambda i, j: (0, j)),
        ],
        out_specs=pl.BlockSpec((TOKEN_TILE, COL_TILE), lambda i, j: (i, j)),
        out_shape=jax.ShapeDtypeStruct((grp.rows, n), F32),
        scratch_shapes=[pltpu.VMEM((TOKEN_TILE, d), BF16)],
        compiler_params=_params(("arbitrary", "arbitrary")),
        name=f"hyena_in_{grp.seq_len}",
    )(x, norm_g.reshape(1, d), mod3, mod3, w, b.reshape(1, n), conv_w, conv_b.reshape(1, n))


def _dft_matrices(L):
    f = np.arange(L, dtype=np.int64)[:, None]
    t = np.arange(L, dtype=np.int64)[None, :]
    ang = ((f * t) % (2 * L)).astype(np.float64) * (np.pi / L)
    cos, sin = np.cos(ang), np.sin(ang)
    nyq = np.where(np.arange(L) % 2 == 0, 1.0, -1.0)
    bot = sin.copy()
    bot[0, :] = nyq
    fwd = np.concatenate([cos, bot], axis=0)
    wgt = np.full((L,), 2.0)
    wgt[0] = 1.0
    inv_cos = (cos * wgt[:, None]).T / (2 * L)
    inv_sin = (-2.0 * sin).T / (2 * L)
    inv_sin[:, 0] = nyq / (2 * L)
    inv = np.concatenate([inv_cos, inv_sin], axis=1)
    return jnp.asarray(fwd, F32).astype(BF16), jnp.asarray(inv, F32).astype(BF16)


def _filter_features(L, width):
    pos = np.arange(L, dtype=np.float64)
    t = pos / L
    omega = 2.0 * math.pi * pos / L
    bands = np.linspace(1e-4, FILTER_BANDS - 1, FILTER_BANDS)
    ang = omega[:, None] * bands[None, :]
    feat = np.concatenate([t[:, None], np.cos(ang), -np.sin(ang)], axis=-1)
    feat = np.pad(feat, ((0, 0), (0, width - feat.shape[1])))
    return jnp.asarray(feat, F32), jnp.asarray(t[:, None], F32)


def _filter_kernel(feat_ref, w1_ref, b1_ref, w2_ref, b2_ref, w3_ref, b3_ref, freq_ref,
                   wof_ref, wob_ref, t_ref, delta_ref, fwd_ref,
                   kr_ref, kiz_ref, krn_ref, h_scr, *, L):
    first = jnp.logical_and(pl.program_id(0) == 0, pl.program_id(1) == 0)

    @pl.when(first)
    def _():
        h = jnp.sin(freq_ref[0:1, :] * (_fdot(feat_ref[...], w1_ref[...]) + b1_ref[...]))
        h = jnp.sin(freq_ref[1:2, :] * (_fdot(h, w2_ref[...]) + b2_ref[...]))
        h = jnp.sin(freq_ref[2:3, :] * (_fdot(h, w3_ref[...]) + b3_ref[...]))
        h_scr[...] = h

    h = h_scr[...].astype(BF16)
    window = jnp.exp(-t_ref[...] * delta_ref[...])
    hf = _bdot(h, wof_ref[...].astype(BF16)) * window
    hb = _bdot(h, wob_ref[...].astype(BF16)) * window
    row = lax.broadcasted_iota(jnp.int32, hf.shape, 0)
    hb = jnp.where(row == 0, 0.0, hb)
    tap_sum = hf + hb
    tap_dif = hb - hf
    kr = _bdot(fwd_ref[0:L, :], tap_sum.astype(BF16))
    ki = _bdot(fwd_ref[L:2 * L, :], tap_dif.astype(BF16))
    sign = jnp.where((row & 1) == 0, 1.0, -1.0)
    k_nyq = jnp.sum(tap_sum * sign, axis=0, keepdims=True)
    kr_ref[0] = kr
    kiz_ref[0] = jnp.where(row == 0, 0.0, ki)
    krn_ref[0] = jnp.where(row == 0, -k_nyq, kr)


def _filter_spectra(L, d, fwd, w1, b1, w2, b2, w3, b3, freq, wout):
    width = w2.shape[0]
    order = wout.shape[1] // (2 * d)
    feat, t = _filter_features(L, width)
    w1p = jnp.pad(w1, ((0, width - w1.shape[0]), (0, 0)))
    deltas = jnp.asarray(np.abs(np.linspace(MIN_DECAY, MAX_DECAY, d))[None, :], F32)
    ct = FILT_CH_TILE
    per = d // ct
    small = lambda shape: pl.BlockSpec(shape, lambda o, j: (0, 0))
    out_spec = pl.BlockSpec((1, L, ct), lambda o, j: (o, 0, j))
    out_shape = jax.ShapeDtypeStruct((order, L, d), F32)
    return pl.pallas_call(
        functools.partial(_filter_kernel, L=L),
        grid=(order, per),
        in_specs=[
            small((L, width)), small((width, width)), small((1, width)),
            small((width, width)), small((1, width)),
            small((width, width)), small((1, width)), small((3, width)),
            pl.BlockSpec((width, ct), lambda o, j: (0, (2 * o) * per + j)),
            pl.BlockSpec((width, ct), lambda o, j: (0, (2 * o + 1) * per + j)),
            small((L, 1)),
            pl.BlockSpec((1, ct), lambda o, j: (0, j)),
            small((2 * L, L)),
        ],
        out_specs=[out_spec, out_spec, out_spec],
        out_shape=[out_shape, out_shape, out_shape],
        scratch_shapes=[pltpu.VMEM((L, width), F32)],
        compiler_params=_params(("arbitrary", "arbitrary")),
        name=f"hyena_filter_{L}",
    )(feat, w1p, b1.reshape(1, width), w2, b2.reshape(1, width), w3, b3.reshape(1, width),
      freq, wout, wout, t, deltas, fwd)


def _longconv_kernel(v_ref, x1_ref, x2_ref, kr_ref, kiz_ref, krn_ref, skip_ref, fwd_ref, inv_ref,
                     o_ref, *, L, pack):
    def lanes(ref):
        return jnp.concatenate([ref[s * L:(s + 1) * L, :] for s in range(pack)], axis=1)

    def tiled(x):
        return jnp.concatenate([x] * pack, axis=1)

    def conv(u, o):
        spec = _bdot(fwd_ref[...], u.astype(BF16))
        a, b = spec[:L], spec[L:]
        kr, kiz, krn = tiled(kr_ref[o]), tiled(kiz_ref[o]), tiled(krn_ref[o])
        top = (a * kr + b * kiz).astype(BF16)
        bot = (a * kiz - b * krn).astype(BF16)
        return _bdot(inv_ref[:, 0:L], top) + _bdot(inv_ref[:, L:2 * L], bot)

    v = lanes(v_ref)
    z = lanes(x1_ref) * (conv(v, 0) + v * tiled(skip_ref[0:1, :]))
    z = lanes(x2_ref) * (conv(z, 1) + z * tiled(skip_ref[1:2, :]))
    ct = o_ref.shape[1]
    for s in range(pack):
        o_ref[s * L:(s + 1) * L, :] = z[:, s * ct:(s + 1) * ct].astype(BF16)


def _longconv(grp, proj, spectra, skip, fwd, inv):
    L, d = grp.seq_len, grp.d
    ct = min(d, CONV_TILE_ELEMS // L)
    pack = max(1, CONV_MIN_LANES // ct)
    assert grp.n_seq % pack == 0
    per = d // ct
    kr, kiz, krn = spectra
    order = kr.shape[0]
    kspec = pl.BlockSpec((order, L, ct), lambda j, b: (0, 0, j))
    return pl.pallas_call(
        functools.partial(_longconv_kernel, L=L, pack=pack),
        grid=(per, grp.n_seq // pack),
        in_specs=[
            pl.BlockSpec((pack * L, ct), lambda j, b: (b, j)),
            pl.BlockSpec((pack * L, ct), lambda j, b: (b, per + j)),
            pl.BlockSpec((pack * L, ct), lambda j, b: (b, 2 * per + j)),
            kspec, kspec, kspec,
            pl.BlockSpec((order, ct), lambda j, b: (0, j)),
            pl.BlockSpec((2 * L, L), lambda j, b: (0, 0)),
            pl.BlockSpec((L, 2 * L), lambda j, b: (0, 0)),
        ],
        out_specs=pl.BlockSpec((pack * L, ct), lambda j, b: (b, j)),
        out_shape=jax.ShapeDtypeStruct((grp.rows, d), BF16),
        compiler_params=_params(("arbitrary", "arbitrary")),
        name=f"hyena_longconv_{L}",
    )(proj, proj, proj, kr, kiz, krn, skip, fwd, inv)


def _proj_res_kernel(*refs, has_bias, cached):
    refs = list(refs)
    a_ref, w_ref = refs[:2]
    refs = refs[2:]
    b_ref = refs.pop(0) if has_bias else None
    x_ref, gate_ref, o_ref = refs[:3]
    if cached:
        wc_scr = refs[3]
        j = pl.program_id(1)

        @pl.when(pl.program_id(0) == 0)
        def _():
            wc_scr[j] = w_ref[...].astype(BF16)

        w = wc_scr[j]
    else:
        w = w_ref[...].astype(BF16)
    m = _bdot(a_ref[...], w)
    if has_bias:
        m = m + b_ref[...]
    o_ref[...] = x_ref[...] + gate_ref[0] * m


def _proj_res(grp, a, w, b, x, mod3, layer):
    k, n = w.shape
    cached = k * n * 2 <= RES_WEIGHT_CACHE_BYTES
    tn = 2 * RES_COL_TILE if cached else RES_COL_TILE
    nj = n // tn
    if cached:
        w_index = lambda i, j: (0, jnp.where(i == 0, j, nj - 1))
    else:
        w_index = lambda i, j: (0, j)
    in_specs = [
        pl.BlockSpec((TOKEN_TILE, k), lambda i, j: (i, 0)),
        pl.BlockSpec((k, tn), w_index),
    ]
    args = [a, w]
    if b is not None:
        in_specs.append(pl.BlockSpec((1, tn), lambda i, j: (0, j)))
        args.append(b.reshape(1, n))
    in_specs += [
        pl.BlockSpec((TOKEN_TILE, tn), lambda i, j: (i, j)),
        grp.mod_spec(layer, 2, tn, True),
    ]
    args += [x, mod3]
    return pl.pallas_call(
        functools.partial(_proj_res_kernel, has_bias=b is not None, cached=cached),
        grid=(grp.tiles, nj),
        in_specs=in_specs,
        out_specs=pl.BlockSpec((TOKEN_TILE, tn), lambda i, j: (i, j)),
        out_shape=jax.ShapeDtypeStruct((grp.rows, n), F32),
        scratch_shapes=[pltpu.VMEM((nj, k, tn), BF16)] if cached else [],
        compiler_params=_params(("arbitrary", "arbitrary")),
        name=f"proj_res_{k}_{grp.seq_len}",
    )(*args)


def _mlp_kernel(*refs, final_norm):
    if final_norm:
        x_ref, g_ref, sc_ref, sh_ref, gate_ref, w1_ref, w2_ref, fg_ref, o_ref, h_scr = refs
    else:
        x_ref, g_ref, sc_ref, sh_ref, gate_ref, w1_ref, w2_ref, o_ref, h_scr = refs
    f = pl.program_id(1)
    last_f = pl.num_programs(1) - 1

    def step(first, last):
        w1 = w1_ref[0]
        w2 = w2_ref[0]
        for r in range(x_ref.shape[0] // MLP_ROWS):
            rows = slice(r * MLP_ROWS, (r + 1) * MLP_ROWS)
            if first:
                h = _modulated_norm(x_ref[rows, :], g_ref[...], sc_ref[0], sh_ref[0]).astype(BF16)
                h_scr[rows, :] = h
            else:
                h = h_scr[rows, :]
            u = jnp.square(jnp.maximum(_bdot(h, w1), 0.0)).astype(BF16)
            acc = _bdot(u, w2)
            if not first:
                acc = o_ref[rows, :] + acc
            if last:
                acc = x_ref[rows, :] + gate_ref[0] * acc
                if final_norm:
                    ms = jnp.mean(acc * acc, axis=-1, keepdims=True)
                    acc = acc * lax.rsqrt(ms + EPS) * fg_ref[...]
            o_ref[rows, :] = acc

    pl.when(f == 0)(lambda: step(True, False))
    pl.when(jnp.logical_and(f > 0, f < last_f))(lambda: step(False, False))
    pl.when(f == last_f)(lambda: step(False, True))


def _mlp(grp, x, norm_g, mod3, layer, w1, w2, final_g=None):
    d = grp.d
    ff = w1.shape[2]
    nf = ff // FF_TILE
    assert nf >= 2
    in_specs = [
        pl.BlockSpec((TOKEN_TILE, d), lambda i, f: (i, 0), pipeline_mode=pl.Buffered(1)),
        pl.BlockSpec((1, d), lambda i, f: (0, 0)),
        grp.mod_spec(layer, 4, d, False),
        grp.mod_spec(layer, 3, d, False),
        grp.mod_spec(layer, 5, d, False),
        pl.BlockSpec((1, d, FF_TILE), lambda i, f: (layer, 0, f)),
        pl.BlockSpec((1, FF_TILE, d), lambda i, f: (layer, f, 0)),
    ]
    args = [x, norm_g.reshape(1, d), mod3, mod3, mod3, w1, w2]
    if final_g is not None:
        in_specs.append(pl.BlockSpec((1, d), lambda i, f: (0, 0)))
        args.append(final_g.reshape(1, d))
    return pl.pallas_call(
        functools.partial(_mlp_kernel, final_norm=final_g is not None),
        grid=(grp.tiles, nf),
        in_specs=in_specs,
        out_specs=pl.BlockSpec((TOKEN_TILE, d), lambda i, f: (i, 0)),
        out_shape=jax.ShapeDtypeStruct((grp.rows, d), F32),
        scratch_shapes=[pltpu.VMEM((TOKEN_TILE, d), BF16)],
        compiler_params=_params(("arbitrary", "arbitrary")),
        name=f"mlp_{layer}_{grp.seq_len}",
    )(*args)


def _rotary_tables(L, dk, width):
    n_rows = L // GRID_W
    rows = np.repeat(np.arange(n_rows, dtype=np.float64), GRID_W)
    cols = np.tile(np.arange(GRID_W, dtype=np.float64), n_rows)
    n_pairs = dk // 4
    inv = ROPE_BASE ** (-np.arange(n_pairs, dtype=np.float64) / n_pairs)
    ang = np.concatenate([rows[:, None] * inv, cols[:, None] * inv], axis=-1)
    cos = np.repeat(np.cos(ang), 2, axis=-1)
    sin = np.repeat(np.sin(ang), 2, axis=-1)
    sin[:, 0::2] *= -1.0
    reps = width // dk
    return (jnp.asarray(np.tile(cos, (1, reps)), F32), jnp.asarray(np.tile(sin, (1, reps)), F32))


def _qkvg_kernel(*refs, rotary, qk_tiles, k_tile0, k_scale):
    if rotary:
        x_ref, g_ref, sc_ref, sh_ref, w_ref, cos_ref, sin_ref, o_ref, h_scr = refs
    else:
        x_ref, g_ref, sc_ref, sh_ref, w_ref, o_ref, h_scr = refs
    j = pl.program_id(1)
    is_qk = j < qk_tiles
    scale = jnp.where(jnp.logical_and(j >= k_tile0, is_qk), k_scale, 1.0)

    def project(fill, rotate):
        if rotate:
            lane = lax.broadcasted_iota(jnp.int32, (h_scr.shape[0], SUB_COLS), 1)
            even = (lane & 1) == 0
        for s in range(o_ref.shape[1] // SUB_COLS):
            cols = slice(s * SUB_COLS, (s + 1) * SUB_COLS)
            acc = _normed_dot(fill and s == 0, x_ref, g_ref, sc_ref, sh_ref, h_scr,
                              w_ref[:, cols].astype(BF16))
            if rotate:
                partner = jnp.where(even, pltpu.roll(acc, SUB_COLS - 1, 1), pltpu.roll(acc, 1, 1))
                acc = acc * cos_ref[...] + partner * sin_ref[...]
            o_ref[:, cols] = (acc * scale).astype(BF16)

    pl.when(j == 0)(lambda: project(True, rotary))
    if rotary:
        pl.when(jnp.logical_and(j > 0, is_qk))(lambda: project(False, True))
        pl.when(jnp.logical_not(is_qk))(lambda: project(False, False))
    else:
        pl.when(j > 0)(lambda: project(False, False))


def _qkvg(grp, x, norm_g, mod3, layer, w, dk, rotary):
    d = grp.d
    n = w.shape[1]
    in_specs = [
        pl.BlockSpec((TOKEN_TILE, d), lambda i, j: (i, 0)),
        pl.BlockSpec((1, d), lambda i, j: (0, 0)),
        grp.mod_spec(layer, 1, d, False),
        grp.mod_spec(layer, 0, d, False),
        pl.BlockSpec((d, COL_TILE), lambda i, j: (0, j)),
    ]
    args = [x, norm_g.reshape(1, d), mod3, mod3, w]
    if rotary:
        assert grp.seq_len == TOKEN_TILE
        cos, sin = _rotary_tables(grp.seq_len, dk, SUB_COLS)
        table = pl.BlockSpec((TOKEN_TILE, SUB_COLS), lambda i, j: (0, 0))
        in_specs += [table, table]
        args += [cos, sin]
    kern = functools.partial(_qkvg_kernel, rotary=rotary, qk_tiles=2 * d // COL_TILE,
                             k_tile0=d // COL_TILE, k_scale=dk ** -0.5)
    return pl.pallas_call(
        kern,
        grid=(grp.tiles, n // COL_TILE),
        in_specs=in_specs,
        out_specs=pl.BlockSpec((TOKEN_TILE, COL_TILE), lambda i, j: (i, j)),
        out_shape=jax.ShapeDtypeStruct((grp.rows, n), BF16),
        scratch_shapes=[pltpu.VMEM((TOKEN_TILE, d), BF16)],
        compiler_params=_params(("arbitrary", "arbitrary")),
        name=f"ret_qkvg_{grp.seq_len}",
    )(*args)


def _log_sigmoid(x):
    return jnp.minimum(x, 0.0) - jnp.log1p(jnp.exp(-jnp.abs(x)))


def _retention_kernel(*refs, L, C, pack, has_s0, emit_state):
    refs = list(refs)
    decay_ref, q_ref, k_ref, v_ref, g_ref, gn_ref = refs[:6]
    refs = refs[6:]
    s0_ref = refs.pop(0) if has_s0 else None
    y_ref = refs.pop(0)
    so_ref = refs.pop(0) if emit_state else None
    mask_scr, dec_scr, cdec_scr, sb_scr = refs

    head = pl.program_id(0)
    dk = q_ref.shape[1]
    dv = v_ref.shape[1]
    nc = L // C

    @pl.when(pl.program_id(1) == 0)
    def _():
        def log_decay(direction, width):
            return _log_sigmoid(jnp.full((1, width), decay_ref[direction, head], F32))

        ri = lax.broadcasted_iota(jnp.int32, (C, C), 0)
        ci = lax.broadcasted_iota(jnp.int32, (C, C), 1)
        diff = (ri - ci).astype(F32)
        lower = diff >= 0
        upper = diff <= 0
        mask_scr[...] = (
            jnp.where(lower, jnp.exp(jnp.where(lower, diff, 0.0) * log_decay(0, C)), 0.0)
            + jnp.where(upper, jnp.exp(jnp.where(upper, -diff, 0.0) * log_decay(1, C)), 0.0))
        rk = lax.broadcasted_iota(jnp.int32, (C, dk), 0).astype(F32)
        dec_scr[0] = jnp.exp((rk + 1.0) * log_decay(0, dk))
        dec_scr[1] = jnp.exp((C - 1.0 - rk) * log_decay(0, dk))
        dec_scr[2] = jnp.exp((C - rk) * log_decay(1, dk))
        dec_scr[3] = jnp.exp(rk * log_decay(1, dk))
        cdec_scr[0:1, :] = jnp.exp(C * log_decay(0, dv))
        cdec_scr[1:2, :] = jnp.exp(C * log_decay(1, dv))

    gn = gn_ref[...]

    def advance(state, dec_row, upd):
        return upd if state is None else state * cdec_scr[dec_row:dec_row + 1, :] + upd

    for s in range(pack):
        base = s * L

        def kv(n, which):
            rows = slice(base + n * C, base + (n + 1) * C)
            kd = (k_ref[rows, :].astype(F32) * dec_scr[which]).astype(BF16)
            return lax.dot_general(kd, v_ref[rows, :], (((0,), (0,)), ((), ())),
                                   preferred_element_type=F32)

        state_b = s0_ref[s, 0, 1, 0] if has_s0 else None
        has_b = [False] * nc
        for n in reversed(range(nc)):
            if state_b is not None:
                sb_scr[n] = state_b.astype(BF16)
                has_b[n] = True
            if n > 0 or emit_state:
                state_b = advance(state_b, 1, kv(n, 3))
        if emit_state:
            so_ref[s, 0, 1, 0] = state_b

        state_f = s0_ref[s, 0, 0, 0] if has_s0 else None
        for n in range(nc):
            rows = slice(base + n * C, base + (n + 1) * C)
            q = q_ref[rows, :]
            scores = lax.dot_general(q, k_ref[rows, :], (((1,), (1,)), ((), ())),
                                     preferred_element_type=F32)
            y = _bdot((scores * mask_scr[...]).astype(BF16), v_ref[rows, :])
            qf = q.astype(F32)
            if state_f is not None:
                y = y + _bdot((qf * dec_scr[0]).astype(BF16), state_f.astype(BF16))
            if has_b[n]:
                y = y + _bdot((qf * dec_scr[2]).astype(BF16), sb_scr[n])
            mu = jnp.mean(y, axis=-1, keepdims=True)
            yc = y - mu
            var = jnp.mean(yc * yc, axis=-1, keepdims=True)
            yn = (yc * lax.rsqrt(var + EPS)) * gn
            gate = g_ref[rows, :].astype(F32)
            y_ref[rows, :] = ((gate * jax.nn.sigmoid(gate)) * yn).astype(BF16)
            if n < nc - 1 or emit_state:
                state_f = advance(state_f, 0, kv(n, 1))
        if emit_state:
            so_ref[s, 0, 0, 0] = state_f


def _retention(grp, qkvg, decay, gn_g, s0, heads, dk, dv, emit_state):
    L = grp.seq_len
    C = min(L, RET_CHUNK)
    pack = max(1, min(grp.n_seq, RET_STEP_ROWS // L))
    assert grp.n_seq % pack == 0
    kq = heads
    kv0 = 2 * heads * dk // dv
    kg0 = kv0 + heads
    rows = pack * L
    in_specs = [
        pl.BlockSpec(memory_space=pltpu.SMEM),
        pl.BlockSpec((rows, dk), lambda h, b: (b, h)),
        pl.BlockSpec((rows, dk), lambda h, b: (b, kq + h)),
        pl.BlockSpec((rows, dv), lambda h, b: (b, kv0 + h)),
        pl.BlockSpec((rows, dv), lambda h, b: (b, kg0 + h)),
        pl.BlockSpec((1, dv), lambda h, b: (0, h)),
    ]
    args = [decay, qkvg, qkvg, qkvg, qkvg, gn_g.reshape(1, heads * dv)]
    state_block = (pack, 1, 2, 1, dk, dv)
    state_index = lambda h, b: (b, 0, 0, h, 0, 0)
    if s0 is not None:
        in_specs.append(pl.BlockSpec(state_block, state_index))
        args.append(s0)
    out_specs = [pl.BlockSpec((rows, dv), lambda h, b: (b, h))]
    out_shape = [jax.ShapeDtypeStruct((grp.rows, heads * dv), BF16)]
    if emit_state:
        out_specs.append(pl.BlockSpec(state_block, state_index))
        out_shape.append(jax.ShapeDtypeStruct((grp.n_seq, 1, 2, heads, dk, dv), F32))
    kern = functools.partial(_retention_kernel, L=L, C=C, pack=pack, has_s0=s0 is not None,
                             emit_state=emit_state)
    return pl.pallas_call(
        kern,
        grid=(heads, grp.n_seq // pack),
        in_specs=in_specs,
        out_specs=out_specs,
        out_shape=out_shape,
        scratch_shapes=[
            pltpu.VMEM((C, C), F32),
            pltpu.VMEM((4, C, dk), F32),
            pltpu.VMEM((8, dv), F32),
            pltpu.VMEM((L // C, dk, dv), BF16),
        ],
        compiler_params=_params(("arbitrary", "arbitrary")),
        name=f"retention_{L}",
    )(*args)


def kernel(x_prompt, x_sample, state_ret, c, c_ctx, w_ada, b_ada, norm_g, final_g, hy_w_in, hy_b_in, hy_conv_w, hy_conv_b, hy_f_w1, hy_f_b1, hy_f_w2, hy_f_b2, hy_f_w3, hy_f_b3, hy_f_freq, hy_f_wout, hy_f_skip, hy_w_out, hy_b_out, ret_w_qkvg, ret_decay, ret_gn_g, ret_w_o, mlp_w1, mlp_w2):
    n_prompt, prompt_len, d = x_prompt.shape
    n_sample, sample_len, _ = x_sample.shape
    heads, dk, dv = state_ret.shape[3:]
    depth = w_ada.shape[0]
    assert depth == 2 and hy_w_in.shape[0] == 1 and ret_w_qkvg.shape[0] == 1
    assert n_sample + 1 <= MOD_ROWS

    cond = jnp.concatenate(
        [c_ctx[None, :], c, jnp.zeros((MOD_ROWS - 1 - n_sample, d), F32)], axis=0)
    mod = _ada_mod(cond, w_ada, b_ada)
    mod3 = mod.reshape(depth * MOD_ROWS, 1, mod.shape[-1])

    w_in, w_out, w_qkvg, w_o = hy_w_in[0], hy_w_out[0], ret_w_qkvg[0], ret_w_o[0]
    w1, w2 = mlp_w1.astype(BF16), mlp_w2.astype(BF16)

    def run(grp, x, s0, latent):
        L = grp.seq_len
        proj = _hyena_in(grp, x, norm_g[0, 0], mod3, 0, w_in, hy_b_in[0], hy_conv_w[0],
                         hy_conv_b[0])
        fwd, inv = _dft_matrices(L)
        spectra = _filter_spectra(L, d, fwd, hy_f_w1[0], hy_f_b1[0], hy_f_w2[0], hy_f_b2[0],
                                  hy_f_w3[0], hy_f_b3[0], hy_f_freq[0], hy_f_wout[0])
        z = _longconv(grp, proj, spectra, hy_f_skip[0], fwd, inv)
        x = _proj_res(grp, z, w_out, hy_b_out[0], x, mod3, 0)
        x = _mlp(grp, x, norm_g[0, 1], mod3, 0, w1, w2)
        qkvg = _qkvg(grp, x, norm_g[1, 0], mod3, 1, w_qkvg, dk, rotary=latent)
        outs = _retention(grp, qkvg, ret_decay[0], ret_gn_g[0], s0, heads, dk, dv,
                          emit_state=s0 is None)
        x = _proj_res(grp, outs[0], w_o, None, x, mod3, 1)
        y = _mlp(grp, x, norm_g[1, 1], mod3, 1, w1, w2, final_g=final_g)
        return y, (outs[1] if s0 is None else None)

    prompt = _Group(n_prompt, prompt_len, d, mod_row0=0, mod_row_step=0)
    sample = _Group(n_sample, sample_len, d, mod_row0=1, mod_row_step=1)
    y_prompt, new_state = run(prompt, x_prompt.reshape(-1, d), None, False)
    y_sample, _ = run(sample, x_sample.reshape(-1, d), state_ret, True)
    return (y_prompt.reshape(x_prompt.shape), y_sample.reshape(x_sample.shape),
            new_state.astype(x_prompt.dtype))
```

```python
import functools
import math

import numpy as np
import jax
import jax.numpy as jnp
from jax import lax
from jax.experimental import pallas as pl
from jax.experimental.pallas import tpu as pltpu

F32 = jnp.float32
BF16 = jnp.bfloat16

EPS = 1e-6
GRID_W = 64
ROPE_BASE = 10000.0
FILTER_BANDS = 16
DECAY_TARGET = 1e-2
MIN_DECAY = math.log(DECAY_TARGET) / 1.5
MAX_DECAY = math.log(DECAY_TARGET) / 0.3

V7X_VMEM_BYTES = 64 * 1024 * 1024
VMEM_LIMIT = V7X_VMEM_BYTES - 8 * 1024 * 1024

TOKEN_TILE = 1024
COL_TILE = 1024
SUB_COLS = 512
RES_COL_TILE = 512
RES_WEIGHT_CACHE_BYTES = 8 * 1024 * 1024
FF_TILE = 1024
MLP_ROWS = 256
CONV_TILE_ELEMS = 256 * 1024
CONV_MIN_LANES = 512
FILT_CH_TILE = 512
RET_CHUNK = 256
RET_STEP_ROWS = 1024
MOD_ROWS = 8


def _params(semantics):
    return pltpu.CompilerParams(dimension_semantics=semantics, vmem_limit_bytes=VMEM_LIMIT)


def _bdot(a, b):
    return jnp.dot(a, b, preferred_element_type=F32)


def _fdot(a, b):
    return jnp.dot(a, b, preferred_element_type=F32, precision=lax.Precision.HIGHEST)


def _modulated_norm(x, g, scale, shift):
    ms = jnp.mean(x * x, axis=-1, keepdims=True)
    return (x * lax.rsqrt(ms + EPS) * g) * (1.0 + scale) + shift


def _normed_dot(fill, x_ref, g_ref, sc_ref, sh_ref, h_scr, w):
    parts = []
    for r in range(x_ref.shape[0] // MLP_ROWS):
        rows = slice(r * MLP_ROWS, (r + 1) * MLP_ROWS)
        if fill:
            h = _modulated_norm(x_ref[rows, :], g_ref[...], sc_ref[0], sh_ref[0]).astype(BF16)
            h_scr[rows, :] = h
        else:
            h = h_scr[rows, :]
        parts.append(_bdot(h, w))
    return jnp.concatenate(parts, axis=0)


class _Group:
    def __init__(self, n_seq, seq_len, d, mod_row0, mod_row_step):
        assert seq_len & (seq_len - 1) == 0
        assert (n_seq * seq_len) % TOKEN_TILE == 0
        assert TOKEN_TILE % seq_len == 0 or (mod_row_step == 0 and seq_len % TOKEN_TILE == 0)
        assert mod_row_step == 0 or seq_len == TOKEN_TILE
        self.n_seq, self.seq_len, self.d = n_seq, seq_len, d
        self.rows = n_seq * seq_len
        self.tiles = self.rows // TOKEN_TILE
        self.mod_row0, self.mod_row_step = mod_row0, mod_row_step

    def mod_spec(self, layer, chunk, width, tiled):
        per = self.d // width
        base = layer * MOD_ROWS + self.mod_row0
        step = self.mod_row_step

        def index(i, j):
            return (base + step * i, 0, chunk * per + (j if tiled else 0))

        return pl.BlockSpec((1, 1, width), index)


def _ada_kernel(cond_ref, w_ref, b_ref, o_ref):
    c = cond_ref[...]
    a = (c * jax.nn.sigmoid(c)).astype(BF16)
    o_ref[0] = _bdot(a, w_ref[0].astype(BF16)) + b_ref[0]


def _ada_mod(cond, w_ada, b_ada):
    depth, d, n = w_ada.shape
    tn = 1024
    return pl.pallas_call(
        _ada_kernel,
        grid=(depth, n // tn),
        in_specs=[
            pl.BlockSpec((MOD_ROWS, d), lambda l, j: (0, 0)),
            pl.BlockSpec((1, d, tn), lambda l, j: (l, 0, j)),
            pl.BlockSpec((1, 1, tn), lambda l, j: (l, 0, j)),
        ],
        out_specs=pl.BlockSpec((1, MOD_ROWS, tn), lambda l, j: (l, 0, j)),
        out_shape=jax.ShapeDtypeStruct((depth, MOD_ROWS, n), F32),
        compiler_params=_params(("arbitrary", "arbitrary")),
        name="ada_mod",
    )(cond, w_ada, b_ada.reshape(depth, 1, n))


def _hyena_in_kernel(x_ref, g_ref, sc_ref, sh_ref, w_ref, b_ref, cw_ref, cb_ref, o_ref, h_scr,
                     *, seq_len):
    def project(fill):
        rows = h_scr.shape[0]
        pos = lax.broadcasted_iota(jnp.int32, (rows, SUB_COLS), 0) & (seq_len - 1)
        first = pos == 0
        last = pos == seq_len - 1
        for s in range(o_ref.shape[1] // SUB_COLS):
            cols = slice(s * SUB_COLS, (s + 1) * SUB_COLS)
            p = _normed_dot(fill and s == 0, x_ref, g_ref, sc_ref, sh_ref, h_scr,
                            w_ref[:, cols].astype(BF16)) + b_ref[:, cols]
            prev = jnp.where(first, 0.0, pltpu.roll(p, 1, 0))
            nxt = jnp.where(last, 0.0, pltpu.roll(p, rows - 1, 0))
            o_ref[:, cols] = (prev * cw_ref[0:1, cols] + p * cw_ref[1:2, cols]
                              + nxt * cw_ref[2:3, cols] + cb_ref[:, cols])

    j = pl.program_id(1)
    pl.when(j == 0)(lambda: project(True))
    pl.when(j > 0)(lambda: project(False))


def _hyena_in(grp, x, norm_g, mod3, layer, w, b, conv_w, conv_b):
    d = grp.d
    n = w.shape[1]
    return pl.pallas_call(
        functools.partial(_hyena_in_kernel, seq_len=grp.seq_len),
        grid=(grp.tiles, n // COL_TILE),
        in_specs=[
            pl.BlockSpec((TOKEN_TILE, d), lambda i, j: (i, 0)),
            pl.BlockSpec((1, d), lambda i, j: (0, 0)),
            grp.mod_spec(layer, 1, d, False),
            grp.mod_spec(layer, 0, d, False),
            pl.BlockSpec((d, COL_TILE), lambda i, j: (0, j)),
            pl.BlockSpec((1, COL_TILE), lambda i, j: (0, j)),
            pl.BlockSpec((3, COL_TILE), lambda i, j: (0, j)),
            pl.BlockSpec((1, COL_TILE), lambda i, j: (0, j)),
        ],
        out_specs=pl.BlockSpec((TOKEN_TILE, COL_TILE), lambda i, j: (i, j)),
        out_shape=jax.ShapeDtypeStruct((grp.rows, n), F32),
        scratch_shapes=[pltpu.VMEM((TOKEN_TILE, d), BF16)],
        compiler_params=_params(("arbitrary", "arbitrary")),
        name=f"hyena_in_{grp.seq_len}",
    )(x, norm_g.reshape(1, d), mod3, mod3, w, b.reshape(1, n), conv_w, conv_b.reshape(1, n))


def _dft_matrices(L):
    f = np.arange(L, dtype=np.int64)[:, None]
    t = np.arange(L, dtype=np.int64)[None, :]
    ang = ((f * t) % (2 * L)).astype(np.float64) * (np.pi / L)
    cos, sin = np.cos(ang), np.sin(ang)
    nyq = np.where(np.arange(L) % 2 == 0, 1.0, -1.0)
    bot = sin.copy()
    bot[0, :] = nyq
    fwd = np.concatenate([cos, bot], axis=0)
    wgt = np.full((L,), 2.0)
    wgt[0] = 1.0
    inv_cos = (cos * wgt[:, None]).T / (2 * L)
    inv_sin = (-2.0 * sin).T / (2 * L)
    inv_sin[:, 0] = nyq / (2 * L)
    inv = np.concatenate([inv_cos, inv_sin], axis=1)
    return jnp.asarray(fwd, F32).astype(BF16), jnp.asarray(inv, F32).astype(BF16)


def _filter_features(L, width):
    pos = np.arange(L, dtype=np.float64)
    t = pos / L
    omega = 2.0 * math.pi * pos / L
    bands = np.linspace(1e-4, FILTER_BANDS - 1, FILTER_BANDS)
    ang = omega[:, None] * bands[None, :]
    feat = np.concatenate([t[:, None], np.cos(ang), -np.sin(ang)], axis=-1)
    feat = np.pad(feat, ((0, 0), (0, width - feat.shape[1])))
    return jnp.asarray(feat, F32), jnp.asarray(t[:, None], F32)


def _filter_kernel(feat_ref, w1_ref, b1_ref, w2_ref, b2_ref, w3_ref, b3_ref, freq_ref,
                   wof_ref, wob_ref, t_ref, delta_ref, fwd_ref,
                   kr_ref, kiz_ref, krn_ref, h_scr, *, L):
    first = jnp.logical_and(pl.program_id(0) == 0, pl.program_id(1) == 0)

    @pl.when(first)
    def _():
        h = jnp.sin(freq_ref[0:1, :] * (_fdot(feat_ref[...], w1_ref[...]) + b1_ref[...]))
        h = jnp.sin(freq_ref[1:2, :] * (_fdot(h, w2_ref[...]) + b2_ref[...]))
        h = jnp.sin(freq_ref[2:3, :] * (_fdot(h, w3_ref[...]) + b3_ref[...]))
        h_scr[...] = h

    h = h_scr[...].astype(BF16)
    window = jnp.exp(-t_ref[...] * delta_ref[...])
    hf = _bdot(h, wof_ref[...].astype(BF16)) * window
    hb = _bdot(h, wob_ref[...].astype(BF16)) * window
    row = lax.broadcasted_iota(jnp.int32, hf.shape, 0)
    hb = jnp.where(row == 0, 0.0, hb)
    tap_sum = hf + hb
    tap_dif = hb - hf
    kr = _bdot(fwd_ref[0:L, :], tap_sum.astype(BF16))
    ki = _bdot(fwd_ref[L:2 * L, :], tap_dif.astype(BF16))
    sign = jnp.where((row & 1) == 0, 1.0, -1.0)
    k_nyq = jnp.sum(tap_sum * sign, axis=0, keepdims=True)
    kr_ref[0] = kr
    kiz_ref[0] = jnp.where(row == 0, 0.0, ki)
    krn_ref[0] = jnp.where(row == 0, -k_nyq, kr)


def _filter_spectra(L, d, fwd, w1, b1, w2, b2, w3, b3, freq, wout):
    width = w2.shape[0]
    order = wout.shape[1] // (2 * d)
    feat, t = _filter_features(L, width)
    w1p = jnp.pad(w1, ((0, width - w1.shape[0]), (0, 0)))
    deltas = jnp.asarray(np.abs(np.linspace(MIN_DECAY, MAX_DECAY, d))[None, :], F32)
    ct = FILT_CH_TILE
    per = d // ct
    small = lambda shape: pl.BlockSpec(shape, lambda o, j: (0, 0))
    out_spec = pl.BlockSpec((1, L, ct), lambda o, j: (o, 0, j))
    out_shape = jax.ShapeDtypeStruct((order, L, d), F32)
    return pl.pallas_call(
        functools.partial(_filter_kernel, L=L),
        grid=(order, per),
        in_specs=[
            small((L, width)), small((width, width)), small((1, width)),
            small((width, width)), small((1, width)),
            small((width, width)), small((1, width)), small((3, width)),
            pl.BlockSpec((width, ct), lambda o, j: (0, (2 * o) * per + j)),
            pl.BlockSpec((width, ct), lambda o, j: (0, (2 * o + 1) * per + j)),
            small((L, 1)),
            pl.BlockSpec((1, ct), lambda o, j: (0, j)),
            small((2 * L, L)),
        ],
        out_specs=[out_spec, out_spec, out_spec],
        out_shape=[out_shape, out_shape, out_shape],
        scratch_shapes=[pltpu.VMEM((L, width), F32)],
        compiler_params=_params(("arbitrary", "arbitrary")),
        name=f"hyena_filter_{L}",
    )(feat, w1p, b1.reshape(1, width), w2, b2.reshape(1, width), w3, b3.reshape(1, width),
      freq, wout, wout, t, deltas, fwd)


def _longconv_kernel(v_ref, x1_ref, x2_ref, kr_ref, kiz_ref, krn_ref, skip_ref, fwd_ref, inv_ref,
                     o_ref, *, L, pack):
    def lanes(ref):
        return jnp.concatenate([ref[s * L:(s + 1) * L, :] for s in range(pack)], axis=1)

    def tiled(x):
        return jnp.concatenate([x] * pack, axis=1)

    def conv(u, o):
        spec = _bdot(fwd_ref[...], u.astype(BF16))
        a, b = spec[:L], spec[L:]
        kr, kiz, krn = tiled(kr_ref[o]), tiled(kiz_ref[o]), tiled(krn_ref[o])
        top = (a * kr + b * kiz).astype(BF16)
        bot = (a * kiz - b * krn).astype(BF16)
        return _bdot(inv_ref[:, 0:L], top) + _bdot(inv_ref[:, L:2 * L], bot)

    v = lanes(v_ref)
    z = lanes(x1_ref) * (conv(v, 0) + v * tiled(skip_ref[0:1, :]))
    z = lanes(x2_ref) * (conv(z, 1) + z * tiled(skip_ref[1:2, :]))
    ct = o_ref.shape[1]
    for s in range(pack):
        o_ref[s * L:(s + 1) * L, :] = z[:, s * ct:(s + 1) * ct].astype(BF16)


def _longconv(grp, proj, spectra, skip, fwd, inv):
    L, d = grp.seq_len, grp.d
    ct = min(d, CONV_TILE_ELEMS // L)
    pack = max(1, CONV_MIN_LANES // ct)
    assert grp.n_seq % pack == 0
    per = d // ct
    kr, kiz, krn = spectra
    order = kr.shape[0]
    kspec = pl.BlockSpec((order, L, ct), lambda j, b: (0, 0, j))
    return pl.pallas_call(
        functools.partial(_longconv_kernel, L=L, pack=pack),
        grid=(per, grp.n_seq // pack),
        in_specs=[
            pl.BlockSpec((pack * L, ct), lambda j, b: (b, j)),
            pl.BlockSpec((pack * L, ct), lambda j, b: (b, per + j)),
            pl.BlockSpec((pack * L, ct), lambda j, b: (b, 2 * per + j)),
            kspec, kspec, kspec,
            pl.BlockSpec((order, ct), lambda j, b: (0, j)),
            pl.BlockSpec((2 * L, L), lambda j, b: (0, 0)),
            pl.BlockSpec((L, 2 * L), lambda j, b: (0, 0)),
        ],
        out_specs=pl.BlockSpec((pack * L, ct), lambda j, b: (b, j)),
        out_shape=jax.ShapeDtypeStruct((grp.rows, d), BF16),
        compiler_params=_params(("arbitrary", "arbitrary")),
        name=f"hyena_longconv_{L}",
    )(proj, proj, proj, kr, kiz, krn, skip, fwd, inv)


def _proj_res_kernel(*refs, has_bias, cached):
    refs = list(refs)
    a_ref, w_ref = refs[:2]
    refs = refs[2:]
    b_ref = refs.pop(0) if has_bias else None
    x_ref, gate_ref, o_ref = refs[:3]
    if cached:
        wc_scr = refs[3]
        j = pl.program_id(1)

        @pl.when(pl.program_id(0) == 0)
        def _():
            wc_scr[j] = w_ref[...].astype(BF16)

        w = wc_scr[j]
    else:
        w = w_ref[...].astype(BF16)
    m = _bdot(a_ref[...], w)
    if has_bias:
        m = m + b_ref[...]
    o_ref[...] = x_ref[...] + gate_ref[0] * m


def _proj_res(grp, a, w, b, x, mod3, layer):
    k, n = w.shape
    cached = k * n * 2 <= RES_WEIGHT_CACHE_BYTES
    tn = 2 * RES_COL_TILE if cached else RES_COL_TILE
    nj = n // tn
    if cached:
        w_index = lambda i, j: (0, jnp.where(i == 0, j, nj - 1))
    else:
        w_index = lambda i, j: (0, j)
    in_specs = [
        pl.BlockSpec((TOKEN_TILE, k), lambda i, j: (i, 0)),
        pl.BlockSpec((k, tn), w_index),
    ]
    args = [a, w]
    if b is not None:
        in_specs.append(pl.BlockSpec((1, tn), lambda i, j: (0, j)))
        args.append(b.reshape(1, n))
    in_specs += [
        pl.BlockSpec((TOKEN_TILE, tn), lambda i, j: (i, j)),
        grp.mod_spec(layer, 2, tn, True),
    ]
    args += [x, mod3]
    return pl.pallas_call(
        functools.partial(_proj_res_kernel, has_bias=b is not None, cached=cached),
        grid=(grp.tiles, nj),
        in_specs=in_specs,
        out_specs=pl.BlockSpec((TOKEN_TILE, tn), lambda i, j: (i, j)),
        out_shape=jax.ShapeDtypeStruct((grp.rows, n), F32),
        scratch_shapes=[pltpu.VMEM((nj, k, tn), BF16)] if cached else [],
        compiler_params=_params(("arbitrary", "arbitrary")),
        name=f"proj_res_{k}_{grp.seq_len}",
    )(*args)


def _mlp_kernel(*refs, final_norm):
    if final_norm:
        (x_hbm, g_ref, sc_ref, sh_ref, gate_ref, w1_ref, w2_ref, fg_ref, o_ref,
         h_scr, x_buf, x_sem) = refs
    else:
        x_hbm, g_ref, sc_ref, sh_ref, gate_ref, w1_ref, w2_ref, o_ref, h_scr, x_buf, x_sem = refs
    i = pl.program_id(0)
    f = pl.program_id(1)
    last_f = pl.num_programs(1) - 1
    tile_rows = x_buf.shape[0]

    def x_copy(tile):
        start = pl.multiple_of(tile * tile_rows, tile_rows)
        return pltpu.make_async_copy(x_hbm.at[pl.ds(start, tile_rows), :], x_buf, x_sem)

    @pl.when(jnp.logical_and(i == 0, f == 0))
    def _():
        x_copy(0).start()

    @pl.when(f == 0)
    def _():
        x_copy(i).wait()

    @pl.when(jnp.logical_and(f == 1, i + 1 < pl.num_programs(0)))
    def _():
        x_copy(i + 1).start()

    def step(first, last):
        w1 = w1_ref[0]
        w2 = w2_ref[0]
        for r in range(tile_rows // MLP_ROWS):
            rows = slice(r * MLP_ROWS, (r + 1) * MLP_ROWS)
            if first:
                x = x_buf[rows, :]
                h = _modulated_norm(x, g_ref[...], sc_ref[0], sh_ref[0]).astype(BF16)
                h_scr[rows, :] = h
            else:
                h = h_scr[rows, :]
            u = jnp.square(jnp.maximum(_bdot(h, w1), 0.0)).astype(BF16)
            acc = (x if first else o_ref[rows, :]) + gate_ref[0] * _bdot(u, w2)
            if last and final_norm:
                ms = jnp.mean(acc * acc, axis=-1, keepdims=True)
                acc = acc * lax.rsqrt(ms + EPS) * fg_ref[...]
            o_ref[rows, :] = acc

    pl.when(f == 0)(lambda: step(True, False))
    pl.when(jnp.logical_and(f > 0, f < last_f))(lambda: step(False, False))
    pl.when(f == last_f)(lambda: step(False, True))


def _mlp(grp, x, norm_g, mod3, layer, w1, w2, final_g=None):
    d = grp.d
    ff = w1.shape[2]
    nf = ff // FF_TILE
    assert nf >= 2
    in_specs = [
        pl.BlockSpec(memory_space=pl.ANY),
        pl.BlockSpec((1, d), lambda i, f: (0, 0)),
        grp.mod_spec(layer, 4, d, False),
        grp.mod_spec(layer, 3, d, False),
        grp.mod_spec(layer, 5, d, False),
        pl.BlockSpec((1, d, FF_TILE), lambda i, f: (layer, 0, f)),
        pl.BlockSpec((1, FF_TILE, d), lambda i, f: (layer, f, 0)),
    ]
    args = [x, norm_g.reshape(1, d), mod3, mod3, mod3, w1, w2]
    if final_g is not None:
        in_specs.append(pl.BlockSpec((1, d), lambda i, f: (0, 0)))
        args.append(final_g.reshape(1, d))
    return pl.pallas_call(
        functools.partial(_mlp_kernel, final_norm=final_g is not None),
        grid=(grp.tiles, nf),
        in_specs=in_specs,
        out_specs=pl.BlockSpec((TOKEN_TILE, d), lambda i, f: (i, 0)),
        out_shape=jax.ShapeDtypeStruct((grp.rows, d), F32),
        scratch_shapes=[pltpu.VMEM((TOKEN_TILE, d), BF16),
                        pltpu.VMEM((TOKEN_TILE, d), F32),
                        pltpu.SemaphoreType.DMA(())],
        compiler_params=_params(("arbitrary", "arbitrary")),
        name=f"mlp_{layer}_{grp.seq_len}",
    )(*args)


def _rotary_tables(L, dk, width):
    n_rows = L // GRID_W
    rows = np.repeat(np.arange(n_rows, dtype=np.float64), GRID_W)
    cols = np.tile(np.arange(GRID_W, dtype=np.float64), n_rows)
    n_pairs = dk // 4
    inv = ROPE_BASE ** (-np.arange(n_pairs, dtype=np.float64) / n_pairs)
    ang = np.concatenate([rows[:, None] * inv, cols[:, None] * inv], axis=-1)
    cos = np.repeat(np.cos(ang), 2, axis=-1)
    sin = np.repeat(np.sin(ang), 2, axis=-1)
    sin[:, 0::2] *= -1.0
    reps = width // dk
    return (jnp.asarray(np.tile(cos, (1, reps)), F32), jnp.asarray(np.tile(sin, (1, reps)), F32))


def _qkvg_kernel(*refs, rotary, qk_tiles, k_tile0, k_scale):
    if rotary:
        x_ref, g_ref, sc_ref, sh_ref, w_ref, cos_ref, sin_ref, o_ref, h_scr = refs
    else:
        x_ref, g_ref, sc_ref, sh_ref, w_ref, o_ref, h_scr = refs
    j = pl.program_id(1)
    is_qk = j < qk_tiles
    scale = jnp.where(jnp.logical_and(j >= k_tile0, is_qk), k_scale, 1.0)

    def project(fill, rotate):
        if rotate:
            lane = lax.broadcasted_iota(jnp.int32, (h_scr.shape[0], SUB_COLS), 1)
            even = (lane & 1) == 0
        for s in range(o_ref.shape[1] // SUB_COLS):
            cols = slice(s * SUB_COLS, (s + 1) * SUB_COLS)
            acc = _normed_dot(fill and s == 0, x_ref, g_ref, sc_ref, sh_ref, h_scr,
                              w_ref[:, cols].astype(BF16))
            if rotate:
                partner = jnp.where(even, pltpu.roll(acc, SUB_COLS - 1, 1), pltpu.roll(acc, 1, 1))
                acc = acc * cos_ref[...] + partner * sin_ref[...]
            o_ref[:, cols] = (acc * scale).astype(BF16)

    pl.when(j == 0)(lambda: project(True, rotary))
    if rotary:
        pl.when(jnp.logical_and(j > 0, is_qk))(lambda: project(False, True))
        pl.when(jnp.logical_not(is_qk))(lambda: project(False, False))
    else:
        pl.when(j > 0)(lambda: project(False, False))


def _qkvg(grp, x, norm_g, mod3, layer, w, dk, rotary):
    d = grp.d
    n = w.shape[1]
    in_specs = [
        pl.BlockSpec((TOKEN_TILE, d), lambda i, j: (i, 0)),
        pl.BlockSpec((1, d), lambda i, j: (0, 0)),
        grp.mod_spec(layer, 1, d, False),
        grp.mod_spec(layer, 0, d, False),
        pl.BlockSpec((d, COL_TILE), lambda i, j: (0, j)),
    ]
    args = [x, norm_g.reshape(1, d), mod3, mod3, w]
    if rotary:
        assert grp.seq_len == TOKEN_TILE
        cos, sin = _rotary_tables(grp.seq_len, dk, SUB_COLS)
        table = pl.BlockSpec((TOKEN_TILE, SUB_COLS), lambda i, j: (0, 0))
        in_specs += [table, table]
        args += [cos, sin]
    kern = functools.partial(_qkvg_kernel, rotary=rotary, qk_tiles=2 * d // COL_TILE,
                             k_tile0=d // COL_TILE, k_scale=dk ** -0.5)
    return pl.pallas_call(
        kern,
        grid=(grp.tiles, n // COL_TILE),
        in_specs=in_specs,
        out_specs=pl.BlockSpec((TOKEN_TILE, COL_TILE), lambda i, j: (i, j)),
        out_shape=jax.ShapeDtypeStruct((grp.rows, n), BF16),
        scratch_shapes=[pltpu.VMEM((TOKEN_TILE, d), BF16)],
        compiler_params=_params(("arbitrary", "arbitrary")),
        name=f"ret_qkvg_{grp.seq_len}",
    )(*args)


def _log_sigmoid(x):
    return jnp.minimum(x, 0.0) - jnp.log1p(jnp.exp(-jnp.abs(x)))


def _retention_kernel(*refs, L, C, pack, has_s0, emit_state):
    refs = list(refs)
    decay_ref, q_ref, k_ref, v_ref, g_ref, gn_ref = refs[:6]
    refs = refs[6:]
    s0_ref = refs.pop(0) if has_s0 else None
    y_ref = refs.pop(0)
    so_ref = refs.pop(0) if emit_state else None
    mask_scr, dec_scr, cdec_scr, sb_scr = refs

    head = pl.program_id(0)
    dk = q_ref.shape[1]
    dv = v_ref.shape[1]
    nc = L // C

    @pl.when(pl.program_id(1) == 0)
    def _():
        def log_decay(direction, width):
            return _log_sigmoid(jnp.full((1, width), decay_ref[direction, head], F32))

        ri = lax.broadcasted_iota(jnp.int32, (C, C), 0)
        ci = lax.broadcasted_iota(jnp.int32, (C, C), 1)
        diff = (ri - ci).astype(F32)
        lower = diff >= 0
        upper = diff <= 0
        mask_scr[...] = (
            jnp.where(lower, jnp.exp(jnp.where(lower, diff, 0.0) * log_decay(0, C)), 0.0)
            + jnp.where(upper, jnp.exp(jnp.where(upper, -diff, 0.0) * log_decay(1, C)), 0.0))
        rk = lax.broadcasted_iota(jnp.int32, (C, dk), 0).astype(F32)
        dec_scr[0] = jnp.exp((rk + 1.0) * log_decay(0, dk))
        dec_scr[1] = jnp.exp((C - 1.0 - rk) * log_decay(0, dk))
        dec_scr[2] = jnp.exp((C - rk) * log_decay(1, dk))
        dec_scr[3] = jnp.exp(rk * log_decay(1, dk))
        cdec_scr[0:1, :] = jnp.exp(C * log_decay(0, dv))
        cdec_scr[1:2, :] = jnp.exp(C * log_decay(1, dv))

    gn = gn_ref[...]

    def advance(state, dec_row, upd):
        return upd if state is None else state * cdec_scr[dec_row:dec_row + 1, :] + upd

    for s in range(pack):
        base = s * L

        def kv(n, which):
            rows = slice(base + n * C, base + (n + 1) * C)
            kd = (k_ref[rows, :].astype(F32) * dec_scr[which]).astype(BF16)
            return lax.dot_general(kd, v_ref[rows, :], (((0,), (0,)), ((), ())),
                                   preferred_element_type=F32)

        state_b = s0_ref[s, 0, 1, 0] if has_s0 else None
        has_b = [False] * nc
        for n in reversed(range(nc)):
            if state_b is not None:
                sb_scr[n] = state_b.astype(BF16)
                has_b[n] = True
            if n > 0 or emit_state:
                state_b = advance(state_b, 1, kv(n, 3))
        if emit_state:
            so_ref[s, 0, 1, 0] = state_b

        state_f = s0_ref[s, 0, 0, 0] if has_s0 else None
        for n in range(nc):
            rows = slice(base + n * C, base + (n + 1) * C)
            q = q_ref[rows, :]
            scores = lax.dot_general(q, k_ref[rows, :], (((1,), (1,)), ((), ())),
                                     preferred_element_type=F32)
            y = _bdot((scores * mask_scr[...]).astype(BF16), v_ref[rows, :])
            qf = q.astype(F32)
            if state_f is not None:
                y = y + _bdot((qf * dec_scr[0]).astype(BF16), state_f.astype(BF16))
            if has_b[n]:
                y = y + _bdot((qf * dec_scr[2]).astype(BF16), sb_scr[n])
            mu = jnp.mean(y, axis=-1, keepdims=True)
            yc = y - mu
            var = jnp.mean(yc * yc, axis=-1, keepdims=True)
            yn = (yc * lax.rsqrt(var + EPS)) * gn
            gate = g_ref[rows, :].astype(F32)
            y_ref[rows, :] = ((gate * jax.nn.sigmoid(gate)) * yn).astype(BF16)
            if n < nc - 1 or emit_state:
                state_f = advance(state_f, 0, kv(n, 1))
        if emit_state:
            so_ref[s, 0, 0, 0] = state_f


def _retention(grp, qkvg, decay, gn_g, s0, heads, dk, dv, emit_state):
    L = grp.seq_len
    C = min(L, RET_CHUNK)
    pack = max(1, min(grp.n_seq, RET_STEP_ROWS // L))
    assert grp.n_seq % pack == 0
    kq = heads
    kv0 = 2 * heads * dk // dv
    kg0 = kv0 + heads
    rows = pack * L
    in_specs = [
        pl.BlockSpec(memory_space=pltpu.SMEM),
        pl.BlockSpec((rows, dk), lambda h, b: (b, h)),
        pl.BlockSpec((rows, dk), lambda h, b: (b, kq + h)),
        pl.BlockSpec((rows, dv), lambda h, b: (b, kv0 + h)),
        pl.BlockSpec((rows, dv), lambda h, b: (b, kg0 + h)),
        pl.BlockSpec((1, dv), lambda h, b: (0, h)),
    ]
    args = [decay, qkvg, qkvg, qkvg, qkvg, gn_g.reshape(1, heads * dv)]
    state_block = (pack, 1, 2, 1, dk, dv)
    state_index = lambda h, b: (b, 0, 0, h, 0, 0)
    if s0 is not None:
        in_specs.append(pl.BlockSpec(state_block, state_index))
        args.append(s0)
    out_specs = [pl.BlockSpec((rows, dv), lambda h, b: (b, h))]
    out_shape = [jax.ShapeDtypeStruct((grp.rows, heads * dv), BF16)]
    if emit_state:
        out_specs.append(pl.BlockSpec(state_block, state_index))
        out_shape.append(jax.ShapeDtypeStruct((grp.n_seq, 1, 2, heads, dk, dv), F32))
    kern = functools.partial(_retention_kernel, L=L, C=C, pack=pack, has_s0=s0 is not None,
                             emit_state=emit_state)
    return pl.pallas_call(
        kern,
        grid=(heads, grp.n_seq // pack),
        in_specs=in_specs,
        out_specs=out_specs,
        out_shape=out_shape,
        scratch_shapes=[
            pltpu.VMEM((C, C), F32),
            pltpu.VMEM((4, C, dk), F32),
            pltpu.VMEM((8, dv), F32),
            pltpu.VMEM((L // C, dk, dv), BF16),
        ],
        compiler_params=_params(("arbitrary", "arbitrary")),
        name=f"retention_{L}",
    )(*args)


def kernel(x_prompt, x_sample, state_ret, c, c_ctx, w_ada, b_ada, norm_g, final_g, hy_w_in, hy_b_in, hy_conv_w, hy_conv_b, hy_f_w1, hy_f_b1, hy_f_w2, hy_f_b2, hy_f_w3, hy_f_b3, hy_f_freq, hy_f_wout, hy_f_skip, hy_w_out, hy_b_out, ret_w_qkvg, ret_decay, ret_gn_g, ret_w_o, mlp_w1, mlp_w2):
    n_prompt, prompt_len, d = x_prompt.shape
    n_sample, sample_len, _ = x_sample.shape
    heads, dk, dv = state_ret.shape[3:]
    depth = w_ada.shape[0]
    assert depth == 2 and hy_w_in.shape[0] == 1 and ret_w_qkvg.shape[0] == 1
    assert n_sample + 1 <= MOD_ROWS

    cond = jnp.concatenate(
        [c_ctx[None, :], c, jnp.zeros((MOD_ROWS - 1 - n_sample, d), F32)], axis=0)
    mod = _ada_mod(cond, w_ada, b_ada)
    mod3 = mod.reshape(depth * MOD_ROWS, 1, mod.shape[-1])

    w_in, w_out, w_qkvg, w_o = hy_w_in[0], hy_w_out[0], ret_w_qkvg[0], ret_w_o[0]
    w1, w2 = mlp_w1.astype(BF16), mlp_w2.astype(BF16)

    def run(grp, x, s0, latent):
        L = grp.seq_len
        proj = _hyena_in(grp, x, norm_g[0, 0], mod3, 0, w_in, hy_b_in[0], hy_conv_w[0],
                         hy_conv_b[0])
        fwd, inv = _dft_matrices(L)
        spectra = _filter_spectra(L, d, fwd, hy_f_w1[0], hy_f_b1[0], hy_f_w2[0], hy_f_b2[0],
                                  hy_f_w3[0], hy_f_b3[0], hy_f_freq[0], hy_f_wout[0])
        z = _longconv(grp, proj, spectra, hy_f_skip[0], fwd, inv)
        x = _proj_res(grp, z, w_out, hy_b_out[0], x, mod3, 0)
        x = _mlp(grp, x, norm_g[0, 1], mod3, 0, w1, w2)
        qkvg = _qkvg(grp, x, norm_g[1, 0], mod3, 1, w_qkvg, dk, rotary=latent)
        outs = _retention(grp, qkvg, ret_decay[0], ret_gn_g[0], s0, heads, dk, dv,
                          emit_state=s0 is None)
        x = _proj_res(grp, outs[0], w_o, None, x, mod3, 1)
        y = _mlp(grp, x, norm_g[1, 1], mod3, 1, w1, w2, final_g=final_g)
        return y, (outs[1] if s0 is None else None)

    prompt = _Group(n_prompt, prompt_len, d, mod_row0=0, mod_row_step=0)
    sample = _Group(n_sample, sample_len, d, mod_row0=1, mod_row_step=1)
    y_prompt, new_state = run(prompt, x_prompt.reshape(-1, d), None, False)
    y_sample, _ = run(sample, x_sample.reshape(-1, d), state_ret, True)
    return (y_prompt.reshape(x_prompt.shape), y_sample.reshape(x_sample.shape),
            new_state.astype(x_prompt.dtype))
```

```python
import functools
import math

import numpy as np
import jax
import jax.numpy as jnp
from jax import lax
from jax.experimental import pallas as pl
from jax.experimental.pallas import tpu as pltpu

F32 = jnp.float32
BF16 = jnp.bfloat16

EPS = 1e-6
GRID_W = 64
ROPE_BASE = 10000.0
FILTER_BANDS = 16
DECAY_TARGET = 1e-2
MIN_DECAY = math.log(DECAY_TARGET) / 1.5
MAX_DECAY = math.log(DECAY_TARGET) / 0.3

V7X_VMEM_BYTES = 64 * 1024 * 1024
VMEM_LIMIT = V7X_VMEM_BYTES - 8 * 1024 * 1024

TOKEN_TILE = 1024
COL_TILE = 1024
SUB_COLS = 512
RES_COL_TILE = 1024
RES_WEIGHT_CACHE_BYTES = 8 * 1024 * 1024
FF_TILE = 1024
MLP_ROWS = 256
CONV_TILE_ELEMS = 256 * 1024
CONV_MIN_LANES = 512
FILT_CH_TILE = 512
RET_CHUNK = 256
RET_STEP_ROWS = 1024
MOD_ROWS = 8


def _params(semantics):
    return pltpu.CompilerParams(dimension_semantics=semantics, vmem_limit_bytes=VMEM_LIMIT)


def _bdot(a, b):
    return jnp.dot(a, b, preferred_element_type=F32)


def _fdot(a, b):
    return jnp.dot(a, b, preferred_element_type=F32, precision=lax.Precision.HIGHEST)


def _modulated_norm(x, g, scale, shift):
    ms = jnp.mean(x * x, axis=-1, keepdims=True)
    return (x * lax.rsqrt(ms + EPS) * g) * (1.0 + scale) + shift


def _normed_dot(fill, x_ref, g_ref, sc_ref, sh_ref, h_scr, w):
    if not fill:
        return _bdot(h_scr[...], w)
    parts = []
    for r in range(x_ref.shape[0] // MLP_ROWS):
        rows = slice(r * MLP_ROWS, (r + 1) * MLP_ROWS)
        h = _modulated_norm(x_ref[rows, :], g_ref[...], sc_ref[0], sh_ref[0]).astype(BF16)
        h_scr[rows, :] = h
        parts.append(_bdot(h, w))
    return jnp.concatenate(parts, axis=0)


class _Group:
    def __init__(self, n_seq, seq_len, d, mod_row0, mod_row_step):
        assert seq_len & (seq_len - 1) == 0
        assert (n_seq * seq_len) % TOKEN_TILE == 0
        assert TOKEN_TILE % seq_len == 0 or (mod_row_step == 0 and seq_len % TOKEN_TILE == 0)
        assert mod_row_step == 0 or seq_len == TOKEN_TILE
        self.n_seq, self.seq_len, self.d = n_seq, seq_len, d
        self.rows = n_seq * seq_len
        self.tiles = self.rows // TOKEN_TILE
        self.mod_row0, self.mod_row_step = mod_row0, mod_row_step

    def mod_spec(self, layer, chunk, width, tiled):
        per = self.d // width
        base = layer * MOD_ROWS + self.mod_row0
        step = self.mod_row_step

        def index(i, j):
            return (base + step * i, 0, chunk * per + (j if tiled else 0))

        return pl.BlockSpec((1, 1, width), index)


def _ada_kernel(cond_ref, w_ref, b_ref, o_ref):
    c = cond_ref[...]
    a = (c * jax.nn.sigmoid(c)).astype(BF16)
    o_ref[0] = _bdot(a, w_ref[0].astype(BF16)) + b_ref[0]


def _ada_mod(cond, w_ada, b_ada):
    depth, d, n = w_ada.shape
    tn = 1024
    return pl.pallas_call(
        _ada_kernel,
        grid=(depth, n // tn),
        in_specs=[
            pl.BlockSpec((MOD_ROWS, d), lambda l, j: (0, 0)),
            pl.BlockSpec((1, d, tn), lambda l, j: (l, 0, j)),
            pl.BlockSpec((1, 1, tn), lambda l, j: (l, 0, j)),
        ],
        out_specs=pl.BlockSpec((1, MOD_ROWS, tn), lambda l, j: (l, 0, j)),
        out_shape=jax.ShapeDtypeStruct((depth, MOD_ROWS, n), F32),
        compiler_params=_params(("arbitrary", "arbitrary")),
        name="ada_mod",
    )(cond, w_ada, b_ada.reshape(depth, 1, n))


def _hyena_in_kernel(x_ref, g_ref, sc_ref, sh_ref, w_ref, b_ref, cw_ref, cb_ref, o_ref, h_scr,
                     *, seq_len):
    def project(fill):
        rows = h_scr.shape[0]
        pos = lax.broadcasted_iota(jnp.int32, (rows, SUB_COLS), 0) & (seq_len - 1)
        first = pos == 0
        last = pos == seq_len - 1
        for s in range(o_ref.shape[1] // SUB_COLS):
            cols = slice(s * SUB_COLS, (s + 1) * SUB_COLS)
            p = _normed_dot(fill and s == 0, x_ref, g_ref, sc_ref, sh_ref, h_scr,
                            w_ref[:, cols].astype(BF16)) + b_ref[:, cols]
            prev = jnp.where(first, 0.0, pltpu.roll(p, 1, 0))
            nxt = jnp.where(last, 0.0, pltpu.roll(p, rows - 1, 0))
            o_ref[:, cols] = (prev * cw_ref[0:1, cols] + p * cw_ref[1:2, cols]
                              + nxt * cw_ref[2:3, cols] + cb_ref[:, cols])

    j = pl.program_id(1)
    pl.when(j == 0)(lambda: project(True))
    pl.when(j > 0)(lambda: project(False))


def _hyena_in(grp, x, norm_g, mod3, layer, w, b, conv_w, conv_b):
    d = grp.d
    n = w.shape[1]
    return pl.pallas_call(
        functools.partial(_hyena_in_kernel, seq_len=grp.seq_len),
        grid=(grp.tiles, n // COL_TILE),
        in_specs=[
            pl.BlockSpec((TOKEN_TILE, d), lambda i, j: (i, 0)),
            pl.BlockSpec((1, d), lambda i, j: (0, 0)),
            grp.mod_spec(layer, 1, d, False),
            grp.mod_spec(layer, 0, d, False),
            pl.BlockSpec((d, COL_TILE), lambda i, j: (0, j)),
            pl.BlockSpec((1, COL_TILE), lambda i, j: (0, j)),
            pl.BlockSpec((3, COL_TILE), lambda i, j: (0, j)),
            pl.BlockSpec((1, COL_TILE), lambda i, j: (0, j)),
        ],
        out_specs=pl.BlockSpec((TOKEN_TILE, COL_TILE), lambda i, j: (i, j)),
        out_shape=jax.ShapeDtypeStruct((grp.rows, n), F32),
        scratch_shapes=[pltpu.VMEM((TOKEN_TILE, d), BF16)],
        compiler_params=_params(("arbitrary", "arbitrary")),
        name=f"hyena_in_{grp.seq_len}",
    )(x, norm_g.reshape(1, d), mod3, mod3, w, b.reshape(1, n), conv_w, conv_b.reshape(1, n))


def _dft_matrices(L):
    f = np.arange(L, dtype=np.int64)[:, None]
    t = np.arange(L, dtype=np.int64)[None, :]
    ang = ((f * t) % (2 * L)).astype(np.float64) * (np.pi / L)
    cos, sin = np.cos(ang), np.sin(ang)
    nyq = np.where(np.arange(L) % 2 == 0, 1.0, -1.0)
    bot = sin.copy()
    bot[0, :] = nyq
    fwd = np.concatenate([cos, bot], axis=0)
    wgt = np.full((L,), 2.0)
    wgt[0] = 1.0
    inv_cos = (cos * wgt[:, None]).T / (2 * L)
    inv_sin = (-2.0 * sin).T / (2 * L)
    inv_sin[:, 0] = nyq / (2 * L)
    inv = np.concatenate([inv_cos, inv_sin], axis=1)
    return jnp.asarray(fwd, F32).astype(BF16), jnp.asarray(inv, F32).astype(BF16)


def _filter_features(L, width):
    pos = np.arange(L, dtype=np.float64)
    t = pos / L
    omega = 2.0 * math.pi * pos / L
    bands = np.linspace(1e-4, FILTER_BANDS - 1, FILTER_BANDS)
    ang = omega[:, None] * bands[None, :]
    feat = np.concatenate([t[:, None], np.cos(ang), -np.sin(ang)], axis=-1)
    feat = np.pad(feat, ((0, 0), (0, width - feat.shape[1])))
    return jnp.asarray(feat, F32), jnp.asarray(t[:, None], F32)


def _filter_kernel(feat_ref, w1_ref, b1_ref, w2_ref, b2_ref, w3_ref, b3_ref, freq_ref,
                   wof_ref, wob_ref, t_ref, delta_ref, fwd_ref,
                   kr_ref, kiz_ref, krn_ref, h_scr, *, L):
    first = jnp.logical_and(pl.program_id(0) == 0, pl.program_id(1) == 0)

    @pl.when(first)
    def _():
        h = jnp.sin(freq_ref[0:1, :] * (_fdot(feat_ref[...], w1_ref[...]) + b1_ref[...]))
        h = jnp.sin(freq_ref[1:2, :] * (_fdot(h, w2_ref[...]) + b2_ref[...]))
        h = jnp.sin(freq_ref[2:3, :] * (_fdot(h, w3_ref[...]) + b3_ref[...]))
        h_scr[...] = h

    h = h_scr[...].astype(BF16)
    window = jnp.exp(-t_ref[...] * delta_ref[...])
    hf = _bdot(h, wof_ref[...].astype(BF16)) * window
    hb = _bdot(h, wob_ref[...].astype(BF16)) * window
    row = lax.broadcasted_iota(jnp.int32, hf.shape, 0)
    hb = jnp.where(row == 0, 0.0, hb)
    tap_sum = hf + hb
    tap_dif = hb - hf
    kr = _bdot(fwd_ref[0:L, :], tap_sum.astype(BF16))
    ki = _bdot(fwd_ref[L:2 * L, :], tap_dif.astype(BF16))
    sign = jnp.where((row & 1) == 0, 1.0, -1.0)
    k_nyq = jnp.sum(tap_sum * sign, axis=0, keepdims=True)
    kr_ref[0] = kr
    kiz_ref[0] = jnp.where(row == 0, 0.0, ki)
    krn_ref[0] = jnp.where(row == 0, -k_nyq, kr)


def _filter_spectra(L, d, fwd, w1, b1, w2, b2, w3, b3, freq, wout):
    width = w2.shape[0]
    order = wout.shape[1] // (2 * d)
    feat, t = _filter_features(L, width)
    w1p = jnp.pad(w1, ((0, width - w1.shape[0]), (0, 0)))
    deltas = jnp.asarray(np.abs(np.linspace(MIN_DECAY, MAX_DECAY, d))[None, :], F32)
    ct = FILT_CH_TILE
    per = d // ct
    small = lambda shape: pl.BlockSpec(shape, lambda o, j: (0, 0))
    out_spec = pl.BlockSpec((1, L, ct), lambda o, j: (o, 0, j))
    out_shape = jax.ShapeDtypeStruct((order, L, d), F32)
    return pl.pallas_call(
        functools.partial(_filter_kernel, L=L),
        grid=(order, per),
        in_specs=[
            small((L, width)), small((width, width)), small((1, width)),
            small((width, width)), small((1, width)),
            small((width, width)), small((1, width)), small((3, width)),
            pl.BlockSpec((width, ct), lambda o, j: (0, (2 * o) * per + j)),
            pl.BlockSpec((width, ct), lambda o, j: (0, (2 * o + 1) * per + j)),
            small((L, 1)),
            pl.BlockSpec((1, ct), lambda o, j: (0, j)),
            small((2 * L, L)),
        ],
        out_specs=[out_spec, out_spec, out_spec],
        out_shape=[out_shape, out_shape, out_shape],
        scratch_shapes=[pltpu.VMEM((L, width), F32)],
        compiler_params=_params(("arbitrary", "arbitrary")),
        name=f"hyena_filter_{L}",
    )(feat, w1p, b1.reshape(1, width), w2, b2.reshape(1, width), w3, b3.reshape(1, width),
      freq, wout, wout, t, deltas, fwd)


def _longconv_kernel(v_ref, x1_ref, x2_ref, kr_ref, kiz_ref, krn_ref, skip_ref, fwd_ref, inv_ref,
                     o_ref, *, L, pack):
    def lanes(ref):
        return jnp.concatenate([ref[s * L:(s + 1) * L, :] for s in range(pack)], axis=1)

    def tiled(x):
        return jnp.concatenate([x] * pack, axis=1)

    def conv(u, o):
        spec = _bdot(fwd_ref[...], u.astype(BF16))
        a, b = spec[:L], spec[L:]
        kr, kiz, krn = tiled(kr_ref[o]), tiled(kiz_ref[o]), tiled(krn_ref[o])
        top = (a * kr + b * kiz).astype(BF16)
        bot = (a * kiz - b * krn).astype(BF16)
        return _bdot(inv_ref[:, 0:L], top) + _bdot(inv_ref[:, L:2 * L], bot)

    v = lanes(v_ref)
    z = lanes(x1_ref) * (conv(v, 0) + v * tiled(skip_ref[0:1, :]))
    z = lanes(x2_ref) * (conv(z, 1) + z * tiled(skip_ref[1:2, :]))
    ct = o_ref.shape[1]
    for s in range(pack):
        o_ref[s * L:(s + 1) * L, :] = z[:, s * ct:(s + 1) * ct].astype(BF16)


def _longconv(grp, proj, spectra, skip, fwd, inv):
    L, d = grp.seq_len, grp.d
    ct = min(d, CONV_TILE_ELEMS // L)
    pack = max(1, CONV_MIN_LANES // ct)
    assert grp.n_seq % pack == 0
    per = d // ct
    kr, kiz, krn = spectra
    order = kr.shape[0]
    kspec = pl.BlockSpec((order, L, ct), lambda j, b: (0, 0, j))
    return pl.pallas_call(
        functools.partial(_longconv_kernel, L=L, pack=pack),
        grid=(per, grp.n_seq // pack),
        in_specs=[
            pl.BlockSpec((pack * L, ct), lambda j, b: (b, j)),
            pl.BlockSpec((pack * L, ct), lambda j, b: (b, per + j)),
            pl.BlockSpec((pack * L, ct), lambda j, b: (b, 2 * per + j)),
            kspec, kspec, kspec,
            pl.BlockSpec((order, ct), lambda j, b: (0, j)),
            pl.BlockSpec((2 * L, L), lambda j, b: (0, 0)),
            pl.BlockSpec((L, 2 * L), lambda j, b: (0, 0)),
        ],
        out_specs=pl.BlockSpec((pack * L, ct), lambda j, b: (b, j)),
        out_shape=jax.ShapeDtypeStruct((grp.rows, d), BF16),
        compiler_params=_params(("arbitrary", "arbitrary")),
        name=f"hyena_longconv_{L}",
    )(proj, proj, proj, kr, kiz, krn, skip, fwd, inv)


def _proj_res_kernel(*refs, has_bias, cached):
    refs = list(refs)
    a_ref, w_ref = refs[:2]
    refs = refs[2:]
    b_ref = refs.pop(0) if has_bias else None
    x_ref, gate_ref, o_ref = refs[:3]
    if cached:
        wc_scr = refs[3]
        j = pl.program_id(1)

        @pl.when(pl.program_id(0) == 0)
        def _():
            wc_scr[j] = w_ref[...].astype(BF16)

        w = wc_scr[j]
    else:
        w = w_ref[...]
    m = _bdot(a_ref[...], w)
    if has_bias:
        m = m + b_ref[...]
    o_ref[...] = x_ref[...] + gate_ref[0] * m


def _proj_res(grp, a, w, b, x, mod3, layer):
    k, n = w.shape
    cached = w.dtype != BF16
    assert not cached or k * n * 2 <= RES_WEIGHT_CACHE_BYTES
    tn = RES_COL_TILE
    nj = n // tn
    if cached:
        w_index = lambda i, j: (0, jnp.where(i == 0, j, nj - 1))
    else:
        w_index = lambda i, j: (0, j)
    in_specs = [
        pl.BlockSpec((TOKEN_TILE, k), lambda i, j: (i, 0)),
        pl.BlockSpec((k, tn), w_index),
    ]
    args = [a, w]
    if b is not None:
        in_specs.append(pl.BlockSpec((1, tn), lambda i, j: (0, j)))
        args.append(b.reshape(1, n))
    in_specs += [
        pl.BlockSpec((TOKEN_TILE, tn), lambda i, j: (i, j)),
        grp.mod_spec(layer, 2, tn, True),
    ]
    args += [x, mod3]
    return pl.pallas_call(
        functools.partial(_proj_res_kernel, has_bias=b is not None, cached=cached),
        grid=(grp.tiles, nj),
        in_specs=in_specs,
        out_specs=pl.BlockSpec((TOKEN_TILE, tn), lambda i, j: (i, j)),
        out_shape=jax.ShapeDtypeStruct((grp.rows, n), F32),
        scratch_shapes=[pltpu.VMEM((nj, k, tn), BF16)] if cached else [],
        compiler_params=_params(("arbitrary", "arbitrary")),
        name=f"proj_res_{k}_{grp.seq_len}",
    )(*args)


def _mlp_kernel(*refs, final_norm):
    if final_norm:
        (x_hbm, g_ref, sc_ref, sh_ref, gate_ref, w1_ref, w2_ref, fg_ref, o_ref,
         h_scr, x_buf, x_sem) = refs
    else:
        x_hbm, g_ref, sc_ref, sh_ref, gate_ref, w1_ref, w2_ref, o_ref, h_scr, x_buf, x_sem = refs
    i = pl.program_id(0)
    f = pl.program_id(1)
    last_f = pl.num_programs(1) - 1
    tile_rows = x_buf.shape[0]

    def x_copy(tile):
        start = pl.multiple_of(tile * tile_rows, tile_rows)
        return pltpu.make_async_copy(x_hbm.at[pl.ds(start, tile_rows), :], x_buf, x_sem)

    @pl.when(jnp.logical_and(i == 0, f == 0))
    def _():
        x_copy(0).start()

    @pl.when(f == 0)
    def _():
        x_copy(i).wait()

    @pl.when(jnp.logical_and(f == 1, i + 1 < pl.num_programs(0)))
    def _():
        x_copy(i + 1).start()

    def step(first, last):
        w1 = w1_ref[0]
        w2 = w2_ref[0]
        for r in range(tile_rows // MLP_ROWS):
            rows = slice(r * MLP_ROWS, (r + 1) * MLP_ROWS)
            if first:
                x = x_buf[rows, :]
                h = _modulated_norm(x, g_ref[...], sc_ref[0], sh_ref[0]).astype(BF16)
                h_scr[rows, :] = h
            else:
                h = h_scr[rows, :]
            u = jnp.square(jnp.maximum(_bdot(h, w1), 0.0)).astype(BF16)
            acc = (x if first else o_ref[rows, :]) + gate_ref[0] * _bdot(u, w2)
            if last and final_norm:
                ms = jnp.mean(acc * acc, axis=-1, keepdims=True)
                acc = acc * lax.rsqrt(ms + EPS) * fg_ref[...]
            o_ref[rows, :] = acc

    pl.when(f == 0)(lambda: step(True, False))
    pl.when(jnp.logical_and(f > 0, f < last_f))(lambda: step(False, False))
    pl.when(f == last_f)(lambda: step(False, True))


def _mlp(grp, x, norm_g, mod3, layer, w1, w2, final_g=None):
    d = grp.d
    ff = w1.shape[2]
    nf = ff // FF_TILE
    assert nf >= 2
    in_specs = [
        pl.BlockSpec(memory_space=pl.ANY),
        pl.BlockSpec((1, d), lambda i, f: (0, 0)),
        grp.mod_spec(layer, 4, d, False),
        grp.mod_spec(layer, 3, d, False),
        grp.mod_spec(layer, 5, d, False),
        pl.BlockSpec((1, d, FF_TILE), lambda i, f: (layer, 0, f)),
        pl.BlockSpec((1, FF_TILE, d), lambda i, f: (layer, f, 0)),
    ]
    args = [x, norm_g.reshape(1, d), mod3, mod3, mod3, w1, w2]
    if final_g is not None:
        in_specs.append(pl.BlockSpec((1, d), lambda i, f: (0, 0)))
        args.append(final_g.reshape(1, d))
    return pl.pallas_call(
        functools.partial(_mlp_kernel, final_norm=final_g is not None),
        grid=(grp.tiles, nf),
        in_specs=in_specs,
        out_specs=pl.BlockSpec((TOKEN_TILE, d), lambda i, f: (i, 0)),
        out_shape=jax.ShapeDtypeStruct((grp.rows, d), F32),
        scratch_shapes=[pltpu.VMEM((TOKEN_TILE, d), BF16),
                        pltpu.VMEM((TOKEN_TILE, d), F32),
                        pltpu.SemaphoreType.DMA(())],
        compiler_params=_params(("arbitrary", "arbitrary")),
        name=f"mlp_{layer}_{grp.seq_len}",
    )(*args)


def _rotary_tables(L, dk, width):
    n_rows = L // GRID_W
    rows = np.repeat(np.arange(n_rows, dtype=np.float64), GRID_W)
    cols = np.tile(np.arange(GRID_W, dtype=np.float64), n_rows)
    n_pairs = dk // 4
    inv = ROPE_BASE ** (-np.arange(n_pairs, dtype=np.float64) / n_pairs)
    ang = np.concatenate([rows[:, None] * inv, cols[:, None] * inv], axis=-1)
    cos = np.repeat(np.cos(ang), 2, axis=-1)
    sin = np.repeat(np.sin(ang), 2, axis=-1)
    sin[:, 0::2] *= -1.0
    reps = width // dk
    return (jnp.asarray(np.tile(cos, (1, reps)), F32), jnp.asarray(np.tile(sin, (1, reps)), F32))


def _qkvg_kernel(*refs, rotary, qk_tiles, k_tile0, k_scale):
    if rotary:
        x_ref, g_ref, sc_ref, sh_ref, w_ref, cos_ref, sin_ref, o_ref, h_scr = refs
    else:
        x_ref, g_ref, sc_ref, sh_ref, w_ref, o_ref, h_scr = refs
    j = pl.program_id(1)
    is_qk = j < qk_tiles
    scale = jnp.where(jnp.logical_and(j >= k_tile0, is_qk), k_scale, 1.0)

    def project(fill, rotate):
        if rotate:
            lane = lax.broadcasted_iota(jnp.int32, (h_scr.shape[0], SUB_COLS), 1)
            even = (lane & 1) == 0
        for s in range(o_ref.shape[1] // SUB_COLS):
            cols = slice(s * SUB_COLS, (s + 1) * SUB_COLS)
            acc = _normed_dot(fill and s == 0, x_ref, g_ref, sc_ref, sh_ref, h_scr,
                              w_ref[:, cols].astype(BF16))
            if rotate:
                partner = jnp.where(even, pltpu.roll(acc, SUB_COLS - 1, 1), pltpu.roll(acc, 1, 1))
                acc = acc * cos_ref[...] + partner * sin_ref[...]
            o_ref[:, cols] = (acc * scale).astype(BF16)

    pl.when(j == 0)(lambda: project(True, rotary))
    if rotary:
        pl.when(jnp.logical_and(j > 0, is_qk))(lambda: project(False, True))
        pl.when(jnp.logical_not(is_qk))(lambda: project(False, False))
    else:
        pl.when(j > 0)(lambda: project(False, False))


def _qkvg(grp, x, norm_g, mod3, layer, w, dk, rotary):
    d = grp.d
    n = w.shape[1]
    in_specs = [
        pl.BlockSpec((TOKEN_TILE, d), lambda i, j: (i, 0)),
        pl.BlockSpec((1, d), lambda i, j: (0, 0)),
        grp.mod_spec(layer, 1, d, False),
        grp.mod_spec(layer, 0, d, False),
        pl.BlockSpec((d, COL_TILE), lambda i, j: (0, j)),
    ]
    args = [x, norm_g.reshape(1, d), mod3, mod3, w]
    if rotary:
        assert grp.seq_len == TOKEN_TILE
        cos, sin = _rotary_tables(grp.seq_len, dk, SUB_COLS)
        table = pl.BlockSpec((TOKEN_TILE, SUB_COLS), lambda i, j: (0, 0))
        in_specs += [table, table]
        args += [cos, sin]
    kern = functools.partial(_qkvg_kernel, rotary=rotary, qk_tiles=2 * d // COL_TILE,
                             k_tile0=d // COL_TILE, k_scale=dk ** -0.5)
    return pl.pallas_call(
        kern,
        grid=(grp.tiles, n // COL_TILE),
        in_specs=in_specs,
        out_specs=pl.BlockSpec((TOKEN_TILE, COL_TILE), lambda i, j: (i, j)),
        out_shape=jax.ShapeDtypeStruct((grp.rows, n), BF16),
        scratch_shapes=[pltpu.VMEM((TOKEN_TILE, d), BF16)],
        compiler_params=_params(("arbitrary", "arbitrary")),
        name=f"ret_qkvg_{grp.seq_len}",
    )(*args)


def _log_sigmoid(x):
    return jnp.minimum(x, 0.0) - jnp.log1p(jnp.exp(-jnp.abs(x)))


def _retention_kernel(*refs, L, C, pack, has_s0, emit_state):
    refs = list(refs)
    decay_ref, q_ref, k_ref, v_ref, g_ref, gn_ref = refs[:6]
    refs = refs[6:]
    s0_ref = refs.pop(0) if has_s0 else None
    y_ref = refs.pop(0)
    so_ref = refs.pop(0) if emit_state else None
    mask_scr, dec_scr, cdec_scr, sb_scr = refs

    head = pl.program_id(0)
    dk = q_ref.shape[1]
    dv = v_ref.shape[1]
    nc = L // C

    @pl.when(pl.program_id(1) == 0)
    def _():
        def log_decay(direction, width):
            return _log_sigmoid(jnp.full((1, width), decay_ref[direction, head], F32))

        ri = lax.broadcasted_iota(jnp.int32, (C, C), 0)
        ci = lax.broadcasted_iota(jnp.int32, (C, C), 1)
        diff = (ri - ci).astype(F32)
        lower = diff >= 0
        upper = diff <= 0
        mask_scr[...] = (
            jnp.where(lower, jnp.exp(jnp.where(lower, diff, 0.0) * log_decay(0, C)), 0.0)
            + jnp.where(upper, jnp.exp(jnp.where(upper, -diff, 0.0) * log_decay(1, C)), 0.0))
        rk = lax.broadcasted_iota(jnp.int32, (C, dk), 0).astype(F32)
        dec_scr[0] = jnp.exp((rk + 1.0) * log_decay(0, dk))
        dec_scr[1] = jnp.exp((C - 1.0 - rk) * log_decay(0, dk))
        dec_scr[2] = jnp.exp((C - rk) * log_decay(1, dk))
        dec_scr[3] = jnp.exp(rk * log_decay(1, dk))
        cdec_scr[0:1, :] = jnp.exp(C * log_decay(0, dv))
        cdec_scr[1:2, :] = jnp.exp(C * log_decay(1, dv))

    gn = gn_ref[...]

    def advance(state, dec_row, upd):
        return upd if state is None else state * cdec_scr[dec_row:dec_row + 1, :] + upd

    for s in range(pack):
        base = s * L

        def kv(n, which):
            rows = slice(base + n * C, base + (n + 1) * C)
            kd = (k_ref[rows, :].astype(F32) * dec_scr[which]).astype(BF16)
            return lax.dot_general(kd, v_ref[rows, :], (((0,), (0,)), ((), ())),
                                   preferred_element_type=F32)

        state_b = s0_ref[s, 0, 1, 0] if has_s0 else None
        has_b = [False] * nc
        for n in reversed(range(nc)):
            if state_b is not None:
                sb_scr[n] = state_b.astype(BF16)
                has_b[n] = True
            if n > 0 or emit_state:
                state_b = advance(state_b, 1, kv(n, 3))
        if emit_state:
            so_ref[s, 0, 1, 0] = state_b

        state_f = s0_ref[s, 0, 0, 0] if has_s0 else None
        for n in range(nc):
            rows = slice(base + n * C, base + (n + 1) * C)
            q = q_ref[rows, :]
            scores = lax.dot_general(q, k_ref[rows, :], (((1,), (1,)), ((), ())),
                                     preferred_element_type=F32)
            y = _bdot((scores * mask_scr[...]).astype(BF16), v_ref[rows, :])
            qf = q.astype(F32)
            if state_f is not None:
                y = y + _bdot((qf * dec_scr[0]).astype(BF16), state_f.astype(BF16))
            if has_b[n]:
                y = y + _bdot((qf * dec_scr[2]).astype(BF16), sb_scr[n])
            mu = jnp.mean(y, axis=-1, keepdims=True)
            yc = y - mu
            var = jnp.mean(yc * yc, axis=-1, keepdims=True)
            yn = (yc * lax.rsqrt(var + EPS)) * gn
            gate = g_ref[rows, :].astype(F32)
            y_ref[rows, :] = ((gate * jax.nn.sigmoid(gate)) * yn).astype(BF16)
            if n < nc - 1 or emit_state:
                state_f = advance(state_f, 0, kv(n, 1))
        if emit_state:
            so_ref[s, 0, 0, 0] = state_f


def _retention(grp, qkvg, decay, gn_g, s0, heads, dk, dv, emit_state):
    L = grp.seq_len
    C = min(L, RET_CHUNK)
    pack = max(1, min(grp.n_seq, RET_STEP_ROWS // L))
    assert grp.n_seq % pack == 0
    kq = heads
    kv0 = 2 * heads * dk // dv
    kg0 = kv0 + heads
    rows = pack * L
    in_specs = [
        pl.BlockSpec(memory_space=pltpu.SMEM),
        pl.BlockSpec((rows, dk), lambda h, b: (b, h)),
        pl.BlockSpec((rows, dk), lambda h, b: (b, kq + h)),
        pl.BlockSpec((rows, dv), lambda h, b: (b, kv0 + h)),
        pl.BlockSpec((rows, dv), lambda h, b: (b, kg0 + h)),
        pl.BlockSpec((1, dv), lambda h, b: (0, h)),
    ]
    args = [decay, qkvg, qkvg, qkvg, qkvg, gn_g.reshape(1, heads * dv)]
    state_block = (pack, 1, 2, 1, dk, dv)
    state_index = lambda h, b: (b, 0, 0, h, 0, 0)
    if s0 is not None:
        in_specs.append(pl.BlockSpec(state_block, state_index))
        args.append(s0)
    out_specs = [pl.BlockSpec((rows, dv), lambda h, b: (b, h))]
    out_shape = [jax.ShapeDtypeStruct((grp.rows, heads * dv), BF16)]
    if emit_state:
        out_specs.append(pl.BlockSpec(state_block, state_index))
        out_shape.append(jax.ShapeDtypeStruct((grp.n_seq, 1, 2, heads, dk, dv), F32))
    kern = functools.partial(_retention_kernel, L=L, C=C, pack=pack, has_s0=s0 is not None,
                             emit_state=emit_state)
    return pl.pallas_call(
        kern,
        grid=(heads, grp.n_seq // pack),
        in_specs=in_specs,
        out_specs=out_specs,
        out_shape=out_shape,
        scratch_shapes=[
            pltpu.VMEM((C, C), F32),
            pltpu.VMEM((4, C, dk), F32),
            pltpu.VMEM((8, dv), F32),
            pltpu.VMEM((L // C, dk, dv), BF16),
        ],
        compiler_params=_params(("arbitrary", "arbitrary")),
        name=f"retention_{L}",
    )(*args)


def kernel(x_prompt, x_sample, state_ret, c, c_ctx, w_ada, b_ada, norm_g, final_g, hy_w_in, hy_b_in, hy_conv_w, hy_conv_b, hy_f_w1, hy_f_b1, hy_f_w2, hy_f_b2, hy_f_w3, hy_f_b3, hy_f_freq, hy_f_wout, hy_f_skip, hy_w_out, hy_b_out, ret_w_qkvg, ret_decay, ret_gn_g, ret_w_o, mlp_w1, mlp_w2):
    n_prompt, prompt_len, d = x_prompt.shape
    n_sample, sample_len, _ = x_sample.shape
    heads, dk, dv = state_ret.shape[3:]
    depth = w_ada.shape[0]
    assert depth == 2 and hy_w_in.shape[0] == 1 and ret_w_qkvg.shape[0] == 1
    assert n_sample + 1 <= MOD_ROWS

    cond = jnp.concatenate(
        [c_ctx[None, :], c, jnp.zeros((MOD_ROWS - 1 - n_sample, d), F32)], axis=0)
    mod = _ada_mod(cond, w_ada, b_ada)
    mod3 = mod.reshape(depth * MOD_ROWS, 1, mod.shape[-1])

    w_in, w_out, w_qkvg = hy_w_in[0], hy_w_out[0], ret_w_qkvg[0]
    w_o = ret_w_o[0].astype(BF16)
    w1, w2 = mlp_w1.astype(BF16), mlp_w2.astype(BF16)

    def run(grp, x, s0, latent):
        L = grp.seq_len
        proj = _hyena_in(grp, x, norm_g[0, 0], mod3, 0, w_in, hy_b_in[0], hy_conv_w[0],
                         hy_conv_b[0])
        fwd, inv = _dft_matrices(L)
        spectra = _filter_spectra(L, d, fwd, hy_f_w1[0], hy_f_b1[0], hy_f_w2[0], hy_f_b2[0],
                                  hy_f_w3[0], hy_f_b3[0], hy_f_freq[0], hy_f_wout[0])
        z = _longconv(grp, proj, spectra, hy_f_skip[0], fwd, inv)
        x = _proj_res(grp, z, w_out, hy_b_out[0], x, mod3, 0)
        x = _mlp(grp, x, norm_g[0, 1], mod3, 0, w1, w2)
        qkvg = _qkvg(grp, x, norm_g[1, 0], mod3, 1, w_qkvg, dk, rotary=latent)
        outs = _retention(grp, qkvg, ret_decay[0], ret_gn_g[0], s0, heads, dk, dv,
                          emit_state=s0 is None)
        x = _proj_res(grp, outs[0], w_o, None, x, mod3, 1)
        y = _mlp(grp, x, norm_g[1, 1], mod3, 1, w1, w2, final_g=final_g)
        return y, (outs[1] if s0 is None else None)

    prompt = _Group(n_prompt, prompt_len, d, mod_row0=0, mod_row_step=0)
    sample = _Group(n_sample, sample_len, d, mod_row0=1, mod_row_step=1)
    y_prompt, new_state = run(prompt, x_prompt.reshape(-1, d), None, False)
    y_sample, _ = run(sample, x_sample.reshape(-1, d), state_ret, True)
    return (y_prompt.reshape(x_prompt.shape), y_sample.reshape(x_sample.shape),
            new_state.astype(x_prompt.dtype))
```

```python
import functools
import math

import numpy as np
import jax
import jax.numpy as jnp
from jax import lax
from jax.experimental import pallas as pl
from jax.experimental.pallas import tpu as pltpu

F32 = jnp.float32
BF16 = jnp.bfloat16

EPS = 1e-6
GRID_W = 64
ROPE_BASE = 10000.0
FILTER_BANDS = 16
DECAY_TARGET = 1e-2
MIN_DECAY = math.log(DECAY_TARGET) / 1.5
MAX_DECAY = math.log(DECAY_TARGET) / 0.3

V7X_VMEM_BYTES = 64 * 1024 * 1024
VMEM_LIMIT = V7X_VMEM_BYTES - 8 * 1024 * 1024

TOKEN_TILE = 1024
COL_TILE = 1024
SUB_COLS = 512
RES_COL_TILE = 1024
RES_WEIGHT_CACHE_BYTES = 8 * 1024 * 1024
FF_TILE = 1024
MLP_ROWS = 256
CONV_TILE_ELEMS = 256 * 1024
CONV_MIN_LANES = 512
FILT_CH_TILE = 512
RET_CHUNK = 256
RET_STEP_ROWS = 2048
MOD_ROWS = 8


def _params(semantics):
    return pltpu.CompilerParams(dimension_semantics=semantics, vmem_limit_bytes=VMEM_LIMIT)


def _bdot(a, b):
    return jnp.dot(a, b, preferred_element_type=F32)


def _fdot(a, b):
    return jnp.dot(a, b, preferred_element_type=F32, precision=lax.Precision.HIGHEST)


def _modulated_norm(x, g, scale, shift):
    ms = jnp.mean(x * x, axis=-1, keepdims=True)
    return (x * lax.rsqrt(ms + EPS) * g) * (1.0 + scale) + shift


def _normed_dot(fill, x_ref, g_ref, sc_ref, sh_ref, h_scr, w):
    if not fill:
        return _bdot(h_scr[...], w)
    parts = []
    for r in range(x_ref.shape[0] // MLP_ROWS):
        rows = slice(r * MLP_ROWS, (r + 1) * MLP_ROWS)
        h = _modulated_norm(x_ref[rows, :], g_ref[...], sc_ref[0], sh_ref[0]).astype(BF16)
        h_scr[rows, :] = h
        parts.append(_bdot(h, w))
    return jnp.concatenate(parts, axis=0)


class _Group:
    def __init__(self, n_seq, seq_len, d, mod_row0, mod_row_step):
        assert seq_len & (seq_len - 1) == 0
        assert (n_seq * seq_len) % TOKEN_TILE == 0
        assert TOKEN_TILE % seq_len == 0 or (mod_row_step == 0 and seq_len % TOKEN_TILE == 0)
        assert mod_row_step == 0 or seq_len == TOKEN_TILE
        self.n_seq, self.seq_len, self.d = n_seq, seq_len, d
        self.rows = n_seq * seq_len
        self.tiles = self.rows // TOKEN_TILE
        self.mod_row0, self.mod_row_step = mod_row0, mod_row_step

    def mod_spec(self, layer, chunk, width, tiled):
        per = self.d // width
        base = layer * MOD_ROWS + self.mod_row0
        step = self.mod_row_step

        def index(i, j):
            return (base + step * i, 0, chunk * per + (j if tiled else 0))

        return pl.BlockSpec((1, 1, width), index)


def _ada_kernel(cond_ref, w_ref, b_ref, o_ref):
    c = cond_ref[...]
    a = (c * jax.nn.sigmoid(c)).astype(BF16)
    o_ref[0] = _bdot(a, w_ref[0].astype(BF16)) + b_ref[0]


def _ada_mod(cond, w_ada, b_ada):
    depth, d, n = w_ada.shape
    tn = 1024
    return pl.pallas_call(
        _ada_kernel,
        grid=(depth, n // tn),
        in_specs=[
            pl.BlockSpec((MOD_ROWS, d), lambda l, j: (0, 0)),
            pl.BlockSpec((1, d, tn), lambda l, j: (l, 0, j)),
            pl.BlockSpec((1, 1, tn), lambda l, j: (l, 0, j)),
        ],
        out_specs=pl.BlockSpec((1, MOD_ROWS, tn), lambda l, j: (l, 0, j)),
        out_shape=jax.ShapeDtypeStruct((depth, MOD_ROWS, n), F32),
        compiler_params=_params(("arbitrary", "arbitrary")),
        name="ada_mod",
    )(cond, w_ada, b_ada.reshape(depth, 1, n))


def _hyena_in_kernel(x_ref, g_ref, sc_ref, sh_ref, w_ref, b_ref, cw_ref, cb_ref, o_ref, h_scr,
                     *, seq_len):
    def project(fill):
        rows = h_scr.shape[0]
        pos = lax.broadcasted_iota(jnp.int32, (rows, SUB_COLS), 0) & (seq_len - 1)
        first = pos == 0
        last = pos == seq_len - 1
        for s in range(o_ref.shape[1] // SUB_COLS):
            cols = slice(s * SUB_COLS, (s + 1) * SUB_COLS)
            p = _normed_dot(fill and s == 0, x_ref, g_ref, sc_ref, sh_ref, h_scr,
                            w_ref[:, cols].astype(BF16)) + b_ref[:, cols]
            prev = jnp.where(first, 0.0, pltpu.roll(p, 1, 0))
            nxt = jnp.where(last, 0.0, pltpu.roll(p, rows - 1, 0))
            o_ref[:, cols] = (prev * cw_ref[0:1, cols] + p * cw_ref[1:2, cols]
                              + nxt * cw_ref[2:3, cols] + cb_ref[:, cols])

    j = pl.program_id(1)
    pl.when(j == 0)(lambda: project(True))
    pl.when(j > 0)(lambda: project(False))


def _hyena_in(grp, x, norm_g, mod3, layer, w, b, conv_w, conv_b):
    d = grp.d
    n = w.shape[1]
    return pl.pallas_call(
        functools.partial(_hyena_in_kernel, seq_len=grp.seq_len),
        grid=(grp.tiles, n // COL_TILE),
        in_specs=[
            pl.BlockSpec((TOKEN_TILE, d), lambda i, j: (i, 0)),
            pl.BlockSpec((1, d), lambda i, j: (0, 0)),
            grp.mod_spec(layer, 1, d, False),
            grp.mod_spec(layer, 0, d, False),
            pl.BlockSpec((d, COL_TILE), lambda i, j: (0, j)),
            pl.BlockSpec((1, COL_TILE), lambda i, j: (0, j)),
            pl.BlockSpec((3, COL_TILE), lambda i, j: (0, j)),
            pl.BlockSpec((1, COL_TILE), lambda i, j: (0, j)),
        ],
        out_specs=pl.BlockSpec((TOKEN_TILE, COL_TILE), lambda i, j: (i, j)),
        out_shape=jax.ShapeDtypeStruct((grp.rows, n), F32),
        scratch_shapes=[pltpu.VMEM((TOKEN_TILE, d), BF16)],
        compiler_params=_params(("arbitrary", "arbitrary")),
        name=f"hyena_in_{grp.seq_len}",
    )(x, norm_g.reshape(1, d), mod3, mod3, w, b.reshape(1, n), conv_w, conv_b.reshape(1, n))


def _dft_matrices(L):
    f = np.arange(L, dtype=np.int64)[:, None]
    t = np.arange(L, dtype=np.int64)[None, :]
    ang = ((f * t) % (2 * L)).astype(np.float64) * (np.pi / L)
    cos, sin = np.cos(ang), np.sin(ang)
    nyq = np.where(np.arange(L) % 2 == 0, 1.0, -1.0)
    bot = sin.copy()
    bot[0, :] = nyq
    fwd = np.concatenate([cos, bot], axis=0)
    wgt = np.full((L,), 2.0)
    wgt[0] = 1.0
    inv_cos = (cos * wgt[:, None]).T / (2 * L)
    inv_sin = (-2.0 * sin).T / (2 * L)
    inv_sin[:, 0] = nyq / (2 * L)
    inv = np.concatenate([inv_cos, inv_sin], axis=1)
    return jnp.asarray(fwd, F32).astype(BF16), jnp.asarray(inv, F32).astype(BF16)


def _filter_features(L, width):
    pos = np.arange(L, dtype=np.float64)
    t = pos / L
    omega = 2.0 * math.pi * pos / L
    bands = np.linspace(1e-4, FILTER_BANDS - 1, FILTER_BANDS)
    ang = omega[:, None] * bands[None, :]
    feat = np.concatenate([t[:, None], np.cos(ang), -np.sin(ang)], axis=-1)
    feat = np.pad(feat, ((0, 0), (0, width - feat.shape[1])))
    return jnp.asarray(feat, F32), jnp.asarray(t[:, None], F32)


def _filter_kernel(feat_ref, w1_ref, b1_ref, w2_ref, b2_ref, w3_ref, b3_ref, freq_ref,
                   wof_ref, wob_ref, t_ref, delta_ref, fwd_ref,
                   kr_ref, kiz_ref, krn_ref, h_scr, *, L):
    first = jnp.logical_and(pl.program_id(0) == 0, pl.program_id(1) == 0)

    @pl.when(first)
    def _():
        h = jnp.sin(freq_ref[0:1, :] * (_fdot(feat_ref[...], w1_ref[...]) + b1_ref[...]))
        h = jnp.sin(freq_ref[1:2, :] * (_fdot(h, w2_ref[...]) + b2_ref[...]))
        h = jnp.sin(freq_ref[2:3, :] * (_fdot(h, w3_ref[...]) + b3_ref[...]))
        h_scr[...] = h

    h = h_scr[...].astype(BF16)
    window = jnp.exp(-t_ref[...] * delta_ref[...])
    hf = _bdot(h, wof_ref[...].astype(BF16)) * window
    hb = _bdot(h, wob_ref[...].astype(BF16)) * window
    row = lax.broadcasted_iota(jnp.int32, hf.shape, 0)
    hb = jnp.where(row == 0, 0.0, hb)
    tap_sum = hf + hb
    tap_dif = hb - hf
    kr = _bdot(fwd_ref[0:L, :], tap_sum.astype(BF16))
    ki = _bdot(fwd_ref[L:2 * L, :], tap_dif.astype(BF16))
    sign = jnp.where((row & 1) == 0, 1.0, -1.0)
    k_nyq = jnp.sum(tap_sum * sign, axis=0, keepdims=True)
    kr_ref[0] = kr
    kiz_ref[0] = jnp.where(row == 0, 0.0, ki)
    krn_ref[0] = jnp.where(row == 0, -k_nyq, kr)


def _filter_spectra(L, d, fwd, w1, b1, w2, b2, w3, b3, freq, wout):
    width = w2.shape[0]
    order = wout.shape[1] // (2 * d)
    feat, t = _filter_features(L, width)
    w1p = jnp.pad(w1, ((0, width - w1.shape[0]), (0, 0)))
    deltas = jnp.asarray(np.abs(np.linspace(MIN_DECAY, MAX_DECAY, d))[None, :], F32)
    ct = FILT_CH_TILE
    per = d // ct
    small = lambda shape: pl.BlockSpec(shape, lambda o, j: (0, 0))
    out_spec = pl.BlockSpec((1, L, ct), lambda o, j: (o, 0, j))
    out_shape = jax.ShapeDtypeStruct((order, L, d), F32)
    return pl.pallas_call(
        functools.partial(_filter_kernel, L=L),
        grid=(order, per),
        in_specs=[
            small((L, width)), small((width, width)), small((1, width)),
            small((width, width)), small((1, width)),
            small((width, width)), small((1, width)), small((3, width)),
            pl.BlockSpec((width, ct), lambda o, j: (0, (2 * o) * per + j)),
            pl.BlockSpec((width, ct), lambda o, j: (0, (2 * o + 1) * per + j)),
            small((L, 1)),
            pl.BlockSpec((1, ct), lambda o, j: (0, j)),
            small((2 * L, L)),
        ],
        out_specs=[out_spec, out_spec, out_spec],
        out_shape=[out_shape, out_shape, out_shape],
        scratch_shapes=[pltpu.VMEM((L, width), F32)],
        compiler_params=_params(("arbitrary", "arbitrary")),
        name=f"hyena_filter_{L}",
    )(feat, w1p, b1.reshape(1, width), w2, b2.reshape(1, width), w3, b3.reshape(1, width),
      freq, wout, wout, t, deltas, fwd)


def _longconv_kernel(v_ref, x1_ref, x2_ref, kr_ref, kiz_ref, krn_ref, skip_ref, fwd_ref, inv_ref,
                     o_ref, *, L, pack):
    def lanes(ref):
        return jnp.concatenate([ref[s * L:(s + 1) * L, :] for s in range(pack)], axis=1)

    def tiled(x):
        return jnp.concatenate([x] * pack, axis=1)

    def conv(u, o):
        spec = _bdot(fwd_ref[...], u.astype(BF16))
        a, b = spec[:L], spec[L:]
        kr, kiz, krn = tiled(kr_ref[o]), tiled(kiz_ref[o]), tiled(krn_ref[o])
        top = (a * kr + b * kiz).astype(BF16)
        bot = (a * kiz - b * krn).astype(BF16)
        return _bdot(inv_ref[:, 0:L], top) + _bdot(inv_ref[:, L:2 * L], bot)

    v = lanes(v_ref)
    z = lanes(x1_ref) * (conv(v, 0) + v * tiled(skip_ref[0:1, :]))
    z = lanes(x2_ref) * (conv(z, 1) + z * tiled(skip_ref[1:2, :]))
    ct = o_ref.shape[1]
    for s in range(pack):
        o_ref[s * L:(s + 1) * L, :] = z[:, s * ct:(s + 1) * ct].astype(BF16)


def _longconv(grp, proj, spectra, skip, fwd, inv):
    L, d = grp.seq_len, grp.d
    ct = min(d, CONV_TILE_ELEMS // L)
    pack = max(1, CONV_MIN_LANES // ct)
    assert grp.n_seq % pack == 0
    per = d // ct
    kr, kiz, krn = spectra
    order = kr.shape[0]
    kspec = pl.BlockSpec((order, L, ct), lambda j, b: (0, 0, j))
    return pl.pallas_call(
        functools.partial(_longconv_kernel, L=L, pack=pack),
        grid=(per, grp.n_seq // pack),
        in_specs=[
            pl.BlockSpec((pack * L, ct), lambda j, b: (b, j)),
            pl.BlockSpec((pack * L, ct), lambda j, b: (b, per + j)),
            pl.BlockSpec((pack * L, ct), lambda j, b: (b, 2 * per + j)),
            kspec, kspec, kspec,
            pl.BlockSpec((order, ct), lambda j, b: (0, j)),
            pl.BlockSpec((2 * L, L), lambda j, b: (0, 0)),
            pl.BlockSpec((L, 2 * L), lambda j, b: (0, 0)),
        ],
        out_specs=pl.BlockSpec((pack * L, ct), lambda j, b: (b, j)),
        out_shape=jax.ShapeDtypeStruct((grp.rows, d), BF16),
        compiler_params=_params(("arbitrary", "arbitrary")),
        name=f"hyena_longconv_{L}",
    )(proj, proj, proj, kr, kiz, krn, skip, fwd, inv)


def _proj_res_kernel(*refs, has_bias, cached):
    refs = list(refs)
    a_ref, w_ref = refs[:2]
    refs = refs[2:]
    b_ref = refs.pop(0) if has_bias else None
    x_ref, gate_ref, o_ref = refs[:3]
    if cached:
        wc_scr = refs[3]
        j = pl.program_id(1)

        @pl.when(pl.program_id(0) == 0)
        def _():
            wc_scr[j] = w_ref[...].astype(BF16)

        w = wc_scr[j]
    else:
        w = w_ref[...]
    m = _bdot(a_ref[...], w)
    if has_bias:
        m = m + b_ref[...]
    o_ref[...] = x_ref[...] + gate_ref[0] * m


def _proj_res(grp, a, w, b, x, mod3, layer):
    k, n = w.shape
    cached = w.dtype != BF16
    assert not cached or k * n * 2 <= RES_WEIGHT_CACHE_BYTES
    tn = RES_COL_TILE
    nj = n // tn
    if cached:
        w_index = lambda i, j: (0, jnp.where(i == 0, j, nj - 1))
    else:
        w_index = lambda i, j: (0, j)
    in_specs = [
        pl.BlockSpec((TOKEN_TILE, k), lambda i, j: (i, 0)),
        pl.BlockSpec((k, tn), w_index),
    ]
    args = [a, w]
    if b is not None:
        in_specs.append(pl.BlockSpec((1, tn), lambda i, j: (0, j)))
        args.append(b.reshape(1, n))
    in_specs += [
        pl.BlockSpec((TOKEN_TILE, tn), lambda i, j: (i, j)),
        grp.mod_spec(layer, 2, tn, True),
    ]
    args += [x, mod3]
    return pl.pallas_call(
        functools.partial(_proj_res_kernel, has_bias=b is not None, cached=cached),
        grid=(grp.tiles, nj),
        in_specs=in_specs,
        out_specs=pl.BlockSpec((TOKEN_TILE, tn), lambda i, j: (i, j)),
        out_shape=jax.ShapeDtypeStruct((grp.rows, n), F32),
        scratch_shapes=[pltpu.VMEM((nj, k, tn), BF16)] if cached else [],
        compiler_params=_params(("arbitrary", "arbitrary")),
        name=f"proj_res_{k}_{grp.seq_len}",
    )(*args)


def _mlp_kernel(*refs, final_norm):
    if final_norm:
        (x_hbm, g_ref, sc_ref, sh_ref, gate_ref, w1_ref, w2_ref, fg_ref, o_ref,
         h_scr, x_buf, x_sem) = refs
    else:
        x_hbm, g_ref, sc_ref, sh_ref, gate_ref, w1_ref, w2_ref, o_ref, h_scr, x_buf, x_sem = refs
    i = pl.program_id(0)
    f = pl.program_id(1)
    last_f = pl.num_programs(1) - 1
    tile_rows = x_buf.shape[0]

    def x_copy(tile):
        start = pl.multiple_of(tile * tile_rows, tile_rows)
        return pltpu.make_async_copy(x_hbm.at[pl.ds(start, tile_rows), :], x_buf, x_sem)

    @pl.when(jnp.logical_and(i == 0, f == 0))
    def _():
        x_copy(0).start()

    @pl.when(f == 0)
    def _():
        x_copy(i).wait()

    @pl.when(jnp.logical_and(f == 1, i + 1 < pl.num_programs(0)))
    def _():
        x_copy(i + 1).start()

    def step(first, last):
        w1 = w1_ref[0]
        w2 = w2_ref[0]
        for r in range(tile_rows // MLP_ROWS):
            rows = slice(r * MLP_ROWS, (r + 1) * MLP_ROWS)
            if first:
                x = x_buf[rows, :]
                h = _modulated_norm(x, g_ref[...], sc_ref[0], sh_ref[0]).astype(BF16)
                h_scr[rows, :] = h
            else:
                h = h_scr[rows, :]
            u = jnp.square(jnp.maximum(_bdot(h, w1), 0.0)).astype(BF16)
            acc = (x if first else o_ref[rows, :]) + gate_ref[0] * _bdot(u, w2)
            if last and final_norm:
                ms = jnp.mean(acc * acc, axis=-1, keepdims=True)
                acc = acc * lax.rsqrt(ms + EPS) * fg_ref[...]
            o_ref[rows, :] = acc

    pl.when(f == 0)(lambda: step(True, False))
    pl.when(jnp.logical_and(f > 0, f < last_f))(lambda: step(False, False))
    pl.when(f == last_f)(lambda: step(False, True))


def _mlp(grp, x, norm_g, mod3, layer, w1, w2, final_g=None):
    d = grp.d
    ff = w1.shape[2]
    nf = ff // FF_TILE
    assert nf >= 2
    in_specs = [
        pl.BlockSpec(memory_space=pl.ANY),
        pl.BlockSpec((1, d), lambda i, f: (0, 0)),
        grp.mod_spec(layer, 4, d, False),
        grp.mod_spec(layer, 3, d, False),
        grp.mod_spec(layer, 5, d, False),
        pl.BlockSpec((1, d, FF_TILE), lambda i, f: (layer, 0, f)),
        pl.BlockSpec((1, FF_TILE, d), lambda i, f: (layer, f, 0)),
    ]
    args = [x, norm_g.reshape(1, d), mod3, mod3, mod3, w1, w2]
    if final_g is not None:
        in_specs.append(pl.BlockSpec((1, d), lambda i, f: (0, 0)))
        args.append(final_g.reshape(1, d))
    return pl.pallas_call(
        functools.partial(_mlp_kernel, final_norm=final_g is not None),
        grid=(grp.tiles, nf),
        in_specs=in_specs,
        out_specs=pl.BlockSpec((TOKEN_TILE, d), lambda i, f: (i, 0)),
        out_shape=jax.ShapeDtypeStruct((grp.rows, d), F32),
        scratch_shapes=[pltpu.VMEM((TOKEN_TILE, d), BF16),
                        pltpu.VMEM((TOKEN_TILE, d), F32),
                        pltpu.SemaphoreType.DMA(())],
        compiler_params=_params(("arbitrary", "arbitrary")),
        name=f"mlp_{layer}_{grp.seq_len}",
    )(*args)


def _rotary_tables(L, dk, width):
    n_rows = L // GRID_W
    rows = np.repeat(np.arange(n_rows, dtype=np.float64), GRID_W)
    cols = np.tile(np.arange(GRID_W, dtype=np.float64), n_rows)
    n_pairs = dk // 4
    inv = ROPE_BASE ** (-np.arange(n_pairs, dtype=np.float64) / n_pairs)
    ang = np.concatenate([rows[:, None] * inv, cols[:, None] * inv], axis=-1)
    cos = np.repeat(np.cos(ang), 2, axis=-1)
    sin = np.repeat(np.sin(ang), 2, axis=-1)
    sin[:, 0::2] *= -1.0
    reps = width // dk
    return (jnp.asarray(np.tile(cos, (1, reps)), F32), jnp.asarray(np.tile(sin, (1, reps)), F32))


def _qkvg_kernel(*refs, rotary, qk_tiles, k_tile0, k_scale):
    if rotary:
        x_ref, g_ref, sc_ref, sh_ref, w_ref, cos_ref, sin_ref, o_ref, h_scr = refs
    else:
        x_ref, g_ref, sc_ref, sh_ref, w_ref, o_ref, h_scr = refs
    j = pl.program_id(1)
    is_qk = j < qk_tiles
    scale = jnp.where(jnp.logical_and(j >= k_tile0, is_qk), k_scale, 1.0)

    def project(fill, rotate):
        if rotate:
            lane = lax.broadcasted_iota(jnp.int32, (h_scr.shape[0], SUB_COLS), 1)
            even = (lane & 1) == 0
        for s in range(o_ref.shape[1] // SUB_COLS):
            cols = slice(s * SUB_COLS, (s + 1) * SUB_COLS)
            acc = _normed_dot(fill and s == 0, x_ref, g_ref, sc_ref, sh_ref, h_scr,
                              w_ref[:, cols].astype(BF16))
            if rotate:
                partner = jnp.where(even, pltpu.roll(acc, SUB_COLS - 1, 1), pltpu.roll(acc, 1, 1))
                acc = acc * cos_ref[...] + partner * sin_ref[...]
            o_ref[:, cols] = (acc * scale).astype(BF16)

    pl.when(j == 0)(lambda: project(True, rotary))
    if rotary:
        pl.when(jnp.logical_and(j > 0, is_qk))(lambda: project(False, True))
        pl.when(jnp.logical_not(is_qk))(lambda: project(False, False))
    else:
        pl.when(j > 0)(lambda: project(False, False))


def _qkvg(grp, x, norm_g, mod3, layer, w, dk, rotary):
    d = grp.d
    n = w.shape[1]
    in_specs = [
        pl.BlockSpec((TOKEN_TILE, d), lambda i, j: (i, 0)),
        pl.BlockSpec((1, d), lambda i, j: (0, 0)),
        grp.mod_spec(layer, 1, d, False),
        grp.mod_spec(layer, 0, d, False),
        pl.BlockSpec((d, COL_TILE), lambda i, j: (0, j)),
    ]
    args = [x, norm_g.reshape(1, d), mod3, mod3, w]
    if rotary:
        assert grp.seq_len == TOKEN_TILE
        cos, sin = _rotary_tables(grp.seq_len, dk, SUB_COLS)
        table = pl.BlockSpec((TOKEN_TILE, SUB_COLS), lambda i, j: (0, 0))
        in_specs += [table, table]
        args += [cos, sin]
    kern = functools.partial(_qkvg_kernel, rotary=rotary, qk_tiles=2 * d // COL_TILE,
                             k_tile0=d // COL_TILE, k_scale=dk ** -0.5)
    return pl.pallas_call(
        kern,
        grid=(grp.tiles, n // COL_TILE),
        in_specs=in_specs,
        out_specs=pl.BlockSpec((TOKEN_TILE, COL_TILE), lambda i, j: (i, j)),
        out_shape=jax.ShapeDtypeStruct((grp.rows, n), BF16),
        scratch_shapes=[pltpu.VMEM((TOKEN_TILE, d), BF16)],
        compiler_params=_params(("arbitrary", "arbitrary")),
        name=f"ret_qkvg_{grp.seq_len}",
    )(*args)


def _log_sigmoid(x):
    return jnp.minimum(x, 0.0) - jnp.log1p(jnp.exp(-jnp.abs(x)))


def _retention_kernel(*refs, L, C, pack, has_s0, emit_state):
    refs = list(refs)
    decay_ref, q_ref, k_ref, v_ref, g_ref, gn_ref = refs[:6]
    refs = refs[6:]
    s0_ref = refs.pop(0) if has_s0 else None
    y_ref = refs.pop(0)
    so_ref = refs.pop(0) if emit_state else None
    mask_scr, dec_scr, cdec_scr, sb_scr = refs

    head = pl.program_id(0)
    dk = q_ref.shape[1]
    dv = v_ref.shape[1]
    nc = L // C

    @pl.when(pl.program_id(1) == 0)
    def _():
        def log_decay(direction, width):
            return _log_sigmoid(jnp.full((1, width), decay_ref[direction, head], F32))

        ri = lax.broadcasted_iota(jnp.int32, (C, C), 0)
        ci = lax.broadcasted_iota(jnp.int32, (C, C), 1)
        diff = (ri - ci).astype(F32)
        lower = diff >= 0
        upper = diff <= 0
        mask_scr[...] = (
            jnp.where(lower, jnp.exp(jnp.where(lower, diff, 0.0) * log_decay(0, C)), 0.0)
            + jnp.where(upper, jnp.exp(jnp.where(upper, -diff, 0.0) * log_decay(1, C)), 0.0))
        rk = lax.broadcasted_iota(jnp.int32, (C, dk), 0).astype(F32)
        dec_scr[0] = jnp.exp((rk + 1.0) * log_decay(0, dk))
        dec_scr[1] = jnp.exp((C - 1.0 - rk) * log_decay(0, dk))
        dec_scr[2] = jnp.exp((C - rk) * log_decay(1, dk))
        dec_scr[3] = jnp.exp(rk * log_decay(1, dk))
        cdec_scr[0:1, :] = jnp.exp(C * log_decay(0, dv))
        cdec_scr[1:2, :] = jnp.exp(C * log_decay(1, dv))

    gn = gn_ref[...]

    def advance(state, dec_row, upd):
        return upd if state is None else state * cdec_scr[dec_row:dec_row + 1, :] + upd

    for s in range(pack):
        base = s * L

        def kv(n, which):
            rows = slice(base + n * C, base + (n + 1) * C)
            kd = (k_ref[rows, :].astype(F32) * dec_scr[which]).astype(BF16)
            return lax.dot_general(kd, v_ref[rows, :], (((0,), (0,)), ((), ())),
                                   preferred_element_type=F32)

        state_b = s0_ref[s, 0, 1, 0] if has_s0 else None
        has_b = [False] * nc
        for n in reversed(range(nc)):
            if state_b is not None:
                sb_scr[n] = state_b.astype(BF16)
                has_b[n] = True
            if n > 0 or emit_state:
                state_b = advance(state_b, 1, kv(n, 3))
        if emit_state:
            so_ref[s, 0, 1, 0] = state_b

        state_f = s0_ref[s, 0, 0, 0] if has_s0 else None
        for n in range(nc):
            rows = slice(base + n * C, base + (n + 1) * C)
            q = q_ref[rows, :]
            scores = lax.dot_general(q, k_ref[rows, :], (((1,), (1,)), ((), ())),
                                     preferred_element_type=F32)
            y = _bdot((scores * mask_scr[...]).astype(BF16), v_ref[rows, :])
            qf = q.astype(F32)
            if state_f is not None:
                y = y + _bdot((qf * dec_scr[0]).astype(BF16), state_f.astype(BF16))
            if has_b[n]:
                y = y + _bdot((qf * dec_scr[2]).astype(BF16), sb_scr[n])
            mu = jnp.mean(y, axis=-1, keepdims=True)
            yc = y - mu
            var = jnp.mean(yc * yc, axis=-1, keepdims=True)
            yn = (yc * lax.rsqrt(var + EPS)) * gn
            gate = g_ref[rows, :].astype(F32)
            y_ref[rows, :] = ((gate * jax.nn.sigmoid(gate)) * yn).astype(BF16)
            if n < nc - 1 or emit_state:
                state_f = advance(state_f, 0, kv(n, 1))
        if emit_state:
            so_ref[s, 0, 0, 0] = state_f


def _retention(grp, qkvg, decay, gn_g, s0, heads, dk, dv, emit_state):
    L = grp.seq_len
    C = min(L, RET_CHUNK)
    pack = max(1, min(grp.n_seq, RET_STEP_ROWS // L))
    assert grp.n_seq % pack == 0
    kq = heads
    kv0 = 2 * heads * dk // dv
    kg0 = kv0 + heads
    rows = pack * L
    in_specs = [
        pl.BlockSpec(memory_space=pltpu.SMEM),
        pl.BlockSpec((rows, dk), lambda h, b: (b, h)),
        pl.BlockSpec((rows, dk), lambda h, b: (b, kq + h)),
        pl.BlockSpec((rows, dv), lambda h, b: (b, kv0 + h)),
        pl.BlockSpec((rows, dv), lambda h, b: (b, kg0 + h)),
        pl.BlockSpec((1, dv), lambda h, b: (0, h)),
    ]
    args = [decay, qkvg, qkvg, qkvg, qkvg, gn_g.reshape(1, heads * dv)]
    state_block = (pack, 1, 2, 1, dk, dv)
    state_index = lambda h, b: (b, 0, 0, h, 0, 0)
    if s0 is not None:
        in_specs.append(pl.BlockSpec(state_block, state_index))
        args.append(s0)
    out_specs = [pl.BlockSpec((rows, dv), lambda h, b: (b, h))]
    out_shape = [jax.ShapeDtypeStruct((grp.rows, heads * dv), BF16)]
    if emit_state:
        out_specs.append(pl.BlockSpec(state_block, state_index))
        out_shape.append(jax.ShapeDtypeStruct((grp.n_seq, 1, 2, heads, dk, dv), F32))
    kern = functools.partial(_retention_kernel, L=L, C=C, pack=pack, has_s0=s0 is not None,
                             emit_state=emit_state)
    return pl.pallas_call(
        kern,
        grid=(heads, grp.n_seq // pack),
        in_specs=in_specs,
        out_specs=out_specs,
        out_shape=out_shape,
        scratch_shapes=[
            pltpu.VMEM((C, C), F32),
            pltpu.VMEM((4, C, dk), F32),
            pltpu.VMEM((8, dv), F32),
            pltpu.VMEM((L // C, dk, dv), BF16),
        ],
        compiler_params=_params(("arbitrary", "arbitrary")),
        name=f"retention_{L}",
    )(*args)


def kernel(x_prompt, x_sample, state_ret, c, c_ctx, w_ada, b_ada, norm_g, final_g, hy_w_in, hy_b_in, hy_conv_w, hy_conv_b, hy_f_w1, hy_f_b1, hy_f_w2, hy_f_b2, hy_f_w3, hy_f_b3, hy_f_freq, hy_f_wout, hy_f_skip, hy_w_out, hy_b_out, ret_w_qkvg, ret_decay, ret_gn_g, ret_w_o, mlp_w1, mlp_w2):
    n_prompt, prompt_len, d = x_prompt.shape
    n_sample, sample_len, _ = x_sample.shape
    heads, dk, dv = state_ret.shape[3:]
    depth = w_ada.shape[0]
    assert depth == 2 and hy_w_in.shape[0] == 1 and ret_w_qkvg.shape[0] == 1
    assert n_sample + 1 <= MOD_ROWS

    cond = jnp.concatenate(
        [c_ctx[None, :], c, jnp.zeros((MOD_ROWS - 1 - n_sample, d), F32)], axis=0)
    mod = _ada_mod(cond, w_ada, b_ada)
    mod3 = mod.reshape(depth * MOD_ROWS, 1, mod.shape[-1])

    w_in, w_out, w_qkvg = hy_w_in[0], hy_w_out[0], ret_w_qkvg[0]
    w_o = ret_w_o[0].astype(BF16)
    w1, w2 = mlp_w1.astype(BF16), mlp_w2.astype(BF16)

    def run(grp, x, s0, latent):
        L = grp.seq_len
        proj = _hyena_in(grp, x, norm_g[0, 0], mod3, 0, w_in, hy_b_in[0], hy_conv_w[0],
                         hy_conv_b[0])
        fwd, inv = _dft_matrices(L)
        spectra = _filter_spectra(L, d, fwd, hy_f_w1[0], hy_f_b1[0], hy_f_w2[0], hy_f_b2[0],
                                  hy_f_w3[0], hy_f_b3[0], hy_f_freq[0], hy_f_wout[0])
        z = _longconv(grp, proj, spectra, hy_f_skip[0], fwd, inv)
        x = _proj_res(grp, z, w_out, hy_b_out[0], x, mod3, 0)
        x = _mlp(grp, x, norm_g[0, 1], mod3, 0, w1, w2)
        qkvg = _qkvg(grp, x, norm_g[1, 0], mod3, 1, w_qkvg, dk, rotary=latent)
        outs = _retention(grp, qkvg, ret_decay[0], ret_gn_g[0], s0, heads, dk, dv,
                          emit_state=s0 is None)
        x = _proj_res(grp, outs[0], w_o, None, x, mod3, 1)
        y = _mlp(grp, x, norm_g[1, 1], mod3, 1, w1, w2, final_g=final_g)
        return y, (outs[1] if s0 is None else None)

    prompt = _Group(n_prompt, prompt_len, d, mod_row0=0, mod_row_step=0)
    sample = _Group(n_sample, sample_len, d, mod_row0=1, mod_row_step=1)
    y_prompt, new_state = run(prompt, x_prompt.reshape(-1, d), None, False)
    y_sample, _ = run(sample, x_sample.reshape(-1, d), state_ret, True)
    return (y_prompt.reshape(x_prompt.shape), y_sample.reshape(x_sample.shape),
            new_state.astype(x_prompt.dtype))
```

```python
import functools
import math

import numpy as np
import jax
import jax.numpy as jnp
from jax import lax
from jax.experimental import pallas as pl
from jax.experimental.pallas import tpu as pltpu

F32 = jnp.float32
BF16 = jnp.bfloat16

EPS = 1e-6
GRID_W = 64
ROPE_BASE = 10000.0
FILTER_BANDS = 16
DECAY_TARGET = 1e-2
MIN_DECAY = math.log(DECAY_TARGET) / 1.5
MAX_DECAY = math.log(DECAY_TARGET) / 0.3

V7X_VMEM_BYTES = 64 * 1024 * 1024
VMEM_LIMIT = V7X_VMEM_BYTES - 8 * 1024 * 1024

TOKEN_TILE = 1024
PROJ_TOKEN_TILE = 2048
COL_TILE = 512
SUB_COLS = 512
RES_COL_TILE = 1024
RES_WEIGHT_CACHE_BYTES = 8 * 1024 * 1024
FF_TILE = 1024
MLP_ROWS = 256
CONV_TILE_ELEMS = 256 * 1024
CONV_MIN_LANES = 512
FILT_CH_TILE = 512
RET_CHUNK = 256
RET_STEP_ROWS = 2048
MOD_ROWS = 8


def _params(semantics):
    return pltpu.CompilerParams(dimension_semantics=semantics, vmem_limit_bytes=VMEM_LIMIT)


def _bdot(a, b):
    return jnp.dot(a, b, preferred_element_type=F32)


def _fdot(a, b):
    return jnp.dot(a, b, preferred_element_type=F32, precision=lax.Precision.HIGHEST)


def _modulated_norm(x, g, scale, shift):
    ms = jnp.mean(x * x, axis=-1, keepdims=True)
    return (x * lax.rsqrt(ms + EPS) * g) * (1.0 + scale) + shift


def _prefetch_tile(x_hbm, x_buf, sem, inner):
    i = pl.program_id(0)
    rows = x_buf.shape[0]

    def copy(tile):
        start = pl.multiple_of(tile * rows, rows)
        return pltpu.make_async_copy(x_hbm.at[pl.ds(start, rows), :], x_buf, sem)

    @pl.when(jnp.logical_and(i == 0, inner == 0))
    def _():
        copy(0).start()

    @pl.when(inner == 0)
    def _():
        copy(i).wait()

    @pl.when(jnp.logical_and(inner == 1, i + 1 < pl.num_programs(0)))
    def _():
        copy(i + 1).start()


def _normed_dot(fill, x_buf, g_ref, sc_refs, sh_refs, h_scr, w):
    if not fill:
        return _bdot(h_scr[...], w)
    parts = []
    rows_per_mod = x_buf.shape[0] // len(sc_refs)
    for r in range(x_buf.shape[0] // MLP_ROWS):
        rows = slice(r * MLP_ROWS, (r + 1) * MLP_ROWS)
        m = r * MLP_ROWS // rows_per_mod
        h = _modulated_norm(x_buf[rows, :], g_ref[...], sc_refs[m][0], sh_refs[m][0]).astype(BF16)
        h_scr[rows, :] = h
        parts.append(_bdot(h, w))
    return jnp.concatenate(parts, axis=0)


class _Group:
    def __init__(self, n_seq, seq_len, d, mod_row0, mod_row_step):
        assert seq_len & (seq_len - 1) == 0
        assert (n_seq * seq_len) % TOKEN_TILE == 0
        assert TOKEN_TILE % seq_len == 0 or (mod_row_step == 0 and seq_len % TOKEN_TILE == 0)
        assert mod_row_step == 0 or seq_len == TOKEN_TILE
        assert (n_seq * seq_len) % PROJ_TOKEN_TILE == 0 and PROJ_TOKEN_TILE % TOKEN_TILE == 0
        self.n_seq, self.seq_len, self.d = n_seq, seq_len, d
        self.rows = n_seq * seq_len
        self.tiles = self.rows // TOKEN_TILE
        self.mod_row0, self.mod_row_step = mod_row0, mod_row_step

    def mod_spec(self, layer, chunk, width, tiled):
        per = self.d // width
        base = layer * MOD_ROWS + self.mod_row0
        step = self.mod_row_step

        def index(i, j):
            return (base + step * i, 0, chunk * per + (j if tiled else 0))

        return pl.BlockSpec((1, 1, width), index)

    def mod_specs(self, layer, chunk, tile_rows):
        n_mod = tile_rows // TOKEN_TILE
        base = layer * MOD_ROWS + self.mod_row0
        step = self.mod_row_step

        def spec(k):
            return pl.BlockSpec((1, 1, self.d),
                                lambda i, j: (base + step * (n_mod * i + k), 0, chunk))

        return [spec(k) for k in range(n_mod)]


def _ada_kernel(cond_ref, w_ref, b_ref, o_ref):
    c = cond_ref[...]
    a = (c * jax.nn.sigmoid(c)).astype(BF16)
    o_ref[0] = _bdot(a, w_ref[0].astype(BF16)) + b_ref[0]


def _ada_mod(cond, w_ada, b_ada):
    depth, d, n = w_ada.shape
    tn = 1024
    return pl.pallas_call(
        _ada_kernel,
        grid=(depth, n // tn),
        in_specs=[
            pl.BlockSpec((MOD_ROWS, d), lambda l, j: (0, 0)),
            pl.BlockSpec((1, d, tn), lambda l, j: (l, 0, j)),
            pl.BlockSpec((1, 1, tn), lambda l, j: (l, 0, j)),
        ],
        out_specs=pl.BlockSpec((1, MOD_ROWS, tn), lambda l, j: (l, 0, j)),
        out_shape=jax.ShapeDtypeStruct((depth, MOD_ROWS, n), F32),
        compiler_params=_params(("arbitrary", "arbitrary")),
        name="ada_mod",
    )(cond, w_ada, b_ada.reshape(depth, 1, n))


def _hyena_in_kernel(*refs, seq_len, n_mod):
    x_hbm, g_ref = refs[:2]
    sc_refs = refs[2:2 + n_mod]
    sh_refs = refs[2 + n_mod:2 + 2 * n_mod]
    w_ref, b_ref, cw_ref, cb_ref, o_ref, h_scr, x_buf, x_sem = refs[2 + 2 * n_mod:]
    j = pl.program_id(1)
    _prefetch_tile(x_hbm, x_buf, x_sem, j)

    def project(fill):
        rows = h_scr.shape[0]
        pos = lax.broadcasted_iota(jnp.int32, (rows, SUB_COLS), 0) & (seq_len - 1)
        first = pos == 0
        last = pos == seq_len - 1
        for s in range(o_ref.shape[1] // SUB_COLS):
            cols = slice(s * SUB_COLS, (s + 1) * SUB_COLS)
            p = _normed_dot(fill and s == 0, x_buf, g_ref, sc_refs, sh_refs, h_scr,
                            w_ref[:, cols].astype(BF16)) + b_ref[:, cols]
            prev = jnp.where(first, 0.0, pltpu.roll(p, 1, 0))
            nxt = jnp.where(last, 0.0, pltpu.roll(p, rows - 1, 0))
            o_ref[:, cols] = (prev * cw_ref[0:1, cols] + p * cw_ref[1:2, cols]
                              + nxt * cw_ref[2:3, cols] + cb_ref[:, cols])

    pl.when(j == 0)(lambda: project(True))
    pl.when(j > 0)(lambda: project(False))


def _proj_scratch(d):
    return [pltpu.VMEM((PROJ_TOKEN_TILE, d), BF16), pltpu.VMEM((PROJ_TOKEN_TILE, d), F32),
            pltpu.SemaphoreType.DMA(())]


def _hyena_in(grp, x, norm_g, mod3, layer, w, b, conv_w, conv_b):
    d = grp.d
    n = w.shape[1]
    assert n // COL_TILE >= 2
    sc_specs = grp.mod_specs(layer, 1, PROJ_TOKEN_TILE)
    sh_specs = grp.mod_specs(layer, 0, PROJ_TOKEN_TILE)
    return pl.pallas_call(
        functools.partial(_hyena_in_kernel, seq_len=grp.seq_len, n_mod=len(sc_specs)),
        grid=(grp.rows // PROJ_TOKEN_TILE, n // COL_TILE),
        in_specs=[
            pl.BlockSpec(memory_space=pl.ANY),
            pl.BlockSpec((1, d), lambda i, j: (0, 0)),
            *sc_specs, *sh_specs,
            pl.BlockSpec((d, COL_TILE), lambda i, j: (0, j)),
            pl.BlockSpec((1, COL_TILE), lambda i, j: (0, j)),
            pl.BlockSpec((3, COL_TILE), lambda i, j: (0, j)),
            pl.BlockSpec((1, COL_TILE), lambda i, j: (0, j)),
        ],
        out_specs=pl.BlockSpec((PROJ_TOKEN_TILE, COL_TILE), lambda i, j: (i, j)),
        out_shape=jax.ShapeDtypeStruct((grp.rows, n), F32),
        scratch_shapes=_proj_scratch(d),
        compiler_params=_params(("arbitrary", "arbitrary")),
        name=f"hyena_in_{grp.seq_len}",
    )(x, norm_g.reshape(1, d), *[mod3] * (2 * len(sc_specs)), w, b.reshape(1, n), conv_w,
      conv_b.reshape(1, n))


def _dft_matrices(L):
    f = np.arange(L, dtype=np.int64)[:, None]
    t = np.arange(L, dtype=np.int64)[None, :]
    ang = ((f * t) % (2 * L)).astype(np.float64) * (np.pi / L)
    cos, sin = np.cos(ang), np.sin(ang)
    nyq = np.where(np.arange(L) % 2 == 0, 1.0, -1.0)
    bot = sin.copy()
    bot[0, :] = nyq
    fwd = np.concatenate([cos, bot], axis=0)
    wgt = np.full((L,), 2.0)
    wgt[0] = 1.0
    inv_cos = (cos * wgt[:, None]).T / (2 * L)
    inv_sin = (-2.0 * sin).T / (2 * L)
    inv_sin[:, 0] = nyq / (2 * L)
    inv = np.concatenate([inv_cos, inv_sin], axis=1)
    return jnp.asarray(fwd, F32).astype(BF16), jnp.asarray(inv, F32).astype(BF16)


def _filter_features(L, width):
    pos = np.arange(L, dtype=np.float64)
    t = pos / L
    omega = 2.0 * math.pi * pos / L
    bands = np.linspace(1e-4, FILTER_BANDS - 1, FILTER_BANDS)
    ang = omega[:, None] * bands[None, :]
    feat = np.concatenate([t[:, None], np.cos(ang), -np.sin(ang)], axis=-1)
    feat = np.pad(feat, ((0, 0), (0, width - feat.shape[1])))
    return jnp.asarray(feat, F32), jnp.asarray(t[:, None], F32)


def _filter_kernel(feat_ref, w1_ref, b1_ref, w2_ref, b2_ref, w3_ref, b3_ref, freq_ref,
                   wof_ref, wob_ref, t_ref, delta_ref, fwd_ref,
                   kr_ref, kiz_ref, krn_ref, h_scr, *, L):
    first = jnp.logical_and(pl.program_id(0) == 0, pl.program_id(1) == 0)

    @pl.when(first)
    def _():
        h = jnp.sin(freq_ref[0:1, :] * (_fdot(feat_ref[...], w1_ref[...]) + b1_ref[...]))
        h = jnp.sin(freq_ref[1:2, :] * (_fdot(h, w2_ref[...]) + b2_ref[...]))
        h = jnp.sin(freq_ref[2:3, :] * (_fdot(h, w3_ref[...]) + b3_ref[...]))
        h_scr[...] = h

    h = h_scr[...].astype(BF16)
    window = jnp.exp(-t_ref[...] * delta_ref[...])
    hf = _bdot(h, wof_ref[...].astype(BF16)) * window
    hb = _bdot(h, wob_ref[...].astype(BF16)) * window
    row = lax.broadcasted_iota(jnp.int32, hf.shape, 0)
    hb = jnp.where(row == 0, 0.0, hb)
    tap_sum = hf + hb
    tap_dif = hb - hf
    kr = _bdot(fwd_ref[0:L, :], tap_sum.astype(BF16))
    ki = _bdot(fwd_ref[L:2 * L, :], tap_dif.astype(BF16))
    sign = jnp.where((row & 1) == 0, 1.0, -1.0)
    k_nyq = jnp.sum(tap_sum * sign, axis=0, keepdims=True)
    kr_ref[0] = kr
    kiz_ref[0] = jnp.where(row == 0, 0.0, ki)
    krn_ref[0] = jnp.where(row == 0, -k_nyq, kr)


def _filter_spectra(L, d, fwd, w1, b1, w2, b2, w3, b3, freq, wout):
    width = w2.shape[0]
    order = wout.shape[1] // (2 * d)
    feat, t = _filter_features(L, width)
    w1p = jnp.pad(w1, ((0, width - w1.shape[0]), (0, 0)))
    deltas = jnp.asarray(np.abs(np.linspace(MIN_DECAY, MAX_DECAY, d))[None, :], F32)
    ct = FILT_CH_TILE
    per = d // ct
    small = lambda shape: pl.BlockSpec(shape, lambda o, j: (0, 0))
    out_spec = pl.BlockSpec((1, L, ct), lambda o, j: (o, 0, j))
    out_shape = jax.ShapeDtypeStruct((order, L, d), F32)
    return pl.pallas_call(
        functools.partial(_filter_kernel, L=L),
        grid=(order, per),
        in_specs=[
            small((L, width)), small((width, width)), small((1, width)),
            small((width, width)), small((1, width)),
            small((width, width)), small((1, width)), small((3, width)),
            pl.BlockSpec((width, ct), lambda o, j: (0, (2 * o) * per + j)),
            pl.BlockSpec((width, ct), lambda o, j: (0, (2 * o + 1) * per + j)),
            small((L, 1)),
            pl.BlockSpec((1, ct), lambda o, j: (0, j)),
            small((2 * L, L)),
        ],
        out_specs=[out_spec, out_spec, out_spec],
        out_shape=[out_shape, out_shape, out_shape],
        scratch_shapes=[pltpu.VMEM((L, width), F32)],
        compiler_params=_params(("arbitrary", "arbitrary")),
        name=f"hyena_filter_{L}",
    )(feat, w1p, b1.reshape(1, width), w2, b2.reshape(1, width), w3, b3.reshape(1, width),
      freq, wout, wout, t, deltas, fwd)


def _longconv_kernel(v_ref, x1_ref, x2_ref, kr_ref, kiz_ref, krn_ref, skip_ref, fwd_ref, inv_ref,
                     o_ref, *, L, pack):
    def lanes(ref):
        return jnp.concatenate([ref[s * L:(s + 1) * L, :] for s in range(pack)], axis=1)

    def tiled(x):
        return jnp.concatenate([x] * pack, axis=1)

    def conv(u, o):
        spec = _bdot(fwd_ref[...], u.astype(BF16))
        a, b = spec[:L], spec[L:]
        kr, kiz, krn = tiled(kr_ref[o]), tiled(kiz_ref[o]), tiled(krn_ref[o])
        top = (a * kr + b * kiz).astype(BF16)
        bot = (a * kiz - b * krn).astype(BF16)
        return _bdot(inv_ref[:, 0:L], top) + _bdot(inv_ref[:, L:2 * L], bot)

    v = lanes(v_ref)
    z = lanes(x1_ref) * (conv(v, 0) + v * tiled(skip_ref[0:1, :]))
    z = lanes(x2_ref) * (conv(z, 1) + z * tiled(skip_ref[1:2, :]))
    ct = o_ref.shape[1]
    for s in range(pack):
        o_ref[s * L:(s + 1) * L, :] = z[:, s * ct:(s + 1) * ct].astype(BF16)


def _longconv(grp, proj, spectra, skip, fwd, inv):
    L, d = grp.seq_len, grp.d
    ct = min(d, CONV_TILE_ELEMS // L)
    pack = max(1, CONV_MIN_LANES // ct)
    assert grp.n_seq % pack == 0
    per = d // ct
    kr, kiz, krn = spectra
    order = kr.shape[0]
    kspec = pl.BlockSpec((order, L, ct), lambda j, b: (0, 0, j))
    return pl.pallas_call(
        functools.partial(_longconv_kernel, L=L, pack=pack),
        grid=(per, grp.n_seq // pack),
        in_specs=[
            pl.BlockSpec((pack * L, ct), lambda j, b: (b, j)),
            pl.BlockSpec((pack * L, ct), lambda j, b: (b, per + j)),
            pl.BlockSpec((pack * L, ct), lambda j, b: (b, 2 * per + j)),
            kspec, kspec, kspec,
            pl.BlockSpec((order, ct), lambda j, b: (0, j)),
            pl.BlockSpec((2 * L, L), lambda j, b: (0, 0)),
            pl.BlockSpec((L, 2 * L), lambda j, b: (0, 0)),
        ],
        out_specs=pl.BlockSpec((pack * L, ct), lambda j, b: (b, j)),
        out_shape=jax.ShapeDtypeStruct((grp.rows, d), BF16),
        compiler_params=_params(("arbitrary", "arbitrary")),
        name=f"hyena_longconv_{L}",
    )(proj, proj, proj, kr, kiz, krn, skip, fwd, inv)


def _proj_res_kernel(*refs, has_bias, cached):
    refs = list(refs)
    a_ref, w_ref = refs[:2]
    refs = refs[2:]
    b_ref = refs.pop(0) if has_bias else None
    x_ref, gate_ref, o_ref = refs[:3]
    if cached:
        wc_scr = refs[3]
        j = pl.program_id(1)

        @pl.when(pl.program_id(0) == 0)
        def _():
            wc_scr[j] = w_ref[...].astype(BF16)

        w = wc_scr[j]
    else:
        w = w_ref[...]
    m = _bdot(a_ref[...], w)
    if has_bias:
        m = m + b_ref[...]
    o_ref[...] = x_ref[...] + gate_ref[0] * m


def _proj_res(grp, a, w, b, x, mod3, layer):
    k, n = w.shape
    cached = w.dtype != BF16
    assert not cached or k * n * 2 <= RES_WEIGHT_CACHE_BYTES
    tn = RES_COL_TILE
    nj = n // tn
    if cached:
        w_index = lambda i, j: (0, jnp.where(i == 0, j, nj - 1))
    else:
        w_index = lambda i, j: (0, j)
    in_specs = [
        pl.BlockSpec((TOKEN_TILE, k), lambda i, j: (i, 0)),
        pl.BlockSpec((k, tn), w_index),
    ]
    args = [a, w]
    if b is not None:
        in_specs.append(pl.BlockSpec((1, tn), lambda i, j: (0, j)))
        args.append(b.reshape(1, n))
    in_specs += [
        pl.BlockSpec((TOKEN_TILE, tn), lambda i, j: (i, j)),
        grp.mod_spec(layer, 2, tn, True),
    ]
    args += [x, mod3]
    return pl.pallas_call(
        functools.partial(_proj_res_kernel, has_bias=b is not None, cached=cached),
        grid=(grp.tiles, nj),
        in_specs=in_specs,
        out_specs=pl.BlockSpec((TOKEN_TILE, tn), lambda i, j: (i, j)),
        out_shape=jax.ShapeDtypeStruct((grp.rows, n), F32),
        scratch_shapes=[pltpu.VMEM((nj, k, tn), BF16)] if cached else [],
        compiler_params=_params(("arbitrary", "arbitrary")),
        name=f"proj_res_{k}_{grp.seq_len}",
    )(*args)


def _mlp_kernel(*refs, final_norm):
    if final_norm:
        (x_hbm, g_ref, sc_ref, sh_ref, gate_ref, w1_ref, w2_ref, fg_ref, o_ref,
         h_scr, x_buf, x_sem) = refs
    else:
        x_hbm, g_ref, sc_ref, sh_ref, gate_ref, w1_ref, w2_ref, o_ref, h_scr, x_buf, x_sem = refs
    f = pl.program_id(1)
    last_f = pl.num_programs(1) - 1
    tile_rows = x_buf.shape[0]
    _prefetch_tile(x_hbm, x_buf, x_sem, f)

    def step(first, last):
        w1 = w1_ref[0]
        w2 = w2_ref[0]
        for r in range(tile_rows // MLP_ROWS):
            rows = slice(r * MLP_ROWS, (r + 1) * MLP_ROWS)
            if first:
                x = x_buf[rows, :]
                h = _modulated_norm(x, g_ref[...], sc_ref[0], sh_ref[0]).astype(BF16)
                h_scr[rows, :] = h
            else:
                h = h_scr[rows, :]
            u = jnp.square(jnp.maximum(_bdot(h, w1), 0.0)).astype(BF16)
            acc = (x if first else o_ref[rows, :]) + gate_ref[0] * _bdot(u, w2)
            if last and final_norm:
                ms = jnp.mean(acc * acc, axis=-1, keepdims=True)
                acc = acc * lax.rsqrt(ms + EPS) * fg_ref[...]
            o_ref[rows, :] = acc

    pl.when(f == 0)(lambda: step(True, False))
    pl.when(jnp.logical_and(f > 0, f < last_f))(lambda: step(False, False))
    pl.when(f == last_f)(lambda: step(False, True))


def _mlp(grp, x, norm_g, mod3, layer, w1, w2, final_g=None):
    d = grp.d
    ff = w1.shape[2]
    nf = ff // FF_TILE
    assert nf >= 2
    in_specs = [
        pl.BlockSpec(memory_space=pl.ANY),
        pl.BlockSpec((1, d), lambda i, f: (0, 0)),
        grp.mod_spec(layer, 4, d, False),
        grp.mod_spec(layer, 3, d, False),
        grp.mod_spec(layer, 5, d, False),
        pl.BlockSpec((1, d, FF_TILE), lambda i, f: (layer, 0, f)),
        pl.BlockSpec((1, FF_TILE, d), lambda i, f: (layer, f, 0)),
    ]
    args = [x, norm_g.reshape(1, d), mod3, mod3, mod3, w1, w2]
    if final_g is not None:
        in_specs.append(pl.BlockSpec((1, d), lambda i, f: (0, 0)))
        args.append(final_g.reshape(1, d))
    return pl.pallas_call(
        functools.partial(_mlp_kernel, final_norm=final_g is not None),
        grid=(grp.tiles, nf),
        in_specs=in_specs,
        out_specs=pl.BlockSpec((TOKEN_TILE, d), lambda i, f: (i, 0)),
        out_shape=jax.ShapeDtypeStruct((grp.rows, d), F32),
        scratch_shapes=[pltpu.VMEM((TOKEN_TILE, d), BF16),
                        pltpu.VMEM((TOKEN_TILE, d), F32),
                        pltpu.SemaphoreType.DMA(())],
        compiler_params=_params(("arbitrary", "arbitrary")),
        name=f"mlp_{layer}_{grp.seq_len}",
    )(*args)


def _rotary_tables(L, dk, width):
    n_rows = L // GRID_W
    rows = np.repeat(np.arange(n_rows, dtype=np.float64), GRID_W)
    cols = np.tile(np.arange(GRID_W, dtype=np.float64), n_rows)
    n_pairs = dk // 4
    inv = ROPE_BASE ** (-np.arange(n_pairs, dtype=np.float64) / n_pairs)
    ang = np.concatenate([rows[:, None] * inv, cols[:, None] * inv], axis=-1)
    cos = np.repeat(np.cos(ang), 2, axis=-1)
    sin = np.repeat(np.sin(ang), 2, axis=-1)
    sin[:, 0::2] *= -1.0
    reps = width // dk
    return (jnp.asarray(np.tile(cos, (1, reps)), F32), jnp.asarray(np.tile(sin, (1, reps)), F32))


def _qkvg_kernel(*refs, rotary, n_mod, qk_tiles, k_tile0, k_scale):
    x_hbm, g_ref = refs[:2]
    sc_refs = refs[2:2 + n_mod]
    sh_refs = refs[2 + n_mod:2 + 2 * n_mod]
    refs = refs[2 + 2 * n_mod:]
    if rotary:
        w_ref, cos_ref, sin_ref, o_ref, h_scr, x_buf, x_sem = refs
    else:
        w_ref, o_ref, h_scr, x_buf, x_sem = refs
    j = pl.program_id(1)
    _prefetch_tile(x_hbm, x_buf, x_sem, j)
    is_qk = j < qk_tiles
    scale = jnp.where(jnp.logical_and(j >= k_tile0, is_qk), k_scale, 1.0)

    def project(fill, rotate):
        if rotate:
            lane = lax.broadcasted_iota(jnp.int32, (h_scr.shape[0], SUB_COLS), 1)
            even = (lane & 1) == 0
            seqs = h_scr.shape[0] // cos_ref.shape[0]
            cos = jnp.concatenate([cos_ref[...]] * seqs, axis=0)
            sin = jnp.concatenate([sin_ref[...]] * seqs, axis=0)
        for s in range(o_ref.shape[1] // SUB_COLS):
            cols = slice(s * SUB_COLS, (s + 1) * SUB_COLS)
            acc = _normed_dot(fill and s == 0, x_buf, g_ref, sc_refs, sh_refs, h_scr,
                              w_ref[:, cols].astype(BF16))
            if rotate:
                partner = jnp.where(even, pltpu.roll(acc, SUB_COLS - 1, 1), pltpu.roll(acc, 1, 1))
                acc = acc * cos + partner * sin
            o_ref[:, cols] = (acc * scale).astype(BF16)

    pl.when(j == 0)(lambda: project(True, rotary))
    if rotary:
        pl.when(jnp.logical_and(j > 0, is_qk))(lambda: project(False, True))
        pl.when(jnp.logical_not(is_qk))(lambda: project(False, False))
    else:
        pl.when(j > 0)(lambda: project(False, False))


def _qkvg(grp, x, norm_g, mod3, layer, w, dk, rotary):
    d = grp.d
    n = w.shape[1]
    assert n // COL_TILE >= 2
    sc_specs = grp.mod_specs(layer, 1, PROJ_TOKEN_TILE)
    sh_specs = grp.mod_specs(layer, 0, PROJ_TOKEN_TILE)
    in_specs = [
        pl.BlockSpec(memory_space=pl.ANY),
        pl.BlockSpec((1, d), lambda i, j: (0, 0)),
        *sc_specs, *sh_specs,
        pl.BlockSpec((d, COL_TILE), lambda i, j: (0, j)),
    ]
    args = [x, norm_g.reshape(1, d), *[mod3] * (2 * len(sc_specs)), w]
    if rotary:
        assert PROJ_TOKEN_TILE % grp.seq_len == 0
        cos, sin = _rotary_tables(grp.seq_len, dk, SUB_COLS)
        table = pl.BlockSpec((grp.seq_len, SUB_COLS), lambda i, j: (0, 0))
        in_specs += [table, table]
        args += [cos, sin]
    kern = functools.partial(_qkvg_kernel, rotary=rotary, n_mod=len(sc_specs),
                             qk_tiles=2 * d // COL_TILE, k_tile0=d // COL_TILE,
                             k_scale=dk ** -0.5)
    return pl.pallas_call(
        kern,
        grid=(grp.rows // PROJ_TOKEN_TILE, n // COL_TILE),
        in_specs=in_specs,
        out_specs=pl.BlockSpec((PROJ_TOKEN_TILE, COL_TILE), lambda i, j: (i, j)),
        out_shape=jax.ShapeDtypeStruct((grp.rows, n), BF16),
        scratch_shapes=_proj_scratch(d),
        compiler_params=_params(("arbitrary", "arbitrary")),
        name=f"ret_qkvg_{grp.seq_len}",
    )(*args)


def _log_sigmoid(x):
    return jnp.minimum(x, 0.0) - jnp.log1p(jnp.exp(-jnp.abs(x)))


def _retention_kernel(*refs, L, C, pack, has_s0, emit_state):
    refs = list(refs)
    decay_ref, q_ref, k_ref, v_ref, g_ref, gn_ref = refs[:6]
    refs = refs[6:]
    s0_ref = refs.pop(0) if has_s0 else None
    y_ref = refs.pop(0)
    so_ref = refs.pop(0) if emit_state else None
    mask_scr, dec_scr, cdec_scr, sb_scr = refs

    head = pl.program_id(0)
    dk = q_ref.shape[1]
    dv = v_ref.shape[1]
    nc = L // C

    @pl.when(pl.program_id(1) == 0)
    def _():
        def log_decay(direction, width):
            return _log_sigmoid(jnp.full((1, width), decay_ref[direction, head], F32))

        ri = lax.broadcasted_iota(jnp.int32, (C, C), 0)
        ci = lax.broadcasted_iota(jnp.int32, (C, C), 1)
        diff = (ri - ci).astype(F32)
        lower = diff >= 0
        upper = diff <= 0
        mask_scr[...] = (
            jnp.where(lower, jnp.exp(jnp.where(lower, diff, 0.0) * log_decay(0, C)), 0.0)
            + jnp.where(upper, jnp.exp(jnp.where(upper, -diff, 0.0) * log_decay(1, C)), 0.0))
        rk = lax.broadcasted_iota(jnp.int32, (C, dk), 0).astype(F32)
        dec_scr[0] = jnp.exp((rk + 1.0) * log_decay(0, dk))
        dec_scr[1] = jnp.exp((C - 1.0 - rk) * log_decay(0, dk))
        dec_scr[2] = jnp.exp((C - rk) * log_decay(1, dk))
        dec_scr[3] = jnp.exp(rk * log_decay(1, dk))
        cdec_scr[0:1, :] = jnp.exp(C * log_decay(0, dv))
        cdec_scr[1:2, :] = jnp.exp(C * log_decay(1, dv))

    gn = gn_ref[...]

    def advance(state, dec_row, upd):
        return upd if state is None else state * cdec_scr[dec_row:dec_row + 1, :] + upd

    for s in range(pack):
        base = s * L

        def kv(n, which):
            rows = slice(base + n * C, base + (n + 1) * C)
            kd = (k_ref[rows, :].astype(F32) * dec_scr[which]).astype(BF16)
            return lax.dot_general(kd, v_ref[rows, :], (((0,), (0,)), ((), ())),
                                   preferred_element_type=F32)

        state_b = s0_ref[s, 0, 1, 0] if has_s0 else None
        has_b = [False] * nc
        for n in reversed(range(nc)):
            if state_b is not None:
                sb_scr[n] = state_b.astype(BF16)
                has_b[n] = True
            if n > 0 or emit_state:
                state_b = advance(state_b, 1, kv(n, 3))
        if emit_state:
            so_ref[s, 0, 1, 0] = state_b

        state_f = s0_ref[s, 0, 0, 0] if has_s0 else None
        for n in range(nc):
            rows = slice(base + n * C, base + (n + 1) * C)
            q = q_ref[rows, :]
            scores = lax.dot_general(q, k_ref[rows, :], (((1,), (1,)), ((), ())),
                                     preferred_element_type=F32)
            y = _bdot((scores * mask_scr[...]).astype(BF16), v_ref[rows, :])
            qf = q.astype(F32)
            if state_f is not None:
                y = y + _bdot((qf * dec_scr[0]).astype(BF16), state_f.astype(BF16))
            if has_b[n]:
                y = y + _bdot((qf * dec_scr[2]).astype(BF16), sb_scr[n])
            mu = jnp.mean(y, axis=-1, keepdims=True)
            yc = y - mu
            var = jnp.mean(yc * yc, axis=-1, keepdims=True)
            yn = (yc * lax.rsqrt(var + EPS)) * gn
            gate = g_ref[rows, :].astype(F32)
            y_ref[rows, :] = ((gate * jax.nn.sigmoid(gate)) * yn).astype(BF16)
            if n < nc - 1 or emit_state:
                state_f = advance(state_f, 0, kv(n, 1))
        if emit_state:
            so_ref[s, 0, 0, 0] = state_f


def _retention(grp, qkvg, decay, gn_g, s0, heads, dk, dv, emit_state):
    L = grp.seq_len
    C = min(L, RET_CHUNK)
    pack = max(1, min(grp.n_seq, RET_STEP_ROWS // L))
    assert grp.n_seq % pack == 0
    kq = heads
    kv0 = 2 * heads * dk // dv
    kg0 = kv0 + heads
    rows = pack * L
    in_specs = [
        pl.BlockSpec(memory_space=pltpu.SMEM),
        pl.BlockSpec((rows, dk), lambda h, b: (b, h)),
        pl.BlockSpec((rows, dk), lambda h, b: (b, kq + h)),
        pl.BlockSpec((rows, dv), lambda h, b: (b, kv0 + h)),
        pl.BlockSpec((rows, dv), lambda h, b: (b, kg0 + h)),
        pl.BlockSpec((1, dv), lambda h, b: (0, h)),
    ]
    args = [decay, qkvg, qkvg, qkvg, qkvg, gn_g.reshape(1, heads * dv)]
    state_block = (pack, 1, 2, 1, dk, dv)
    state_index = lambda h, b: (b, 0, 0, h, 0, 0)
    if s0 is not None:
        in_specs.append(pl.BlockSpec(state_block, state_index))
        args.append(s0)
    out_specs = [pl.BlockSpec((rows, dv), lambda h, b: (b, h))]
    out_shape = [jax.ShapeDtypeStruct((grp.rows, heads * dv), BF16)]
    if emit_state:
        out_specs.append(pl.BlockSpec(state_block, state_index))
        out_shape.append(jax.ShapeDtypeStruct((grp.n_seq, 1, 2, heads, dk, dv), F32))
    kern = functools.partial(_retention_kernel, L=L, C=C, pack=pack, has_s0=s0 is not None,
                             emit_state=emit_state)
    return pl.pallas_call(
        kern,
        grid=(heads, grp.n_seq // pack),
        in_specs=in_specs,
        out_specs=out_specs,
        out_shape=out_shape,
        scratch_shapes=[
            pltpu.VMEM((C, C), F32),
            pltpu.VMEM((4, C, dk), F32),
            pltpu.VMEM((8, dv), F32),
            pltpu.VMEM((L // C, dk, dv), BF16),
        ],
        compiler_params=_params(("arbitrary", "arbitrary")),
        name=f"retention_{L}",
    )(*args)


def kernel(x_prompt, x_sample, state_ret, c, c_ctx, w_ada, b_ada, norm_g, final_g, hy_w_in, hy_b_in, hy_conv_w, hy_conv_b, hy_f_w1, hy_f_b1, hy_f_w2, hy_f_b2, hy_f_w3, hy_f_b3, hy_f_freq, hy_f_wout, hy_f_skip, hy_w_out, hy_b_out, ret_w_qkvg, ret_decay, ret_gn_g, ret_w_o, mlp_w1, mlp_w2):
    n_prompt, prompt_len, d = x_prompt.shape
    n_sample, sample_len, _ = x_sample.shape
    heads, dk, dv = state_ret.shape[3:]
    depth = w_ada.shape[0]
    assert depth == 2 and hy_w_in.shape[0] == 1 and ret_w_qkvg.shape[0] == 1
    assert n_sample + 1 <= MOD_ROWS

    cond = jnp.concatenate(
        [c_ctx[None, :], c, jnp.zeros((MOD_ROWS - 1 - n_sample, d), F32)], axis=0)
    mod = _ada_mod(cond, w_ada, b_ada)
    mod3 = mod.reshape(depth * MOD_ROWS, 1, mod.shape[-1])

    w_in, w_out, w_qkvg = hy_w_in[0], hy_w_out[0], ret_w_qkvg[0]
    w_o = ret_w_o[0].astype(BF16)
    w1, w2 = mlp_w1.astype(BF16), mlp_w2.astype(BF16)

    def run(grp, x, s0, latent):
        L = grp.seq_len
        proj = _hyena_in(grp, x, norm_g[0, 0], mod3, 0, w_in, hy_b_in[0], hy_conv_w[0],
                         hy_conv_b[0])
        fwd, inv = _dft_matrices(L)
        spectra = _filter_spectra(L, d, fwd, hy_f_w1[0], hy_f_b1[0], hy_f_w2[0], hy_f_b2[0],
                                  hy_f_w3[0], hy_f_b3[0], hy_f_freq[0], hy_f_wout[0])
        z = _longconv(grp, proj, spectra, hy_f_skip[0], fwd, inv)
        x = _proj_res(grp, z, w_out, hy_b_out[0], x, mod3, 0)
        x = _mlp(grp, x, norm_g[0, 1], mod3, 0, w1, w2)
        qkvg = _qkvg(grp, x, norm_g[1, 0], mod3, 1, w_qkvg, dk, rotary=latent)
        outs = _retention(grp, qkvg, ret_decay[0], ret_gn_g[0], s0, heads, dk, dv,
                          emit_state=s0 is None)
        x = _proj_res(grp, outs[0], w_o, None, x, mod3, 1)
        y = _mlp(grp, x, norm_g[1, 1], mod3, 1, w1, w2, final_g=final_g)
        return y, (outs[1] if s0 is None else None)

    prompt = _Group(n_prompt, prompt_len, d, mod_row0=0, mod_row_step=0)
    sample = _Group(n_sample, sample_len, d, mod_row0=1, mod_row_step=1)
    y_prompt, new_state = run(prompt, x_prompt.reshape(-1, d), None, False)
    y_sample, _ = run(sample, x_sample.reshape(-1, d), state_ret, True)
    return (y_prompt.reshape(x_prompt.shape), y_sample.reshape(x_sample.shape),
            new_state.astype(x_prompt.dtype))
```

```python
import functools
import math

import numpy as np
import jax
import jax.numpy as jnp
from jax import lax
from jax.experimental import pallas as pl
from jax.experimental.pallas import tpu as pltpu

F32 = jnp.float32
BF16 = jnp.bfloat16

EPS = 1e-6
GRID_W = 64
ROPE_BASE = 10000.0
FILTER_BANDS = 16
DECAY_TARGET = 1e-2
MIN_DECAY = math.log(DECAY_TARGET) / 1.5
MAX_DECAY = math.log(DECAY_TARGET) / 0.3

V7X_VMEM_BYTES = 64 * 1024 * 1024
VMEM_LIMIT = V7X_VMEM_BYTES - 8 * 1024 * 1024

TOKEN_TILE = 1024
COL_TILE = 1024
SUB_COLS = 512
RES_COL_TILE = 1024
RES_WEIGHT_CACHE_BYTES = 8 * 1024 * 1024
FF_TILE = 1024
MLP_ROWS = 256
ADA_COL_TILE = 2048
CONV_TILE_ELEMS = 256 * 1024
CONV_SHORT_SEQ = 256
CONV_MIN_LANES = 512
FILT_CH_TILE = 1024
RET_CHUNK = 256
RET_STEP_ROWS = 2048
MOD_ROWS = 8


def _params(semantics):
    return pltpu.CompilerParams(dimension_semantics=semantics, vmem_limit_bytes=VMEM_LIMIT)


def _bdot(a, b):
    return jnp.dot(a, b, preferred_element_type=F32)


def _fdot(a, b):
    return jnp.dot(a, b, preferred_element_type=F32, precision=lax.Precision.HIGHEST)


def _modulated_norm(x, g, scale, shift):
    ms = jnp.mean(x * x, axis=-1, keepdims=True)
    return (x * lax.rsqrt(ms + EPS) * g) * (1.0 + scale) + shift


def _normed_dot(fill, x_ref, g_ref, sc_ref, sh_ref, h_scr, w):
    if not fill:
        return _bdot(h_scr[...], w)
    parts = []
    for r in range(x_ref.shape[0] // MLP_ROWS):
        rows = slice(r * MLP_ROWS, (r + 1) * MLP_ROWS)
        h = _modulated_norm(x_ref[rows, :], g_ref[...], sc_ref[0], sh_ref[0]).astype(BF16)
        h_scr[rows, :] = h
        parts.append(_bdot(h, w))
    return jnp.concatenate(parts, axis=0)


class _Group:
    def __init__(self, n_seq, seq_len, d, mod_row0, mod_row_step):
        assert seq_len & (seq_len - 1) == 0
        assert (n_seq * seq_len) % TOKEN_TILE == 0
        assert TOKEN_TILE % seq_len == 0 or (mod_row_step == 0 and seq_len % TOKEN_TILE == 0)
        assert mod_row_step == 0 or seq_len == TOKEN_TILE
        self.n_seq, self.seq_len, self.d = n_seq, seq_len, d
        self.rows = n_seq * seq_len
        self.tiles = self.rows // TOKEN_TILE
        self.mod_row0, self.mod_row_step = mod_row0, mod_row_step

    def mod_spec(self, layer, chunk, width, tiled):
        per = self.d // width
        base = layer * MOD_ROWS + self.mod_row0
        step = self.mod_row_step

        def index(i, j):
            return (base + step * i, 0, chunk * per + (j if tiled else 0))

        return pl.BlockSpec((1, 1, width), index)


def _ada_kernel(cond_ref, w_ref, b_ref, o_ref):
    c = cond_ref[...]
    a = (c * jax.nn.sigmoid(c)).astype(BF16)
    o_ref[0] = _bdot(a, w_ref[0].astype(BF16)) + b_ref[0]


def _ada_mod(cond, w_ada, b_ada):
    depth, d, n = w_ada.shape
    tn = ADA_COL_TILE
    return pl.pallas_call(
        _ada_kernel,
        grid=(depth, n // tn),
        in_specs=[
            pl.BlockSpec((MOD_ROWS, d), lambda l, j: (0, 0)),
            pl.BlockSpec((1, d, tn), lambda l, j: (l, 0, j)),
            pl.BlockSpec((1, 1, tn), lambda l, j: (l, 0, j)),
        ],
        out_specs=pl.BlockSpec((1, MOD_ROWS, tn), lambda l, j: (l, 0, j)),
        out_shape=jax.ShapeDtypeStruct((depth, MOD_ROWS, n), F32),
        compiler_params=_params(("arbitrary", "arbitrary")),
        name="ada_mod",
    )(cond, w_ada, b_ada.reshape(depth, 1, n))


def _hyena_in_kernel(x_ref, g_ref, sc_ref, sh_ref, w_ref, b_ref, cw_ref, cb_ref, o_ref, h_scr,
                     *, seq_len):
    def project(fill):
        rows = h_scr.shape[0]
        pos = lax.broadcasted_iota(jnp.int32, (rows, SUB_COLS), 0) & (seq_len - 1)
        first = pos == 0
        last = pos == seq_len - 1
        for s in range(o_ref.shape[1] // SUB_COLS):
            cols = slice(s * SUB_COLS, (s + 1) * SUB_COLS)
            p = _normed_dot(fill and s == 0, x_ref, g_ref, sc_ref, sh_ref, h_scr,
                            w_ref[:, cols].astype(BF16)) + b_ref[:, cols]
            prev = jnp.where(first, 0.0, pltpu.roll(p, 1, 0))
            nxt = jnp.where(last, 0.0, pltpu.roll(p, rows - 1, 0))
            o_ref[:, cols] = (prev * cw_ref[0:1, cols] + p * cw_ref[1:2, cols]
                              + nxt * cw_ref[2:3, cols] + cb_ref[:, cols])

    j = pl.program_id(1)
    pl.when(j == 0)(lambda: project(True))
    pl.when(j > 0)(lambda: project(False))


def _hyena_in(grp, x, norm_g, mod3, layer, w, b, conv_w, conv_b):
    d = grp.d
    n = w.shape[1]
    return pl.pallas_call(
        functools.partial(_hyena_in_kernel, seq_len=grp.seq_len),
        grid=(grp.tiles, n // COL_TILE),
        in_specs=[
            pl.BlockSpec((TOKEN_TILE, d), lambda i, j: (i, 0)),
            pl.BlockSpec((1, d), lambda i, j: (0, 0)),
            grp.mod_spec(layer, 1, d, False),
            grp.mod_spec(layer, 0, d, False),
            pl.BlockSpec((d, COL_TILE), lambda i, j: (0, j)),
            pl.BlockSpec((1, COL_TILE), lambda i, j: (0, j)),
            pl.BlockSpec((3, COL_TILE), lambda i, j: (0, j)),
            pl.BlockSpec((1, COL_TILE), lambda i, j: (0, j)),
        ],
        out_specs=pl.BlockSpec((TOKEN_TILE, COL_TILE), lambda i, j: (i, j)),
        out_shape=jax.ShapeDtypeStruct((grp.rows, n), F32),
        scratch_shapes=[pltpu.VMEM((TOKEN_TILE, d), BF16)],
        compiler_params=_params(("arbitrary", "arbitrary")),
        name=f"hyena_in_{grp.seq_len}",
    )(x, norm_g.reshape(1, d), mod3, mod3, w, b.reshape(1, n), conv_w, conv_b.reshape(1, n))


def _dft_matrices(L):
    f = np.arange(L, dtype=np.int64)[:, None]
    t = np.arange(L, dtype=np.int64)[None, :]
    ang = ((f * t) % (2 * L)).astype(np.float64) * (np.pi / L)
    cos, sin = np.cos(ang), np.sin(ang)
    nyq = np.where(np.arange(L) % 2 == 0, 1.0, -1.0)
    bot = sin.copy()
    bot[0, :] = nyq
    fwd = np.concatenate([cos, bot], axis=0)
    wgt = np.full((L,), 2.0)
    wgt[0] = 1.0
    inv_cos = (cos * wgt[:, None]).T / (2 * L)
    inv_sin = (-2.0 * sin).T / (2 * L)
    inv_sin[:, 0] = nyq / (2 * L)
    inv = np.concatenate([inv_cos, inv_sin], axis=1)
    return jnp.asarray(fwd, F32).astype(BF16), jnp.asarray(inv, F32).astype(BF16)


def _filter_features(L, width):
    pos = np.arange(L, dtype=np.float64)
    t = pos / L
    omega = 2.0 * math.pi * pos / L
    bands = np.linspace(1e-4, FILTER_BANDS - 1, FILTER_BANDS)
    ang = omega[:, None] * bands[None, :]
    feat = np.concatenate([t[:, None], np.cos(ang), -np.sin(ang)], axis=-1)
    feat = np.pad(feat, ((0, 0), (0, width - feat.shape[1])))
    return jnp.asarray(feat, F32), jnp.asarray(t[:, None], F32)


def _filter_kernel(feat_ref, w1_ref, b1_ref, w2_ref, b2_ref, w3_ref, b3_ref, freq_ref,
                   wof_ref, wob_ref, t_ref, delta_ref, fwd_ref,
                   kr_ref, kiz_ref, krn_ref, h_scr, *, L):
    first = jnp.logical_and(pl.program_id(0) == 0, pl.program_id(1) == 0)

    @pl.when(first)
    def _():
        h = jnp.sin(freq_ref[0:1, :] * (_fdot(feat_ref[...], w1_ref[...]) + b1_ref[...]))
        h = jnp.sin(freq_ref[1:2, :] * (_fdot(h, w2_ref[...]) + b2_ref[...]))
        h = jnp.sin(freq_ref[2:3, :] * (_fdot(h, w3_ref[...]) + b3_ref[...]))
        h_scr[...] = h

    h = h_scr[...].astype(BF16)
    window = jnp.exp(-t_ref[...] * delta_ref[...])
    hf = _bdot(h, wof_ref[...].astype(BF16)) * window
    hb = _bdot(h, wob_ref[...].astype(BF16)) * window
    row = lax.broadcasted_iota(jnp.int32, hf.shape, 0)
    hb = jnp.where(row == 0, 0.0, hb)
    tap_sum = hf + hb
    tap_dif = hb - hf
    kr = _bdot(fwd_ref[0:L, :], tap_sum.astype(BF16))
    ki = _bdot(fwd_ref[L:2 * L, :], tap_dif.astype(BF16))
    sign = jnp.where((row & 1) == 0, 1.0, -1.0)
    k_nyq = jnp.sum(tap_sum * sign, axis=0, keepdims=True)
    kr_ref[0] = kr
    kiz_ref[0] = jnp.where(row == 0, 0.0, ki)
    krn_ref[0] = jnp.where(row == 0, -k_nyq, kr)


def _filter_spectra(L, d, fwd, w1, b1, w2, b2, w3, b3, freq, wout):
    width = w2.shape[0]
    order = wout.shape[1] // (2 * d)
    feat, t = _filter_features(L, width)
    w1p = jnp.pad(w1, ((0, width - w1.shape[0]), (0, 0)))
    deltas = jnp.asarray(np.abs(np.linspace(MIN_DECAY, MAX_DECAY, d))[None, :], F32)
    ct = FILT_CH_TILE
    per = d // ct
    small = lambda shape: pl.BlockSpec(shape, lambda o, j: (0, 0))
    out_spec = pl.BlockSpec((1, L, ct), lambda o, j: (o, 0, j))
    out_shape = jax.ShapeDtypeStruct((order, L, d), F32)
    return pl.pallas_call(
        functools.partial(_filter_kernel, L=L),
        grid=(order, per),
        in_specs=[
            small((L, width)), small((width, width)), small((1, width)),
            small((width, width)), small((1, width)),
            small((width, width)), small((1, width)), small((3, width)),
            pl.BlockSpec((width, ct), lambda o, j: (0, (2 * o) * per + j)),
            pl.BlockSpec((width, ct), lambda o, j: (0, (2 * o + 1) * per + j)),
            small((L, 1)),
            pl.BlockSpec((1, ct), lambda o, j: (0, j)),
            small((2 * L, L)),
        ],
        out_specs=[out_spec, out_spec, out_spec],
        out_shape=[out_shape, out_shape, out_shape],
        scratch_shapes=[pltpu.VMEM((L, width), F32)],
        compiler_params=_params(("arbitrary", "arbitrary")),
        name=f"hyena_filter_{L}",
    )(feat, w1p, b1.reshape(1, width), w2, b2.reshape(1, width), w3, b3.reshape(1, width),
      freq, wout, wout, t, deltas, fwd)


def _longconv_kernel(v_ref, x1_ref, x2_ref, kr_ref, kiz_ref, krn_ref, skip_ref, fwd_ref, inv_ref,
                     o_ref, *, L, pack):
    def lanes(ref):
        return jnp.concatenate([ref[s * L:(s + 1) * L, :] for s in range(pack)], axis=1)

    def tiled(x):
        return jnp.concatenate([x] * pack, axis=1)

    def conv(u, o):
        spec = _bdot(fwd_ref[...], u.astype(BF16))
        a, b = spec[:L], spec[L:]
        kr, kiz, krn = tiled(kr_ref[o]), tiled(kiz_ref[o]), tiled(krn_ref[o])
        top = (a * kr + b * kiz).astype(BF16)
        bot = (a * kiz - b * krn).astype(BF16)
        return _bdot(inv_ref[:, 0:L], top) + _bdot(inv_ref[:, L:2 * L], bot)

    v = lanes(v_ref)
    z = lanes(x1_ref) * (conv(v, 0) + v * tiled(skip_ref[0:1, :]))
    z = lanes(x2_ref) * (conv(z, 1) + z * tiled(skip_ref[1:2, :]))
    ct = o_ref.shape[1]
    for s in range(pack):
        o_ref[s * L:(s + 1) * L, :] = z[:, s * ct:(s + 1) * ct].astype(BF16)


def _longconv(grp, proj, spectra, skip, fwd, inv):
    L, d = grp.seq_len, grp.d
    ct = min(d, (2 if L <= CONV_SHORT_SEQ else 1) * CONV_TILE_ELEMS // L)
    pack = max(1, CONV_MIN_LANES // ct)
    assert grp.n_seq % pack == 0
    per = d // ct
    kr, kiz, krn = spectra
    order = kr.shape[0]
    kspec = pl.BlockSpec((order, L, ct), lambda j, b: (0, 0, j))
    return pl.pallas_call(
        functools.partial(_longconv_kernel, L=L, pack=pack),
        grid=(per, grp.n_seq // pack),
        in_specs=[
            pl.BlockSpec((pack * L, ct), lambda j, b: (b, j)),
            pl.BlockSpec((pack * L, ct), lambda j, b: (b, per + j)),
            pl.BlockSpec((pack * L, ct), lambda j, b: (b, 2 * per + j)),
            kspec, kspec, kspec,
            pl.BlockSpec((order, ct), lambda j, b: (0, j)),
            pl.BlockSpec((2 * L, L), lambda j, b: (0, 0)),
            pl.BlockSpec((L, 2 * L), lambda j, b: (0, 0)),
        ],
        out_specs=pl.BlockSpec((pack * L, ct), lambda j, b: (b, j)),
        out_shape=jax.ShapeDtypeStruct((grp.rows, d), BF16),
        compiler_params=_params(("arbitrary", "arbitrary")),
        name=f"hyena_longconv_{L}",
    )(proj, proj, proj, kr, kiz, krn, skip, fwd, inv)


def _proj_res_kernel(*refs, has_bias, cached):
    refs = list(refs)
    a_ref, w_ref = refs[:2]
    refs = refs[2:]
    b_ref = refs.pop(0) if has_bias else None
    x_ref, gate_ref, o_ref = refs[:3]
    if cached:
        wc_scr = refs[3]
        j = pl.program_id(1)

        @pl.when(pl.program_id(0) == 0)
        def _():
            wc_scr[j] = w_ref[...].astype(BF16)

        w = wc_scr[j]
    else:
        w = w_ref[...]
    m = _bdot(a_ref[...], w)
    if has_bias:
        m = m + b_ref[...]
    o_ref[...] = x_ref[...] + gate_ref[0] * m


def _proj_res(grp, a, w, b, x, mod3, layer):
    k, n = w.shape
    cached = w.dtype != BF16
    assert not cached or k * n * 2 <= RES_WEIGHT_CACHE_BYTES
    tn = RES_COL_TILE
    nj = n // tn
    if cached:
        w_index = lambda i, j: (0, jnp.where(i == 0, j, nj - 1))
    else:
        w_index = lambda i, j: (0, j)
    in_specs = [
        pl.BlockSpec((TOKEN_TILE, k), lambda i, j: (i, 0)),
        pl.BlockSpec((k, tn), w_index),
    ]
    args = [a, w]
    if b is not None:
        in_specs.append(pl.BlockSpec((1, tn), lambda i, j: (0, j)))
        args.append(b.reshape(1, n))
    in_specs += [
        pl.BlockSpec((TOKEN_TILE, tn), lambda i, j: (i, j)),
        grp.mod_spec(layer, 2, tn, True),
    ]
    args += [x, mod3]
    return pl.pallas_call(
        functools.partial(_proj_res_kernel, has_bias=b is not None, cached=cached),
        grid=(grp.tiles, nj),
        in_specs=in_specs,
        out_specs=pl.BlockSpec((TOKEN_TILE, tn), lambda i, j: (i, j)),
        out_shape=jax.ShapeDtypeStruct((grp.rows, n), F32),
        scratch_shapes=[pltpu.VMEM((nj, k, tn), BF16)] if cached else [],
        compiler_params=_params(("arbitrary", "arbitrary")),
        name=f"proj_res_{k}_{grp.seq_len}",
    )(*args)


def _mlp_kernel(*refs, final_norm):
    if final_norm:
        (x_hbm, g_ref, sc_ref, sh_ref, gate_ref, w1_ref, w2_ref, fg_ref, o_ref,
         h_scr, x_buf, x_sem) = refs
    else:
        x_hbm, g_ref, sc_ref, sh_ref, gate_ref, w1_ref, w2_ref, o_ref, h_scr, x_buf, x_sem = refs
    i = pl.program_id(0)
    f = pl.program_id(1)
    last_f = pl.num_programs(1) - 1
    tile_rows = x_buf.shape[0]

    def x_copy(tile):
        start = pl.multiple_of(tile * tile_rows, tile_rows)
        return pltpu.make_async_copy(x_hbm.at[pl.ds(start, tile_rows), :], x_buf, x_sem)

    @pl.when(jnp.logical_and(i == 0, f == 0))
    def _():
        x_copy(0).start()

    @pl.when(f == 0)
    def _():
        x_copy(i).wait()

    @pl.when(jnp.logical_and(f == 1, i + 1 < pl.num_programs(0)))
    def _():
        x_copy(i + 1).start()

    def step(first, last):
        w1 = w1_ref[0]
        w2 = w2_ref[0]
        for r in range(tile_rows // MLP_ROWS):
            rows = slice(r * MLP_ROWS, (r + 1) * MLP_ROWS)
            if first:
                x = x_buf[rows, :]
                h = _modulated_norm(x, g_ref[...], sc_ref[0], sh_ref[0]).astype(BF16)
                h_scr[rows, :] = h
            else:
                h = h_scr[rows, :]
            u = jnp.square(jnp.maximum(_bdot(h, w1), 0.0)).astype(BF16)
            acc = (x if first else o_ref[rows, :]) + gate_ref[0] * _bdot(u, w2)
            if last and final_norm:
                ms = jnp.mean(acc * acc, axis=-1, keepdims=True)
                acc = acc * lax.rsqrt(ms + EPS) * fg_ref[...]
            o_ref[rows, :] = acc

    pl.when(f == 0)(lambda: step(True, False))
    pl.when(jnp.logical_and(f > 0, f < last_f))(lambda: step(False, False))
    pl.when(f == last_f)(lambda: step(False, True))


def _mlp(grp, x, norm_g, mod3, layer, w1, w2, final_g=None):
    d = grp.d
    ff = w1.shape[2]
    nf = ff // FF_TILE
    assert nf >= 2
    in_specs = [
        pl.BlockSpec(memory_space=pl.ANY),
        pl.BlockSpec((1, d), lambda i, f: (0, 0)),
        grp.mod_spec(layer, 4, d, False),
        grp.mod_spec(layer, 3, d, False),
        grp.mod_spec(layer, 5, d, False),
        pl.BlockSpec((1, d, FF_TILE), lambda i, f: (layer, 0, f)),
        pl.BlockSpec((1, FF_TILE, d), lambda i, f: (layer, f, 0)),
    ]
    args = [x, norm_g.reshape(1, d), mod3, mod3, mod3, w1, w2]
    if final_g is not None:
        in_specs.append(pl.BlockSpec((1, d), lambda i, f: (0, 0)))
        args.append(final_g.reshape(1, d))
    return pl.pallas_call(
        functools.partial(_mlp_kernel, final_norm=final_g is not None),
        grid=(grp.tiles, nf),
        in_specs=in_specs,
        out_specs=pl.BlockSpec((TOKEN_TILE, d), lambda i, f: (i, 0)),
        out_shape=jax.ShapeDtypeStruct((grp.rows, d), F32),
        scratch_shapes=[pltpu.VMEM((TOKEN_TILE, d), BF16),
                        pltpu.VMEM((TOKEN_TILE, d), F32),
                        pltpu.SemaphoreType.DMA(())],
        compiler_params=_params(("arbitrary", "arbitrary")),
        name=f"mlp_{layer}_{grp.seq_len}",
    )(*args)


def _rotary_tables(L, dk, width):
    n_rows = L // GRID_W
    rows = np.repeat(np.arange(n_rows, dtype=np.float64), GRID_W)
    cols = np.tile(np.arange(GRID_W, dtype=np.float64), n_rows)
    n_pairs = dk // 4
    inv = ROPE_BASE ** (-np.arange(n_pairs, dtype=np.float64) / n_pairs)
    ang = np.concatenate([rows[:, None] * inv, cols[:, None] * inv], axis=-1)
    cos = np.repeat(np.cos(ang), 2, axis=-1)
    sin = np.repeat(np.sin(ang), 2, axis=-1)
    sin[:, 0::2] *= -1.0
    reps = width // dk
    return (jnp.asarray(np.tile(cos, (1, reps)), F32), jnp.asarray(np.tile(sin, (1, reps)), F32))


def _qkvg_kernel(*refs, rotary, qk_tiles, k_tile0, k_scale):
    if rotary:
        x_ref, g_ref, sc_ref, sh_ref, w_ref, cos_ref, sin_ref, o_ref, h_scr = refs
    else:
        x_ref, g_ref, sc_ref, sh_ref, w_ref, o_ref, h_scr = refs
    j = pl.program_id(1)
    is_qk = j < qk_tiles
    scale = jnp.where(jnp.logical_and(j >= k_tile0, is_qk), k_scale, 1.0)

    def project(fill, rotate):
        if rotate:
            lane = lax.broadcasted_iota(jnp.int32, (h_scr.shape[0], SUB_COLS), 1)
            even = (lane & 1) == 0
        for s in range(o_ref.shape[1] // SUB_COLS):
            cols = slice(s * SUB_COLS, (s + 1) * SUB_COLS)
            acc = _normed_dot(fill and s == 0, x_ref, g_ref, sc_ref, sh_ref, h_scr,
                              w_ref[:, cols].astype(BF16))
            if rotate:
                partner = jnp.where(even, pltpu.roll(acc, SUB_COLS - 1, 1), pltpu.roll(acc, 1, 1))
                acc = acc * cos_ref[...] + partner * sin_ref[...]
            o_ref[:, cols] = (acc * scale).astype(BF16)

    pl.when(j == 0)(lambda: project(True, rotary))
    if rotary:
        pl.when(jnp.logical_and(j > 0, is_qk))(lambda: project(False, True))
        pl.when(jnp.logical_not(is_qk))(lambda: project(False, False))
    else:
        pl.when(j > 0)(lambda: project(False, False))


def _qkvg(grp, x, norm_g, mod3, layer, w, dk, rotary):
    d = grp.d
    n = w.shape[1]
    in_specs = [
        pl.BlockSpec((TOKEN_TILE, d), lambda i, j: (i, 0)),
        pl.BlockSpec((1, d), lambda i, j: (0, 0)),
        grp.mod_spec(layer, 1, d, False),
        grp.mod_spec(layer, 0, d, False),
        pl.BlockSpec((d, COL_TILE), lambda i, j: (0, j)),
    ]
    args = [x, norm_g.reshape(1, d), mod3, mod3, w]
    if rotary:
        assert grp.seq_len == TOKEN_TILE
        cos, sin = _rotary_tables(grp.seq_len, dk, SUB_COLS)
        table = pl.BlockSpec((TOKEN_TILE, SUB_COLS), lambda i, j: (0, 0))
        in_specs += [table, table]
        args += [cos, sin]
    kern = functools.partial(_qkvg_kernel, rotary=rotary, qk_tiles=2 * d // COL_TILE,
                             k_tile0=d // COL_TILE, k_scale=dk ** -0.5)
    return pl.pallas_call(
        kern,
        grid=(grp.tiles, n // COL_TILE),
        in_specs=in_specs,
        out_specs=pl.BlockSpec((TOKEN_TILE, COL_TILE), lambda i, j: (i, j)),
        out_shape=jax.ShapeDtypeStruct((grp.rows, n), BF16),
        scratch_shapes=[pltpu.VMEM((TOKEN_TILE, d), BF16)],
        compiler_params=_params(("arbitrary", "arbitrary")),
        name=f"ret_qkvg_{grp.seq_len}",
    )(*args)


def _log_sigmoid(x):
    return jnp.minimum(x, 0.0) - jnp.log1p(jnp.exp(-jnp.abs(x)))


def _retention_kernel(*refs, L, C, pack, has_s0, emit_state):
    refs = list(refs)
    decay_ref, q_ref, k_ref, v_ref, g_ref, gn_ref = refs[:6]
    refs = refs[6:]
    s0_ref = refs.pop(0) if has_s0 else None
    y_ref = refs.pop(0)
    so_ref = refs.pop(0) if emit_state else None
    mask_scr, dec_scr, cdec_scr, sb_scr = refs

    head = pl.program_id(0)
    dk = q_ref.shape[1]
    dv = v_ref.shape[1]
    nc = L // C

    @pl.when(pl.program_id(1) == 0)
    def _():
        def log_decay(direction, width):
            return _log_sigmoid(jnp.full((1, width), decay_ref[direction, head], F32))

        ri = lax.broadcasted_iota(jnp.int32, (C, C), 0)
        ci = lax.broadcasted_iota(jnp.int32, (C, C), 1)
        diff = (ri - ci).astype(F32)
        lower = diff >= 0
        upper = diff <= 0
        mask_scr[...] = (
            jnp.where(lower, jnp.exp(jnp.where(lower, diff, 0.0) * log_decay(0, C)), 0.0)
            + jnp.where(upper, jnp.exp(jnp.where(upper, -diff, 0.0) * log_decay(1, C)), 0.0))
        rk = lax.broadcasted_iota(jnp.int32, (C, dk), 0).astype(F32)
        dec_scr[0] = jnp.exp((rk + 1.0) * log_decay(0, dk))
        dec_scr[1] = jnp.exp((C - 1.0 - rk) * log_decay(0, dk))
        dec_scr[2] = jnp.exp((C - rk) * log_decay(1, dk))
        dec_scr[3] = jnp.exp(rk * log_decay(1, dk))
        cdec_scr[0:1, :] = jnp.exp(C * log_decay(0, dv))
        cdec_scr[1:2, :] = jnp.exp(C * log_decay(1, dv))

    gn = gn_ref[...]

    def advance(state, dec_row, upd):
        return upd if state is None else state * cdec_scr[dec_row:dec_row + 1, :] + upd

    for s in range(pack):
        base = s * L

        def kv(n, which):
            rows = slice(base + n * C, base + (n + 1) * C)
            kd = (k_ref[rows, :].astype(F32) * dec_scr[which]).astype(BF16)
            return lax.dot_general(kd, v_ref[rows, :], (((0,), (0,)), ((), ())),
                                   preferred_element_type=F32)

        state_b = s0_ref[s, 0, 1, 0] if has_s0 else None
        has_b = [False] * nc
        for n in reversed(range(nc)):
            if state_b is not None:
                sb_scr[n] = state_b.astype(BF16)
                has_b[n] = True
            if n > 0 or emit_state:
                state_b = advance(state_b, 1, kv(n, 3))
        if emit_state:
            so_ref[s, 0, 1, 0] = state_b

        state_f = s0_ref[s, 0, 0, 0] if has_s0 else None
        for n in range(nc):
            rows = slice(base + n * C, base + (n + 1) * C)
            q = q_ref[rows, :]
            scores = lax.dot_general(q, k_ref[rows, :], (((1,), (1,)), ((), ())),
                                     preferred_element_type=F32)
            y = _bdot((scores * mask_scr[...]).astype(BF16), v_ref[rows, :])
            qf = q.astype(F32)
            if state_f is not None:
                y = y + _bdot((qf * dec_scr[0]).astype(BF16), state_f.astype(BF16))
            if has_b[n]:
                y = y + _bdot((qf * dec_scr[2]).astype(BF16), sb_scr[n])
            mu = jnp.mean(y, axis=-1, keepdims=True)
            yc = y - mu
            var = jnp.mean(yc * yc, axis=-1, keepdims=True)
            yn = (yc * lax.rsqrt(var + EPS)) * gn
            gate = g_ref[rows, :].astype(F32)
            y_ref[rows, :] = ((gate * jax.nn.sigmoid(gate)) * yn).astype(BF16)
            if n < nc - 1 or emit_state:
                state_f = advance(state_f, 0, kv(n, 1))
        if emit_state:
            so_ref[s, 0, 0, 0] = state_f


def _retention(grp, qkvg, decay, gn_g, s0, heads, dk, dv, emit_state):
    L = grp.seq_len
    C = min(L, RET_CHUNK)
    pack = max(1, min(grp.n_seq, RET_STEP_ROWS // L))
    assert grp.n_seq % pack == 0
    kq = heads
    kv0 = 2 * heads * dk // dv
    kg0 = kv0 + heads
    rows = pack * L
    in_specs = [
        pl.BlockSpec(memory_space=pltpu.SMEM),
        pl.BlockSpec((rows, dk), lambda h, b: (b, h)),
        pl.BlockSpec((rows, dk), lambda h, b: (b, kq + h)),
        pl.BlockSpec((rows, dv), lambda h, b: (b, kv0 + h)),
        pl.BlockSpec((rows, dv), lambda h, b: (b, kg0 + h)),
        pl.BlockSpec((1, dv), lambda h, b: (0, h)),
    ]
    args = [decay, qkvg, qkvg, qkvg, qkvg, gn_g.reshape(1, heads * dv)]
    state_block = (pack, 1, 2, 1, dk, dv)
    state_index = lambda h, b: (b, 0, 0, h, 0, 0)
    if s0 is not None:
        in_specs.append(pl.BlockSpec(state_block, state_index))
        args.append(s0)
    out_specs = [pl.BlockSpec((rows, dv), lambda h, b: (b, h))]
    out_shape = [jax.ShapeDtypeStruct((grp.rows, heads * dv), BF16)]
    if emit_state:
        out_specs.append(pl.BlockSpec(state_block, state_index))
        out_shape.append(jax.ShapeDtypeStruct((grp.n_seq, 1, 2, heads, dk, dv), F32))
    kern = functools.partial(_retention_kernel, L=L, C=C, pack=pack, has_s0=s0 is not None,
                             emit_state=emit_state)
    return pl.pallas_call(
        kern,
        grid=(heads, grp.n_seq // pack),
        in_specs=in_specs,
        out_specs=out_specs,
        out_shape=out_shape,
        scratch_shapes=[
            pltpu.VMEM((C, C), F32),
            pltpu.VMEM((4, C, dk), F32),
            pltpu.VMEM((8, dv), F32),
            pltpu.VMEM((L // C, dk, dv), BF16),
        ],
        compiler_params=_params(("arbitrary", "arbitrary")),
        name=f"retention_{L}",
    )(*args)


def kernel(x_prompt, x_sample, state_ret, c, c_ctx, w_ada, b_ada, norm_g, final_g, hy_w_in, hy_b_in, hy_conv_w, hy_conv_b, hy_f_w1, hy_f_b1, hy_f_w2, hy_f_b2, hy_f_w3, hy_f_b3, hy_f_freq, hy_f_wout, hy_f_skip, hy_w_out, hy_b_out, ret_w_qkvg, ret_decay, ret_gn_g, ret_w_o, mlp_w1, mlp_w2):
    n_prompt, prompt_len, d = x_prompt.shape
    n_sample, sample_len, _ = x_sample.shape
    heads, dk, dv = state_ret.shape[3:]
    depth = w_ada.shape[0]
    assert depth == 2 and hy_w_in.shape[0] == 1 and ret_w_qkvg.shape[0] == 1
    assert n_sample + 1 <= MOD_ROWS

    cond = jnp.concatenate(
        [c_ctx[None, :], c, jnp.zeros((MOD_ROWS - 1 - n_sample, d), F32)], axis=0)
    mod = _ada_mod(cond, w_ada, b_ada)
    mod3 = mod.reshape(depth * MOD_ROWS, 1, mod.shape[-1])

    w_in, w_out, w_qkvg = hy_w_in[0], hy_w_out[0], ret_w_qkvg[0]
    w_o = ret_w_o[0].astype(BF16)
    w1, w2 = mlp_w1.astype(BF16), mlp_w2.astype(BF16)

    def run(grp, x, s0, latent):
        L = grp.seq_len
        proj = _hyena_in(grp, x, norm_g[0, 0], mod3, 0, w_in, hy_b_in[0], hy_conv_w[0],
                         hy_conv_b[0])
        fwd, inv = _dft_matrices(L)
        spectra = _filter_spectra(L, d, fwd, hy_f_w1[0], hy_f_b1[0], hy_f_w2[0], hy_f_b2[0],
                                  hy_f_w3[0], hy_f_b3[0], hy_f_freq[0], hy_f_wout[0])
        z = _longconv(grp, proj, spectra, hy_f_skip[0], fwd, inv)
        x = _proj_res(grp, z, w_out, hy_b_out[0], x, mod3, 0)
        x = _mlp(grp, x, norm_g[0, 1], mod3, 0, w1, w2)
        qkvg = _qkvg(grp, x, norm_g[1, 0], mod3, 1, w_qkvg, dk, rotary=latent)
        outs = _retention(grp, qkvg, ret_decay[0], ret_gn_g[0], s0, heads, dk, dv,
                          emit_state=s0 is None)
        x = _proj_res(grp, outs[0], w_o, None, x, mod3, 1)
        y = _mlp(grp, x, norm_g[1, 1], mod3, 1, w1, w2, final_g=final_g)
        return y, (outs[1] if s0 is None else None)

    prompt = _Group(n_prompt, prompt_len, d, mod_row0=0, mod_row_step=0)
    sample = _Group(n_sample, sample_len, d, mod_row0=1, mod_row_step=1)
    y_prompt, new_state = run(prompt, x_prompt.reshape(-1, d), None, False)
    y_sample, _ = run(sample, x_sample.reshape(-1, d), state_ret, True)
    return (y_prompt.reshape(x_prompt.shape), y_sample.reshape(x_sample.shape),
            new_state.astype(x_prompt.dtype))
```

```python
import functools
import math

import numpy as np
import jax
import jax.numpy as jnp
from jax import lax
from jax.experimental import pallas as pl
from jax.experimental.pallas import tpu as pltpu

F32 = jnp.float32
BF16 = jnp.bfloat16

EPS = 1e-6
GRID_W = 64
ROPE_BASE = 10000.0
FILTER_BANDS = 16
DECAY_TARGET = 1e-2
MIN_DECAY = math.log(DECAY_TARGET) / 1.5
MAX_DECAY = math.log(DECAY_TARGET) / 0.3

V7X_VMEM_BYTES = 64 * 1024 * 1024
VMEM_LIMIT = V7X_VMEM_BYTES - 8 * 1024 * 1024

TOKEN_TILE = 1024
COL_TILE = 1024
SUB_COLS = 512
RES_COL_TILE = 1024
RES_WEIGHT_CACHE_BYTES = 8 * 1024 * 1024
FF_TILE = 1024
MLP_ROWS = 256
ADA_COL_TILE = 2048
CONV_TILE_ELEMS = 256 * 1024
CONV_SHORT_SEQ = 256
CONV_MIN_LANES = 512
FILT_CH_TILE = 1024
RET_CHUNK = 256
RET_STEP_ROWS = 2048
MOD_ROWS = 8


def _params(semantics):
    return pltpu.CompilerParams(dimension_semantics=semantics, vmem_limit_bytes=VMEM_LIMIT)


def _bdot(a, b):
    return jnp.dot(a, b, preferred_element_type=F32)


def _fdot(a, b):
    return jnp.dot(a, b, preferred_element_type=F32, precision=lax.Precision.HIGHEST)


def _modulated_norm(x, g, scale, shift):
    ms = jnp.mean(x * x, axis=-1, keepdims=True)
    return (x * lax.rsqrt(ms + EPS) * g) * (1.0 + scale) + shift


def _normed_dot(fill, x_ref, g_ref, sc_ref, sh_ref, h_scr, w):
    if not fill:
        return _bdot(h_scr[...], w)
    parts = []
    for r in range(x_ref.shape[0] // MLP_ROWS):
        rows = slice(r * MLP_ROWS, (r + 1) * MLP_ROWS)
        h = _modulated_norm(x_ref[rows, :], g_ref[...], sc_ref[0], sh_ref[0]).astype(BF16)
        h_scr[rows, :] = h
        parts.append(_bdot(h, w))
    return jnp.concatenate(parts, axis=0)


class _Group:
    def __init__(self, n_seq, seq_len, d, mod_row0, mod_row_step):
        assert seq_len & (seq_len - 1) == 0
        assert (n_seq * seq_len) % TOKEN_TILE == 0
        assert TOKEN_TILE % seq_len == 0 or (mod_row_step == 0 and seq_len % TOKEN_TILE == 0)
        assert mod_row_step == 0 or seq_len == TOKEN_TILE
        self.n_seq, self.seq_len, self.d = n_seq, seq_len, d
        self.rows = n_seq * seq_len
        self.tiles = self.rows // TOKEN_TILE
        self.mod_row0, self.mod_row_step = mod_row0, mod_row_step

    def mod_spec(self, layer, chunk, width, tiled):
        per = self.d // width
        base = layer * MOD_ROWS + self.mod_row0
        step = self.mod_row_step

        def index(i, j):
            return (base + step * i, 0, chunk * per + (j if tiled else 0))

        return pl.BlockSpec((1, 1, width), index)


def _ada_kernel(cond_ref, w_ref, b_ref, o_ref):
    c = cond_ref[...]
    a = (c * jax.nn.sigmoid(c)).astype(BF16)
    o_ref[0] = _bdot(a, w_ref[0].astype(BF16)) + b_ref[0]


def _ada_mod(cond, w_ada, b_ada):
    depth, d, n = w_ada.shape
    tn = ADA_COL_TILE
    return pl.pallas_call(
        _ada_kernel,
        grid=(depth, n // tn),
        in_specs=[
            pl.BlockSpec((MOD_ROWS, d), lambda l, j: (0, 0)),
            pl.BlockSpec((1, d, tn), lambda l, j: (l, 0, j)),
            pl.BlockSpec((1, 1, tn), lambda l, j: (l, 0, j)),
        ],
        out_specs=pl.BlockSpec((1, MOD_ROWS, tn), lambda l, j: (l, 0, j)),
        out_shape=jax.ShapeDtypeStruct((depth, MOD_ROWS, n), F32),
        compiler_params=_params(("arbitrary", "arbitrary")),
        name="ada_mod",
    )(cond, w_ada, b_ada.reshape(depth, 1, n))


def _hyena_in_kernel(x_ref, g_ref, sc_ref, sh_ref, w_ref, b_ref, cw_ref, cb_ref, o_ref, h_scr,
                     *, seq_len):
    def project(fill):
        rows = h_scr.shape[0]
        pos = lax.broadcasted_iota(jnp.int32, (rows, SUB_COLS), 0) & (seq_len - 1)
        first = pos == 0
        last = pos == seq_len - 1
        for s in range(o_ref.shape[1] // SUB_COLS):
            cols = slice(s * SUB_COLS, (s + 1) * SUB_COLS)
            p = _normed_dot(fill and s == 0, x_ref, g_ref, sc_ref, sh_ref, h_scr,
                            w_ref[:, cols].astype(BF16)) + b_ref[:, cols]
            prev = jnp.where(first, 0.0, pltpu.roll(p, 1, 0))
            nxt = jnp.where(last, 0.0, pltpu.roll(p, rows - 1, 0))
            o_ref[:, cols] = (prev * cw_ref[0:1, cols] + p * cw_ref[1:2, cols]
                              + nxt * cw_ref[2:3, cols] + cb_ref[:, cols])

    j = pl.program_id(1)
    pl.when(j == 0)(lambda: project(True))
    pl.when(j > 0)(lambda: project(False))


def _hyena_in(grp, x, norm_g, mod3, layer, w, b, conv_w, conv_b):
    d = grp.d
    n = w.shape[1]
    return pl.pallas_call(
        functools.partial(_hyena_in_kernel, seq_len=grp.seq_len),
        grid=(grp.tiles, n // COL_TILE),
        in_specs=[
            pl.BlockSpec((TOKEN_TILE, d), lambda i, j: (i, 0)),
            pl.BlockSpec((1, d), lambda i, j: (0, 0)),
            grp.mod_spec(layer, 1, d, False),
            grp.mod_spec(layer, 0, d, False),
            pl.BlockSpec((d, COL_TILE), lambda i, j: (0, j)),
            pl.BlockSpec((1, COL_TILE), lambda i, j: (0, j)),
            pl.BlockSpec((3, COL_TILE), lambda i, j: (0, j)),
            pl.BlockSpec((1, COL_TILE), lambda i, j: (0, j)),
        ],
        out_specs=pl.BlockSpec((TOKEN_TILE, COL_TILE), lambda i, j: (i, j)),
        out_shape=jax.ShapeDtypeStruct((grp.rows, n), F32),
        scratch_shapes=[pltpu.VMEM((TOKEN_TILE, d), BF16)],
        compiler_params=_params(("arbitrary", "arbitrary")),
        name=f"hyena_in_{grp.seq_len}",
    )(x, norm_g.reshape(1, d), mod3, mod3, w, b.reshape(1, n), conv_w, conv_b.reshape(1, n))


def _dft_matrices(L):
    f = np.arange(L, dtype=np.int64)[:, None]
    t = np.arange(L, dtype=np.int64)[None, :]
    ang = ((f * t) % (2 * L)).astype(np.float64) * (np.pi / L)
    cos, sin = np.cos(ang), np.sin(ang)
    nyq = np.where(np.arange(L) % 2 == 0, 1.0, -1.0)
    bot = sin.copy()
    bot[0, :] = nyq
    fwd = np.concatenate([cos, bot], axis=0)
    wgt = np.full((L,), 2.0)
    wgt[0] = 1.0
    inv_cos = (cos * wgt[:, None]).T / (2 * L)
    inv_sin = (-2.0 * sin).T / (2 * L)
    inv_sin[:, 0] = nyq / (2 * L)
    inv = np.concatenate([inv_cos, inv_sin], axis=1)
    return jnp.asarray(fwd, F32).astype(BF16), jnp.asarray(inv, F32).astype(BF16)


def _filter_features(L, width):
    pos = np.arange(L, dtype=np.float64)
    t = pos / L
    omega = 2.0 * math.pi * pos / L
    bands = np.linspace(1e-4, FILTER_BANDS - 1, FILTER_BANDS)
    ang = omega[:, None] * bands[None, :]
    feat = np.concatenate([t[:, None], np.cos(ang), -np.sin(ang)], axis=-1)
    feat = np.pad(feat, ((0, 0), (0, width - feat.shape[1])))
    return jnp.asarray(feat, F32), jnp.asarray(t[:, None], F32)


def _filter_kernel(feat_ref, w1_ref, b1_ref, w2_ref, b2_ref, w3_ref, b3_ref, freq_ref,
                   wof_ref, wob_ref, t_ref, delta_ref, fwd_ref,
                   kr_ref, kiz_ref, krn_ref, h_scr, *, L):
    first = jnp.logical_and(pl.program_id(0) == 0, pl.program_id(1) == 0)

    @pl.when(first)
    def _():
        h = jnp.sin(freq_ref[0:1, :] * (_fdot(feat_ref[...], w1_ref[...]) + b1_ref[...]))
        h = jnp.sin(freq_ref[1:2, :] * (_fdot(h, w2_ref[...]) + b2_ref[...]))
        h = jnp.sin(freq_ref[2:3, :] * (_fdot(h, w3_ref[...]) + b3_ref[...]))
        h_scr[...] = h

    h = h_scr[...].astype(BF16)
    window = jnp.exp(-t_ref[...] * delta_ref[...])
    hf = _bdot(h, wof_ref[...].astype(BF16)) * window
    hb = _bdot(h, wob_ref[...].astype(BF16)) * window
    row = lax.broadcasted_iota(jnp.int32, hf.shape, 0)
    hb = jnp.where(row == 0, 0.0, hb)
    tap_sum = hf + hb
    tap_dif = hb - hf
    kr = _bdot(fwd_ref[0:L, :], tap_sum.astype(BF16))
    ki = _bdot(fwd_ref[L:2 * L, :], tap_dif.astype(BF16))
    sign = jnp.where((row & 1) == 0, 1.0, -1.0)
    k_nyq = jnp.sum(tap_sum * sign, axis=0, keepdims=True)
    kr_ref[0] = kr
    kiz_ref[0] = jnp.where(row == 0, 0.0, ki)
    krn_ref[0] = jnp.where(row == 0, -k_nyq, kr)


def _filter_spectra(L, d, fwd, w1, b1, w2, b2, w3, b3, freq, wout):
    width = w2.shape[0]
    order = wout.shape[1] // (2 * d)
    feat, t = _filter_features(L, width)
    w1p = jnp.pad(w1, ((0, width - w1.shape[0]), (0, 0)))
    deltas = jnp.asarray(np.abs(np.linspace(MIN_DECAY, MAX_DECAY, d))[None, :], F32)
    ct = FILT_CH_TILE
    per = d // ct
    small = lambda shape: pl.BlockSpec(shape, lambda o, j: (0, 0))
    out_spec = pl.BlockSpec((1, L, ct), lambda o, j: (o, 0, j))
    out_shape = jax.ShapeDtypeStruct((order, L, d), F32)
    return pl.pallas_call(
        functools.partial(_filter_kernel, L=L),
        grid=(order, per),
        in_specs=[
            small((L, width)), small((width, width)), small((1, width)),
            small((width, width)), small((1, width)),
            small((width, width)), small((1, width)), small((3, width)),
            pl.BlockSpec((width, ct), lambda o, j: (0, (2 * o) * per + j)),
            pl.BlockSpec((width, ct), lambda o, j: (0, (2 * o + 1) * per + j)),
            small((L, 1)),
            pl.BlockSpec((1, ct), lambda o, j: (0, j)),
            small((2 * L, L)),
        ],
        out_specs=[out_spec, out_spec, out_spec],
        out_shape=[out_shape, out_shape, out_shape],
        scratch_shapes=[pltpu.VMEM((L, width), F32)],
        compiler_params=_params(("arbitrary", "arbitrary")),
        name=f"hyena_filter_{L}",
    )(feat, w1p, b1.reshape(1, width), w2, b2.reshape(1, width), w3, b3.reshape(1, width),
      freq, wout, wout, t, deltas, fwd)


def _longconv_kernel(v_ref, x1_ref, x2_ref, kr_ref, kiz_ref, krn_ref, skip_ref, fwd_ref, inv_ref,
                     o_ref, *, L, pack):
    def lanes(ref):
        return jnp.concatenate([ref[s * L:(s + 1) * L, :] for s in range(pack)], axis=1)

    def tiled(x):
        return jnp.concatenate([x] * pack, axis=1)

    def conv(u, o):
        spec = _bdot(fwd_ref[...], u.astype(BF16))
        a, b = spec[:L], spec[L:]
        kr, kiz, krn = tiled(kr_ref[o]), tiled(kiz_ref[o]), tiled(krn_ref[o])
        top = (a * kr + b * kiz).astype(BF16)
        bot = (a * kiz - b * krn).astype(BF16)
        return _bdot(inv_ref[:, 0:L], top) + _bdot(inv_ref[:, L:2 * L], bot)

    v = lanes(v_ref)
    z = lanes(x1_ref) * (conv(v, 0) + v * tiled(skip_ref[0:1, :]))
    z = lanes(x2_ref) * (conv(z, 1) + z * tiled(skip_ref[1:2, :]))
    ct = o_ref.shape[1]
    for s in range(pack):
        o_ref[s * L:(s + 1) * L, :] = z[:, s * ct:(s + 1) * ct].astype(BF16)


def _longconv(grp, proj, spectra, skip, fwd, inv):
    L, d = grp.seq_len, grp.d
    ct = min(d, (2 if L <= CONV_SHORT_SEQ else 1) * CONV_TILE_ELEMS // L)
    pack = max(1, CONV_MIN_LANES // ct)
    assert grp.n_seq % pack == 0
    per = d // ct
    kr, kiz, krn = spectra
    order = kr.shape[0]
    kspec = pl.BlockSpec((order, L, ct), lambda j, b: (0, 0, j))
    return pl.pallas_call(
        functools.partial(_longconv_kernel, L=L, pack=pack),
        grid=(per, grp.n_seq // pack),
        in_specs=[
            pl.BlockSpec((pack * L, ct), lambda j, b: (b, j)),
            pl.BlockSpec((pack * L, ct), lambda j, b: (b, per + j)),
            pl.BlockSpec((pack * L, ct), lambda j, b: (b, 2 * per + j)),
            kspec, kspec, kspec,
            pl.BlockSpec((order, ct), lambda j, b: (0, j)),
            pl.BlockSpec((2 * L, L), lambda j, b: (0, 0)),
            pl.BlockSpec((L, 2 * L), lambda j, b: (0, 0)),
        ],
        out_specs=pl.BlockSpec((pack * L, ct), lambda j, b: (b, j)),
        out_shape=jax.ShapeDtypeStruct((grp.rows, d), BF16),
        compiler_params=_params(("arbitrary", "arbitrary")),
        name=f"hyena_longconv_{L}",
    )(proj, proj, proj, kr, kiz, krn, skip, fwd, inv)


def _proj_res_kernel(*refs, has_bias, cached):
    refs = list(refs)
    a_ref, w_ref = refs[:2]
    refs = refs[2:]
    b_ref = refs.pop(0) if has_bias else None
    x_ref, gate_ref, o_ref = refs[:3]
    if cached:
        wc_scr = refs[3]
        j = pl.program_id(1)

        @pl.when(pl.program_id(0) == 0)
        def _():
            wc_scr[j] = w_ref[...].astype(BF16)

        w = wc_scr[j]
    else:
        w = w_ref[...]
    m = _bdot(a_ref[...], w)
    if has_bias:
        m = m + b_ref[...]
    o_ref[...] = x_ref[...] + gate_ref[0] * m


def _proj_res(grp, a, w, b, x, mod3, layer):
    k, n = w.shape
    cached = w.dtype != BF16
    assert not cached or k * n * 2 <= RES_WEIGHT_CACHE_BYTES
    tn = RES_COL_TILE
    nj = n // tn
    if cached:
        w_index = lambda i, j: (0, jnp.where(i == 0, j, nj - 1))
    else:
        w_index = lambda i, j: (0, j)
    in_specs = [
        pl.BlockSpec((TOKEN_TILE, k), lambda i, j: (i, 0)),
        pl.BlockSpec((k, tn), w_index),
    ]
    args = [a, w]
    if b is not None:
        in_specs.append(pl.BlockSpec((1, tn), lambda i, j: (0, j)))
        args.append(b.reshape(1, n))
    in_specs += [
        pl.BlockSpec((TOKEN_TILE, tn), lambda i, j: (i, j)),
        grp.mod_spec(layer, 2, tn, True),
    ]
    args += [x, mod3]
    return pl.pallas_call(
        functools.partial(_proj_res_kernel, has_bias=b is not None, cached=cached),
        grid=(grp.tiles, nj),
        in_specs=in_specs,
        out_specs=pl.BlockSpec((TOKEN_TILE, tn), lambda i, j: (i, j)),
        out_shape=jax.ShapeDtypeStruct((grp.rows, n), F32),
        scratch_shapes=[pltpu.VMEM((nj, k, tn), BF16)] if cached else [],
        compiler_params=_params(("arbitrary", "arbitrary")),
        name=f"proj_res_{k}_{grp.seq_len}",
    )(*args)


def _mlp_kernel(*refs, final_norm):
    if final_norm:
        (x_hbm, g_ref, sc_ref, sh_ref, gate_ref, w1_ref, w2_ref, fg_ref, o_ref,
         h_scr, x_buf, x_sem) = refs
    else:
        x_hbm, g_ref, sc_ref, sh_ref, gate_ref, w1_ref, w2_ref, o_ref, h_scr, x_buf, x_sem = refs
    i = pl.program_id(0)
    f = pl.program_id(1)
    last_f = pl.num_programs(1) - 1
    tile_rows = x_buf.shape[0]

    def x_copy(tile):
        start = pl.multiple_of(tile * tile_rows, tile_rows)
        return pltpu.make_async_copy(x_hbm.at[pl.ds(start, tile_rows), :], x_buf, x_sem)

    @pl.when(jnp.logical_and(i == 0, f == 0))
    def _():
        x_copy(0).start()

    @pl.when(f == 0)
    def _():
        x_copy(i).wait()

    @pl.when(jnp.logical_and(f == 1, i + 1 < pl.num_programs(0)))
    def _():
        x_copy(i + 1).start()

    def step(first, last):
        w1 = w1_ref[0]
        w2 = w2_ref[0]
        for r in range(tile_rows // MLP_ROWS):
            rows = slice(r * MLP_ROWS, (r + 1) * MLP_ROWS)
            if first:
                x = x_buf[rows, :]
                h = _modulated_norm(x, g_ref[...], sc_ref[0], sh_ref[0]).astype(BF16)
                h_scr[rows, :] = h
            else:
                h = h_scr[rows, :]
            u = jnp.square(jnp.maximum(_bdot(h, w1), 0.0)).astype(BF16)
            acc = (x if first else o_ref[rows, :]) + gate_ref[0] * _bdot(u, w2)
            if last and final_norm:
                ms = jnp.mean(acc * acc, axis=-1, keepdims=True)
                acc = acc * lax.rsqrt(ms + EPS) * fg_ref[...]
            o_ref[rows, :] = acc

    pl.when(f == 0)(lambda: step(True, False))
    pl.when(jnp.logical_and(f > 0, f < last_f))(lambda: step(False, False))
    pl.when(f == last_f)(lambda: step(False, True))


def _mlp(grp, x, norm_g, mod3, layer, w1, w2, final_g=None):
    d = grp.d
    ff = w1.shape[2]
    nf = ff // FF_TILE
    assert nf >= 2
    block_specs = [
        pl.BlockSpec((1, d), lambda i, f: (0, 0)),
        grp.mod_spec(layer, 4, d, False),
        grp.mod_spec(layer, 3, d, False),
        grp.mod_spec(layer, 5, d, False),
        pl.BlockSpec((1, d, FF_TILE), lambda i, f: (layer, 0, f)),
        pl.BlockSpec((1, FF_TILE, d), lambda i, f: (layer, f, 0)),
    ]
    args = [x, norm_g.reshape(1, d), mod3, mod3, mod3, w1, w2]
    if final_g is not None:
        block_specs.append(pl.BlockSpec((1, d), lambda i, f: (0, 0)))
        args.append(final_g.reshape(1, d))
    out_spec = pl.BlockSpec((TOKEN_TILE, d), lambda i, f: (i, 0))
    step = functools.partial(_mlp_kernel, final_norm=final_g is not None)

    def call(x_hbm, *refs):
        operands, (h_scr, x_buf, x_sem) = refs[:-3], refs[-3:]
        pltpu.emit_pipeline(
            functools.partial(step, x_hbm),
            grid=(grp.tiles, nf),
            in_specs=block_specs,
            out_specs=[out_spec],
        )(*operands, scratches=(h_scr, x_buf, x_sem))

    return pl.pallas_call(
        call,
        in_specs=[pl.BlockSpec(memory_space=pl.ANY)] * len(args),
        out_specs=pl.BlockSpec(memory_space=pl.ANY),
        out_shape=jax.ShapeDtypeStruct((grp.rows, d), F32),
        scratch_shapes=[pltpu.VMEM((TOKEN_TILE, d), BF16),
                        pltpu.VMEM((TOKEN_TILE, d), F32),
                        pltpu.SemaphoreType.DMA(())],
        compiler_params=pltpu.CompilerParams(vmem_limit_bytes=VMEM_LIMIT),
        name=f"mlp_{layer}_{grp.seq_len}",
    )(*args)


def _rotary_tables(L, dk, width):
    n_rows = L // GRID_W
    rows = np.repeat(np.arange(n_rows, dtype=np.float64), GRID_W)
    cols = np.tile(np.arange(GRID_W, dtype=np.float64), n_rows)
    n_pairs = dk // 4
    inv = ROPE_BASE ** (-np.arange(n_pairs, dtype=np.float64) / n_pairs)
    ang = np.concatenate([rows[:, None] * inv, cols[:, None] * inv], axis=-1)
    cos = np.repeat(np.cos(ang), 2, axis=-1)
    sin = np.repeat(np.sin(ang), 2, axis=-1)
    sin[:, 0::2] *= -1.0
    reps = width // dk
    return (jnp.asarray(np.tile(cos, (1, reps)), F32), jnp.asarray(np.tile(sin, (1, reps)), F32))


def _qkvg_kernel(*refs, rotary, qk_tiles, k_tile0, k_scale):
    if rotary:
        x_ref, g_ref, sc_ref, sh_ref, w_ref, cos_ref, sin_ref, o_ref, h_scr = refs
    else:
        x_ref, g_ref, sc_ref, sh_ref, w_ref, o_ref, h_scr = refs
    j = pl.program_id(1)
    is_qk = j < qk_tiles
    scale = jnp.where(jnp.logical_and(j >= k_tile0, is_qk), k_scale, 1.0)

    def project(fill, rotate):
        if rotate:
            lane = lax.broadcasted_iota(jnp.int32, (h_scr.shape[0], SUB_COLS), 1)
            even = (lane & 1) == 0
        for s in range(o_ref.shape[1] // SUB_COLS):
            cols = slice(s * SUB_COLS, (s + 1) * SUB_COLS)
            acc = _normed_dot(fill and s == 0, x_ref, g_ref, sc_ref, sh_ref, h_scr,
                              w_ref[:, cols].astype(BF16))
            if rotate:
                partner = jnp.where(even, pltpu.roll(acc, SUB_COLS - 1, 1), pltpu.roll(acc, 1, 1))
                acc = acc * cos_ref[...] + partner * sin_ref[...]
            o_ref[:, cols] = (acc * scale).astype(BF16)

    pl.when(j == 0)(lambda: project(True, rotary))
    if rotary:
        pl.when(jnp.logical_and(j > 0, is_qk))(lambda: project(False, True))
        pl.when(jnp.logical_not(is_qk))(lambda: project(False, False))
    else:
        pl.when(j > 0)(lambda: project(False, False))


def _qkvg(grp, x, norm_g, mod3, layer, w, dk, rotary):
    d = grp.d
    n = w.shape[1]
    in_specs = [
        pl.BlockSpec((TOKEN_TILE, d), lambda i, j: (i, 0)),
        pl.BlockSpec((1, d), lambda i, j: (0, 0)),
        grp.mod_spec(layer, 1, d, False),
        grp.mod_spec(layer, 0, d, False),
        pl.BlockSpec((d, COL_TILE), lambda i, j: (0, j)),
    ]
    args = [x, norm_g.reshape(1, d), mod3, mod3, w]
    if rotary:
        assert grp.seq_len == TOKEN_TILE
        cos, sin = _rotary_tables(grp.seq_len, dk, SUB_COLS)
        table = pl.BlockSpec((TOKEN_TILE, SUB_COLS), lambda i, j: (0, 0))
        in_specs += [table, table]
        args += [cos, sin]
    kern = functools.partial(_qkvg_kernel, rotary=rotary, qk_tiles=2 * d // COL_TILE,
                             k_tile0=d // COL_TILE, k_scale=dk ** -0.5)
    return pl.pallas_call(
        kern,
        grid=(grp.tiles, n // COL_TILE),
        in_specs=in_specs,
        out_specs=pl.BlockSpec((TOKEN_TILE, COL_TILE), lambda i, j: (i, j)),
        out_shape=jax.ShapeDtypeStruct((grp.rows, n), BF16),
        scratch_shapes=[pltpu.VMEM((TOKEN_TILE, d), BF16)],
        compiler_params=_params(("arbitrary", "arbitrary")),
        name=f"ret_qkvg_{grp.seq_len}",
    )(*args)


def _log_sigmoid(x):
    return jnp.minimum(x, 0.0) - jnp.log1p(jnp.exp(-jnp.abs(x)))


def _retention_kernel(*refs, L, C, pack, has_s0, emit_state):
    refs = list(refs)
    decay_ref, q_ref, k_ref, v_ref, g_ref, gn_ref = refs[:6]
    refs = refs[6:]
    s0_ref = refs.pop(0) if has_s0 else None
    y_ref = refs.pop(0)
    so_ref = refs.pop(0) if emit_state else None
    mask_scr, dec_scr, cdec_scr, sb_scr = refs

    head = pl.program_id(0)
    dk = q_ref.shape[1]
    dv = v_ref.shape[1]
    nc = L // C

    @pl.when(pl.program_id(1) == 0)
    def _():
        def log_decay(direction, width):
            return _log_sigmoid(jnp.full((1, width), decay_ref[direction, head], F32))

        ri = lax.broadcasted_iota(jnp.int32, (C, C), 0)
        ci = lax.broadcasted_iota(jnp.int32, (C, C), 1)
        diff = (ri - ci).astype(F32)
        lower = diff >= 0
        upper = diff <= 0
        mask_scr[...] = (
            jnp.where(lower, jnp.exp(jnp.where(lower, diff, 0.0) * log_decay(0, C)), 0.0)
            + jnp.where(upper, jnp.exp(jnp.where(upper, -diff, 0.0) * log_decay(1, C)), 0.0))
        rk = lax.broadcasted_iota(jnp.int32, (C, dk), 0).astype(F32)
        dec_scr[0] = jnp.exp((rk + 1.0) * log_decay(0, dk))
        dec_scr[1] = jnp.exp((C - 1.0 - rk) * log_decay(0, dk))
        dec_scr[2] = jnp.exp((C - rk) * log_decay(1, dk))
        dec_scr[3] = jnp.exp(rk * log_decay(1, dk))
        cdec_scr[0:1, :] = jnp.exp(C * log_decay(0, dv))
        cdec_scr[1:2, :] = jnp.exp(C * log_decay(1, dv))

    gn = gn_ref[...]

    def advance(state, dec_row, upd):
        return upd if state is None else state * cdec_scr[dec_row:dec_row + 1, :] + upd

    for s in range(pack):
        base = s * L

        def kv(n, which):
            rows = slice(base + n * C, base + (n + 1) * C)
            kd = (k_ref[rows, :].astype(F32) * dec_scr[which]).astype(BF16)
            return lax.dot_general(kd, v_ref[rows, :], (((0,), (0,)), ((), ())),
                                   preferred_element_type=F32)

        state_b = s0_ref[s, 0, 1, 0] if has_s0 else None
        has_b = [False] * nc
        for n in reversed(range(nc)):
            if state_b is not None:
                sb_scr[n] = state_b.astype(BF16)
                has_b[n] = True
            if n > 0 or emit_state:
                state_b = advance(state_b, 1, kv(n, 3))
        if emit_state:
            so_ref[s, 0, 1, 0] = state_b

        state_f = s0_ref[s, 0, 0, 0] if has_s0 else None
        for n in range(nc):
            rows = slice(base + n * C, base + (n + 1) * C)
            q = q_ref[rows, :]
            scores = lax.dot_general(q, k_ref[rows, :], (((1,), (1,)), ((), ())),
                                     preferred_element_type=F32)
            y = _bdot((scores * mask_scr[...]).astype(BF16), v_ref[rows, :])
            qf = q.astype(F32)
            if state_f is not None:
                y = y + _bdot((qf * dec_scr[0]).astype(BF16), state_f.astype(BF16))
            if has_b[n]:
                y = y + _bdot((qf * dec_scr[2]).astype(BF16), sb_scr[n])
            mu = jnp.mean(y, axis=-1, keepdims=True)
            yc = y - mu
            var = jnp.mean(yc * yc, axis=-1, keepdims=True)
            yn = (yc * lax.rsqrt(var + EPS)) * gn
            gate = g_ref[rows, :].astype(F32)
            y_ref[rows, :] = ((gate * jax.nn.sigmoid(gate)) * yn).astype(BF16)
            if n < nc - 1 or emit_state:
                state_f = advance(state_f, 0, kv(n, 1))
        if emit_state:
            so_ref[s, 0, 0, 0] = state_f


def _retention(grp, qkvg, decay, gn_g, s0, heads, dk, dv, emit_state):
    L = grp.seq_len
    C = min(L, RET_CHUNK)
    pack = max(1, min(grp.n_seq, RET_STEP_ROWS // L))
    assert grp.n_seq % pack == 0
    kq = heads
    kv0 = 2 * heads * dk // dv
    kg0 = kv0 + heads
    rows = pack * L
    in_specs = [
        pl.BlockSpec(memory_space=pltpu.SMEM),
        pl.BlockSpec((rows, dk), lambda h, b: (b, h)),
        pl.BlockSpec((rows, dk), lambda h, b: (b, kq + h)),
        pl.BlockSpec((rows, dv), lambda h, b: (b, kv0 + h)),
        pl.BlockSpec((rows, dv), lambda h, b: (b, kg0 + h)),
        pl.BlockSpec((1, dv), lambda h, b: (0, h)),
    ]
    args = [decay, qkvg, qkvg, qkvg, qkvg, gn_g.reshape(1, heads * dv)]
    state_block = (pack, 1, 2, 1, dk, dv)
    state_index = lambda h, b: (b, 0, 0, h, 0, 0)
    if s0 is not None:
        in_specs.append(pl.BlockSpec(state_block, state_index))
        args.append(s0)
    out_specs = [pl.BlockSpec((rows, dv), lambda h, b: (b, h))]
    out_shape = [jax.ShapeDtypeStruct((grp.rows, heads * dv), BF16)]
    if emit_state:
        out_specs.append(pl.BlockSpec(state_block, state_index))
        out_shape.append(jax.ShapeDtypeStruct((grp.n_seq, 1, 2, heads, dk, dv), F32))
    kern = functools.partial(_retention_kernel, L=L, C=C, pack=pack, has_s0=s0 is not None,
                             emit_state=emit_state)
    return pl.pallas_call(
        kern,
        grid=(heads, grp.n_seq // pack),
        in_specs=in_specs,
        out_specs=out_specs,
        out_shape=out_shape,
        scratch_shapes=[
            pltpu.VMEM((C, C), F32),
            pltpu.VMEM((4, C, dk), F32),
            pltpu.VMEM((8, dv), F32),
            pltpu.VMEM((L // C, dk, dv), BF16),
        ],
        compiler_params=_params(("arbitrary", "arbitrary")),
        name=f"retention_{L}",
    )(*args)


def kernel(x_prompt, x_sample, state_ret, c, c_ctx, w_ada, b_ada, norm_g, final_g, hy_w_in, hy_b_in, hy_conv_w, hy_conv_b, hy_f_w1, hy_f_b1, hy_f_w2, hy_f_b2, hy_f_w3, hy_f_b3, hy_f_freq, hy_f_wout, hy_f_skip, hy_w_out, hy_b_out, ret_w_qkvg, ret_decay, ret_gn_g, ret_w_o, mlp_w1, mlp_w2):
    n_prompt, prompt_len, d = x_prompt.shape
    n_sample, sample_len, _ = x_sample.shape
    heads, dk, dv = state_ret.shape[3:]
    depth = w_ada.shape[0]
    assert depth == 2 and hy_w_in.shape[0] == 1 and ret_w_qkvg.shape[0] == 1
    assert n_sample + 1 <= MOD_ROWS

    cond = jnp.concatenate(
        [c_ctx[None, :], c, jnp.zeros((MOD_ROWS - 1 - n_sample, d), F32)], axis=0)
    mod = _ada_mod(cond, w_ada, b_ada)
    mod3 = mod.reshape(depth * MOD_ROWS, 1, mod.shape[-1])

    w_in, w_out, w_qkvg = hy_w_in[0], hy_w_out[0], ret_w_qkvg[0]
    w_o = ret_w_o[0].astype(BF16)
    w1, w2 = mlp_w1.astype(BF16), mlp_w2.astype(BF16)

    def run(grp, x, s0, latent):
        L = grp.seq_len
        proj = _hyena_in(grp, x, norm_g[0, 0], mod3, 0, w_in, hy_b_in[0], hy_conv_w[0],
                         hy_conv_b[0])
        fwd, inv = _dft_matrices(L)
        spectra = _filter_spectra(L, d, fwd, hy_f_w1[0], hy_f_b1[0], hy_f_w2[0], hy_f_b2[0],
                                  hy_f_w3[0], hy_f_b3[0], hy_f_freq[0], hy_f_wout[0])
        z = _longconv(grp, proj, spectra, hy_f_skip[0], fwd, inv)
        x = _proj_res(grp, z, w_out, hy_b_out[0], x, mod3, 0)
        x = _mlp(grp, x, norm_g[0, 1], mod3, 0, w1, w2)
        qkvg = _qkvg(grp, x, norm_g[1, 0], mod3, 1, w_qkvg, dk, rotary=latent)
        outs = _retention(grp, qkvg, ret_decay[0], ret_gn_g[0], s0, heads, dk, dv,
                          emit_state=s0 is None)
        x = _proj_res(grp, outs[0], w_o, None, x, mod3, 1)
        y = _mlp(grp, x, norm_g[1, 1], mod3, 1, w1, w2, final_g=final_g)
        return y, (outs[1] if s0 is None else None)

    prompt = _Group(n_prompt, prompt_len, d, mod_row0=0, mod_row_step=0)
    sample = _Group(n_sample, sample_len, d, mod_row0=1, mod_row_step=1)
    y_prompt, new_state = run(prompt, x_prompt.reshape(-1, d), None, False)
    y_sample, _ = run(sample, x_sample.reshape(-1, d), state_ret, True)
    return (y_prompt.reshape(x_prompt.shape), y_sample.reshape(x_sample.shape),
            new_state.astype(x_prompt.dtype))
```

```python
import functools
import math

import numpy as np
import jax
import jax.numpy as jnp
from jax import lax
from jax.experimental import pallas as pl
from jax.experimental.pallas import tpu as pltpu

F32 = jnp.float32
BF16 = jnp.bfloat16

EPS = 1e-6
GRID_W = 64
ROPE_BASE = 10000.0
FILTER_BANDS = 16
DECAY_TARGET = 1e-2
MIN_DECAY = math.log(DECAY_TARGET) / 1.5
MAX_DECAY = math.log(DECAY_TARGET) / 0.3

V7X_VMEM_BYTES = 64 * 1024 * 1024
VMEM_LIMIT = V7X_VMEM_BYTES - 8 * 1024 * 1024

TOKEN_TILE = 1024
COL_TILE = 1024
SUB_COLS = 512
RES_COL_TILE = 1024
RES_WEIGHT_CACHE_BYTES = 8 * 1024 * 1024
FF_TILE = 1024
MLP_ROWS = 256
ADA_COL_TILE = 2048
CONV_TILE_ELEMS = 256 * 1024
CONV_SHORT_SEQ = 256
CONV_MIN_LANES = 512
FILT_CH_TILE = 1024
RET_CHUNK = 256
RET_STEP_ROWS = 2048
MOD_ROWS = 8


def _params(semantics):
    return pltpu.CompilerParams(dimension_semantics=semantics, vmem_limit_bytes=VMEM_LIMIT)


def _pipelined_call(kernel, *, grid, in_specs, out_specs, out_shape, direct_specs=(),
                    scratch_shapes=(), name):
    single = not isinstance(out_specs, (list, tuple))
    outs = [out_specs] if single else list(out_specs)
    n_direct, n_piped = len(direct_specs), len(in_specs) + len(outs)
    any_spec = pl.BlockSpec(memory_space=pl.ANY)

    def call(*refs):
        direct, piped, scratch = (refs[:n_direct], refs[n_direct:n_direct + n_piped],
                                  refs[n_direct + n_piped:])
        pltpu.emit_pipeline(functools.partial(kernel, *direct), grid=grid,
                            in_specs=list(in_specs), out_specs=outs)(*piped, scratches=scratch)

    return pl.pallas_call(
        call,
        in_specs=list(direct_specs) + [any_spec] * len(in_specs),
        out_specs=any_spec if single else [any_spec] * len(outs),
        out_shape=out_shape,
        scratch_shapes=list(scratch_shapes),
        compiler_params=pltpu.CompilerParams(vmem_limit_bytes=VMEM_LIMIT),
        name=name,
    )


def _bdot(a, b):
    return jnp.dot(a, b, preferred_element_type=F32)


def _fdot(a, b):
    return jnp.dot(a, b, preferred_element_type=F32, precision=lax.Precision.HIGHEST)


def _modulated_norm(x, g, scale, shift):
    ms = jnp.mean(x * x, axis=-1, keepdims=True)
    return (x * lax.rsqrt(ms + EPS) * g) * (1.0 + scale) + shift


def _normed_dot(fill, x_ref, g_ref, sc_ref, sh_ref, h_scr, w):
    if not fill:
        return _bdot(h_scr[...], w)
    parts = []
    for r in range(x_ref.shape[0] // MLP_ROWS):
        rows = slice(r * MLP_ROWS, (r + 1) * MLP_ROWS)
        h = _modulated_norm(x_ref[rows, :], g_ref[...], sc_ref[0], sh_ref[0]).astype(BF16)
        h_scr[rows, :] = h
        parts.append(_bdot(h, w))
    return jnp.concatenate(parts, axis=0)


class _Group:
    def __init__(self, n_seq, seq_len, d, mod_row0, mod_row_step):
        assert seq_len & (seq_len - 1) == 0
        assert (n_seq * seq_len) % TOKEN_TILE == 0
        assert TOKEN_TILE % seq_len == 0 or (mod_row_step == 0 and seq_len % TOKEN_TILE == 0)
        assert mod_row_step == 0 or seq_len == TOKEN_TILE
        self.n_seq, self.seq_len, self.d = n_seq, seq_len, d
        self.rows = n_seq * seq_len
        self.tiles = self.rows // TOKEN_TILE
        self.mod_row0, self.mod_row_step = mod_row0, mod_row_step

    def mod_spec(self, layer, chunk, width, tiled):
        per = self.d // width
        base = layer * MOD_ROWS + self.mod_row0
        step = self.mod_row_step

        def index(i, j):
            return (base + step * i, 0, chunk * per + (j if tiled else 0))

        return pl.BlockSpec((1, 1, width), index)


def _ada_kernel(cond_ref, w_ref, b_ref, o_ref):
    c = cond_ref[...]
    a = (c * jax.nn.sigmoid(c)).astype(BF16)
    o_ref[0] = _bdot(a, w_ref[0].astype(BF16)) + b_ref[0]


def _ada_mod(cond, w_ada, b_ada):
    depth, d, n = w_ada.shape
    tn = ADA_COL_TILE
    return pl.pallas_call(
        _ada_kernel,
        grid=(depth, n // tn),
        in_specs=[
            pl.BlockSpec((MOD_ROWS, d), lambda l, j: (0, 0)),
            pl.BlockSpec((1, d, tn), lambda l, j: (l, 0, j)),
            pl.BlockSpec((1, 1, tn), lambda l, j: (l, 0, j)),
        ],
        out_specs=pl.BlockSpec((1, MOD_ROWS, tn), lambda l, j: (l, 0, j)),
        out_shape=jax.ShapeDtypeStruct((depth, MOD_ROWS, n), F32),
        compiler_params=_params(("arbitrary", "arbitrary")),
        name="ada_mod",
    )(cond, w_ada, b_ada.reshape(depth, 1, n))


def _hyena_in_kernel(x_ref, g_ref, sc_ref, sh_ref, w_ref, b_ref, cw_ref, cb_ref, o_ref, h_scr,
                     *, seq_len):
    def project(fill):
        rows = h_scr.shape[0]
        pos = lax.broadcasted_iota(jnp.int32, (rows, SUB_COLS), 0) & (seq_len - 1)
        first = pos == 0
        last = pos == seq_len - 1
        for s in range(o_ref.shape[1] // SUB_COLS):
            cols = slice(s * SUB_COLS, (s + 1) * SUB_COLS)
            p = _normed_dot(fill and s == 0, x_ref, g_ref, sc_ref, sh_ref, h_scr,
                            w_ref[:, cols].astype(BF16)) + b_ref[:, cols]
            prev = jnp.where(first, 0.0, pltpu.roll(p, 1, 0))
            nxt = jnp.where(last, 0.0, pltpu.roll(p, rows - 1, 0))
            o_ref[:, cols] = (prev * cw_ref[0:1, cols] + p * cw_ref[1:2, cols]
                              + nxt * cw_ref[2:3, cols] + cb_ref[:, cols])

    j = pl.program_id(1)
    pl.when(j == 0)(lambda: project(True))
    pl.when(j > 0)(lambda: project(False))


def _hyena_in(grp, x, norm_g, mod3, layer, w, b, conv_w, conv_b):
    d = grp.d
    n = w.shape[1]
    return _pipelined_call(
        functools.partial(_hyena_in_kernel, seq_len=grp.seq_len),
        grid=(grp.tiles, n // COL_TILE),
        in_specs=[
            pl.BlockSpec((TOKEN_TILE, d), lambda i, j: (i, 0)),
            pl.BlockSpec((1, d), lambda i, j: (0, 0)),
            grp.mod_spec(layer, 1, d, False),
            grp.mod_spec(layer, 0, d, False),
            pl.BlockSpec((d, COL_TILE), lambda i, j: (0, j)),
            pl.BlockSpec((1, COL_TILE), lambda i, j: (0, j)),
            pl.BlockSpec((3, COL_TILE), lambda i, j: (0, j)),
            pl.BlockSpec((1, COL_TILE), lambda i, j: (0, j)),
        ],
        out_specs=pl.BlockSpec((TOKEN_TILE, COL_TILE), lambda i, j: (i, j)),
        out_shape=jax.ShapeDtypeStruct((grp.rows, n), F32),
        scratch_shapes=[pltpu.VMEM((TOKEN_TILE, d), BF16)],
        name=f"hyena_in_{grp.seq_len}",
    )(x, norm_g.reshape(1, d), mod3, mod3, w, b.reshape(1, n), conv_w, conv_b.reshape(1, n))


def _dft_matrices(L):
    f = np.arange(L, dtype=np.int64)[:, None]
    t = np.arange(L, dtype=np.int64)[None, :]
    ang = ((f * t) % (2 * L)).astype(np.float64) * (np.pi / L)
    cos, sin = np.cos(ang), np.sin(ang)
    nyq = np.where(np.arange(L) % 2 == 0, 1.0, -1.0)
    bot = sin.copy()
    bot[0, :] = nyq
    fwd = np.concatenate([cos, bot], axis=0)
    wgt = np.full((L,), 2.0)
    wgt[0] = 1.0
    inv_cos = (cos * wgt[:, None]).T / (2 * L)
    inv_sin = (-2.0 * sin).T / (2 * L)
    inv_sin[:, 0] = nyq / (2 * L)
    inv = np.concatenate([inv_cos, inv_sin], axis=1)
    return jnp.asarray(fwd, F32).astype(BF16), jnp.asarray(inv, F32).astype(BF16)


def _filter_features(L, width):
    pos = np.arange(L, dtype=np.float64)
    t = pos / L
    omega = 2.0 * math.pi * pos / L
    bands = np.linspace(1e-4, FILTER_BANDS - 1, FILTER_BANDS)
    ang = omega[:, None] * bands[None, :]
    feat = np.concatenate([t[:, None], np.cos(ang), -np.sin(ang)], axis=-1)
    feat = np.pad(feat, ((0, 0), (0, width - feat.shape[1])))
    return jnp.asarray(feat, F32), jnp.asarray(t[:, None], F32)


def _filter_kernel(feat_ref, w1_ref, b1_ref, w2_ref, b2_ref, w3_ref, b3_ref, freq_ref,
                   wof_ref, wob_ref, t_ref, delta_ref, fwd_ref,
                   kr_ref, kiz_ref, krn_ref, h_scr, *, L):
    first = jnp.logical_and(pl.program_id(0) == 0, pl.program_id(1) == 0)

    @pl.when(first)
    def _():
        h = jnp.sin(freq_ref[0:1, :] * (_fdot(feat_ref[...], w1_ref[...]) + b1_ref[...]))
        h = jnp.sin(freq_ref[1:2, :] * (_fdot(h, w2_ref[...]) + b2_ref[...]))
        h = jnp.sin(freq_ref[2:3, :] * (_fdot(h, w3_ref[...]) + b3_ref[...]))
        h_scr[...] = h

    h = h_scr[...].astype(BF16)
    window = jnp.exp(-t_ref[...] * delta_ref[...])
    hf = _bdot(h, wof_ref[...].astype(BF16)) * window
    hb = _bdot(h, wob_ref[...].astype(BF16)) * window
    row = lax.broadcasted_iota(jnp.int32, hf.shape, 0)
    hb = jnp.where(row == 0, 0.0, hb)
    tap_sum = hf + hb
    tap_dif = hb - hf
    kr = _bdot(fwd_ref[0:L, :], tap_sum.astype(BF16))
    ki = _bdot(fwd_ref[L:2 * L, :], tap_dif.astype(BF16))
    sign = jnp.where((row & 1) == 0, 1.0, -1.0)
    k_nyq = jnp.sum(tap_sum * sign, axis=0, keepdims=True)
    kr_ref[0] = kr
    kiz_ref[0] = jnp.where(row == 0, 0.0, ki)
    krn_ref[0] = jnp.where(row == 0, -k_nyq, kr)


def _filter_spectra(L, d, fwd, w1, b1, w2, b2, w3, b3, freq, wout):
    width = w2.shape[0]
    order = wout.shape[1] // (2 * d)
    feat, t = _filter_features(L, width)
    w1p = jnp.pad(w1, ((0, width - w1.shape[0]), (0, 0)))
    deltas = jnp.asarray(np.abs(np.linspace(MIN_DECAY, MAX_DECAY, d))[None, :], F32)
    ct = FILT_CH_TILE
    per = d // ct
    small = lambda shape: pl.BlockSpec(shape, lambda o, j: (0, 0))
    out_spec = pl.BlockSpec((1, L, ct), lambda o, j: (o, 0, j))
    out_shape = jax.ShapeDtypeStruct((order, L, d), F32)
    return pl.pallas_call(
        functools.partial(_filter_kernel, L=L),
        grid=(order, per),
        in_specs=[
            small((L, width)), small((width, width)), small((1, width)),
            small((width, width)), small((1, width)),
            small((width, width)), small((1, width)), small((3, width)),
            pl.BlockSpec((width, ct), lambda o, j: (0, (2 * o) * per + j)),
            pl.BlockSpec((width, ct), lambda o, j: (0, (2 * o + 1) * per + j)),
            small((L, 1)),
            pl.BlockSpec((1, ct), lambda o, j: (0, j)),
            small((2 * L, L)),
        ],
        out_specs=[out_spec, out_spec, out_spec],
        out_shape=[out_shape, out_shape, out_shape],
        scratch_shapes=[pltpu.VMEM((L, width), F32)],
        compiler_params=_params(("arbitrary", "arbitrary")),
        name=f"hyena_filter_{L}",
    )(feat, w1p, b1.reshape(1, width), w2, b2.reshape(1, width), w3, b3.reshape(1, width),
      freq, wout, wout, t, deltas, fwd)


def _longconv_kernel(v_ref, x1_ref, x2_ref, kr_ref, kiz_ref, krn_ref, skip_ref, fwd_ref, inv_ref,
                     o_ref, *, L, pack):
    def lanes(ref):
        return jnp.concatenate([ref[s * L:(s + 1) * L, :] for s in range(pack)], axis=1)

    def tiled(x):
        return jnp.concatenate([x] * pack, axis=1)

    def conv(u, o):
        spec = _bdot(fwd_ref[...], u.astype(BF16))
        a, b = spec[:L], spec[L:]
        kr, kiz, krn = tiled(kr_ref[o]), tiled(kiz_ref[o]), tiled(krn_ref[o])
        top = (a * kr + b * kiz).astype(BF16)
        bot = (a * kiz - b * krn).astype(BF16)
        return _bdot(inv_ref[:, 0:L], top) + _bdot(inv_ref[:, L:2 * L], bot)

    v = lanes(v_ref)
    z = lanes(x1_ref) * (conv(v, 0) + v * tiled(skip_ref[0:1, :]))
    z = lanes(x2_ref) * (conv(z, 1) + z * tiled(skip_ref[1:2, :]))
    ct = o_ref.shape[1]
    for s in range(pack):
        o_ref[s * L:(s + 1) * L, :] = z[:, s * ct:(s + 1) * ct].astype(BF16)


def _longconv(grp, proj, spectra, skip, fwd, inv):
    L, d = grp.seq_len, grp.d
    ct = min(d, (2 if L <= CONV_SHORT_SEQ else 1) * CONV_TILE_ELEMS // L)
    pack = max(1, CONV_MIN_LANES // ct)
    assert grp.n_seq % pack == 0
    per = d // ct
    kr, kiz, krn = spectra
    order = kr.shape[0]
    kspec = pl.BlockSpec((order, L, ct), lambda j, b: (0, 0, j))
    return _pipelined_call(
        functools.partial(_longconv_kernel, L=L, pack=pack),
        grid=(per, grp.n_seq // pack),
        in_specs=[
            pl.BlockSpec((pack * L, ct), lambda j, b: (b, j)),
            pl.BlockSpec((pack * L, ct), lambda j, b: (b, per + j)),
            pl.BlockSpec((pack * L, ct), lambda j, b: (b, 2 * per + j)),
            kspec, kspec, kspec,
            pl.BlockSpec((order, ct), lambda j, b: (0, j)),
            pl.BlockSpec((2 * L, L), lambda j, b: (0, 0)),
            pl.BlockSpec((L, 2 * L), lambda j, b: (0, 0)),
        ],
        out_specs=pl.BlockSpec((pack * L, ct), lambda j, b: (b, j)),
        out_shape=jax.ShapeDtypeStruct((grp.rows, d), BF16),
        name=f"hyena_longconv_{L}",
    )(proj, proj, proj, kr, kiz, krn, skip, fwd, inv)


def _proj_res_kernel(*refs, has_bias, cached):
    refs = list(refs)
    a_ref, w_ref = refs[:2]
    refs = refs[2:]
    b_ref = refs.pop(0) if has_bias else None
    x_ref, gate_ref, o_ref = refs[:3]
    if cached:
        wc_scr = refs[3]
        j = pl.program_id(1)

        @pl.when(pl.program_id(0) == 0)
        def _():
            wc_scr[j] = w_ref[...].astype(BF16)

        w = wc_scr[j]
    else:
        w = w_ref[...]
    m = _bdot(a_ref[...], w)
    if has_bias:
        m = m + b_ref[...]
    o_ref[...] = x_ref[...] + gate_ref[0] * m


def _proj_res(grp, a, w, b, x, mod3, layer):
    k, n = w.shape
    cached = w.dtype != BF16
    assert not cached or k * n * 2 <= RES_WEIGHT_CACHE_BYTES
    tn = RES_COL_TILE
    nj = n // tn
    if cached:
        w_index = lambda i, j: (0, jnp.where(i == 0, j, nj - 1))
    else:
        w_index = lambda i, j: (0, j)
    in_specs = [
        pl.BlockSpec((TOKEN_TILE, k), lambda i, j: (i, 0)),
        pl.BlockSpec((k, tn), w_index),
    ]
    args = [a, w]
    if b is not None:
        in_specs.append(pl.BlockSpec((1, tn), lambda i, j: (0, j)))
        args.append(b.reshape(1, n))
    in_specs += [
        pl.BlockSpec((TOKEN_TILE, tn), lambda i, j: (i, j)),
        grp.mod_spec(layer, 2, tn, True),
    ]
    args += [x, mod3]
    return _pipelined_call(
        functools.partial(_proj_res_kernel, has_bias=b is not None, cached=cached),
        grid=(grp.tiles, nj),
        in_specs=in_specs,
        out_specs=pl.BlockSpec((TOKEN_TILE, tn), lambda i, j: (i, j)),
        out_shape=jax.ShapeDtypeStruct((grp.rows, n), F32),
        scratch_shapes=[pltpu.VMEM((nj, k, tn), BF16)] if cached else [],
        name=f"proj_res_{k}_{grp.seq_len}",
    )(*args)


def _mlp_kernel(*refs, final_norm):
    if final_norm:
        (x_hbm, g_ref, sc_ref, sh_ref, gate_ref, w1_ref, w2_ref, fg_ref, o_ref,
         h_scr, x_buf, x_sem) = refs
    else:
        x_hbm, g_ref, sc_ref, sh_ref, gate_ref, w1_ref, w2_ref, o_ref, h_scr, x_buf, x_sem = refs
    i = pl.program_id(0)
    f = pl.program_id(1)
    last_f = pl.num_programs(1) - 1
    tile_rows = x_buf.shape[0]

    def x_copy(tile):
        start = pl.multiple_of(tile * tile_rows, tile_rows)
        return pltpu.make_async_copy(x_hbm.at[pl.ds(start, tile_rows), :], x_buf, x_sem)

    @pl.when(jnp.logical_and(i == 0, f == 0))
    def _():
        x_copy(0).start()

    @pl.when(f == 0)
    def _():
        x_copy(i).wait()

    @pl.when(jnp.logical_and(f == 1, i + 1 < pl.num_programs(0)))
    def _():
        x_copy(i + 1).start()

    def step(first, last):
        w1 = w1_ref[0]
        w2 = w2_ref[0]
        for r in range(tile_rows // MLP_ROWS):
            rows = slice(r * MLP_ROWS, (r + 1) * MLP_ROWS)
            if first:
                x = x_buf[rows, :]
                h = _modulated_norm(x, g_ref[...], sc_ref[0], sh_ref[0]).astype(BF16)
                h_scr[rows, :] = h
            else:
                h = h_scr[rows, :]
            u = jnp.square(jnp.maximum(_bdot(h, w1), 0.0)).astype(BF16)
            acc = (x if first else o_ref[rows, :]) + gate_ref[0] * _bdot(u, w2)
            if last and final_norm:
                ms = jnp.mean(acc * acc, axis=-1, keepdims=True)
                acc = acc * lax.rsqrt(ms + EPS) * fg_ref[...]
            o_ref[rows, :] = acc

    pl.when(f == 0)(lambda: step(True, False))
    pl.when(jnp.logical_and(f > 0, f < last_f))(lambda: step(False, False))
    pl.when(f == last_f)(lambda: step(False, True))


def _mlp(grp, x, norm_g, mod3, layer, w1, w2, final_g=None):
    d = grp.d
    ff = w1.shape[2]
    nf = ff // FF_TILE
    assert nf >= 2
    block_specs = [
        pl.BlockSpec((1, d), lambda i, f: (0, 0)),
        grp.mod_spec(layer, 4, d, False),
        grp.mod_spec(layer, 3, d, False),
        grp.mod_spec(layer, 5, d, False),
        pl.BlockSpec((1, d, FF_TILE), lambda i, f: (layer, 0, f)),
        pl.BlockSpec((1, FF_TILE, d), lambda i, f: (layer, f, 0)),
    ]
    args = [x, norm_g.reshape(1, d), mod3, mod3, mod3, w1, w2]
    if final_g is not None:
        block_specs.append(pl.BlockSpec((1, d), lambda i, f: (0, 0)))
        args.append(final_g.reshape(1, d))
    return _pipelined_call(
        functools.partial(_mlp_kernel, final_norm=final_g is not None),
        grid=(grp.tiles, nf),
        direct_specs=[pl.BlockSpec(memory_space=pl.ANY)],
        in_specs=block_specs,
        out_specs=pl.BlockSpec((TOKEN_TILE, d), lambda i, f: (i, 0)),
        out_shape=jax.ShapeDtypeStruct((grp.rows, d), F32),
        scratch_shapes=[pltpu.VMEM((TOKEN_TILE, d), BF16),
                        pltpu.VMEM((TOKEN_TILE, d), F32),
                        pltpu.SemaphoreType.DMA(())],
        name=f"mlp_{layer}_{grp.seq_len}",
    )(*args)


def _rotary_tables(L, dk, width):
    n_rows = L // GRID_W
    rows = np.repeat(np.arange(n_rows, dtype=np.float64), GRID_W)
    cols = np.tile(np.arange(GRID_W, dtype=np.float64), n_rows)
    n_pairs = dk // 4
    inv = ROPE_BASE ** (-np.arange(n_pairs, dtype=np.float64) / n_pairs)
    ang = np.concatenate([rows[:, None] * inv, cols[:, None] * inv], axis=-1)
    cos = np.repeat(np.cos(ang), 2, axis=-1)
    sin = np.repeat(np.sin(ang), 2, axis=-1)
    sin[:, 0::2] *= -1.0
    reps = width // dk
    return (jnp.asarray(np.tile(cos, (1, reps)), F32), jnp.asarray(np.tile(sin, (1, reps)), F32))


def _qkvg_kernel(*refs, rotary, qk_tiles, k_tile0, k_scale):
    if rotary:
        x_ref, g_ref, sc_ref, sh_ref, w_ref, cos_ref, sin_ref, o_ref, h_scr = refs
    else:
        x_ref, g_ref, sc_ref, sh_ref, w_ref, o_ref, h_scr = refs
    j = pl.program_id(1)
    is_qk = j < qk_tiles
    scale = jnp.where(jnp.logical_and(j >= k_tile0, is_qk), k_scale, 1.0)

    def project(fill, rotate):
        if rotate:
            lane = lax.broadcasted_iota(jnp.int32, (h_scr.shape[0], SUB_COLS), 1)
            even = (lane & 1) == 0
        for s in range(o_ref.shape[1] // SUB_COLS):
            cols = slice(s * SUB_COLS, (s + 1) * SUB_COLS)
            acc = _normed_dot(fill and s == 0, x_ref, g_ref, sc_ref, sh_ref, h_scr,
                              w_ref[:, cols].astype(BF16))
            if rotate:
                partner = jnp.where(even, pltpu.roll(acc, SUB_COLS - 1, 1), pltpu.roll(acc, 1, 1))
                acc = acc * cos_ref[...] + partner * sin_ref[...]
            o_ref[:, cols] = (acc * scale).astype(BF16)

    pl.when(j == 0)(lambda: project(True, rotary))
    if rotary:
        pl.when(jnp.logical_and(j > 0, is_qk))(lambda: project(False, True))
        pl.when(jnp.logical_not(is_qk))(lambda: project(False, False))
    else:
        pl.when(j > 0)(lambda: project(False, False))


def _qkvg(grp, x, norm_g, mod3, layer, w, dk, rotary):
    d = grp.d
    n = w.shape[1]
    in_specs = [
        pl.BlockSpec((TOKEN_TILE, d), lambda i, j: (i, 0)),
        pl.BlockSpec((1, d), lambda i, j: (0, 0)),
        grp.mod_spec(layer, 1, d, False),
        grp.mod_spec(layer, 0, d, False),
        pl.BlockSpec((d, COL_TILE), lambda i, j: (0, j)),
    ]
    args = [x, norm_g.reshape(1, d), mod3, mod3, w]
    if rotary:
        assert grp.seq_len == TOKEN_TILE
        cos, sin = _rotary_tables(grp.seq_len, dk, SUB_COLS)
        table = pl.BlockSpec((TOKEN_TILE, SUB_COLS), lambda i, j: (0, 0))
        in_specs += [table, table]
        args += [cos, sin]
    kern = functools.partial(_qkvg_kernel, rotary=rotary, qk_tiles=2 * d // COL_TILE,
                             k_tile0=d // COL_TILE, k_scale=dk ** -0.5)
    return _pipelined_call(
        kern,
        grid=(grp.tiles, n // COL_TILE),
        in_specs=in_specs,
        out_specs=pl.BlockSpec((TOKEN_TILE, COL_TILE), lambda i, j: (i, j)),
        out_shape=jax.ShapeDtypeStruct((grp.rows, n), BF16),
        scratch_shapes=[pltpu.VMEM((TOKEN_TILE, d), BF16)],
        name=f"ret_qkvg_{grp.seq_len}",
    )(*args)


def _log_sigmoid(x):
    return jnp.minimum(x, 0.0) - jnp.log1p(jnp.exp(-jnp.abs(x)))


def _retention_kernel(*refs, L, C, pack, has_s0, emit_state):
    refs = list(refs)
    decay_ref, q_ref, k_ref, v_ref, g_ref, gn_ref = refs[:6]
    refs = refs[6:]
    s0_ref = refs.pop(0) if has_s0 else None
    y_ref = refs.pop(0)
    so_ref = refs.pop(0) if emit_state else None
    mask_scr, dec_scr, cdec_scr, sb_scr = refs

    head = pl.program_id(0)
    dk = q_ref.shape[1]
    dv = v_ref.shape[1]
    nc = L // C

    @pl.when(pl.program_id(1) == 0)
    def _():
        def log_decay(direction, width):
            return _log_sigmoid(jnp.full((1, width), decay_ref[direction, head], F32))

        ri = lax.broadcasted_iota(jnp.int32, (C, C), 0)
        ci = lax.broadcasted_iota(jnp.int32, (C, C), 1)
        diff = (ri - ci).astype(F32)
        lower = diff >= 0
        upper = diff <= 0
        mask_scr[...] = (
            jnp.where(lower, jnp.exp(jnp.where(lower, diff, 0.0) * log_decay(0, C)), 0.0)
            + jnp.where(upper, jnp.exp(jnp.where(upper, -diff, 0.0) * log_decay(1, C)), 0.0))
        rk = lax.broadcasted_iota(jnp.int32, (C, dk), 0).astype(F32)
        dec_scr[0] = jnp.exp((rk + 1.0) * log_decay(0, dk))
        dec_scr[1] = jnp.exp((C - 1.0 - rk) * log_decay(0, dk))
        dec_scr[2] = jnp.exp((C - rk) * log_decay(1, dk))
        dec_scr[3] = jnp.exp(rk * log_decay(1, dk))
        cdec_scr[0:1, :] = jnp.exp(C * log_decay(0, dv))
        cdec_scr[1:2, :] = jnp.exp(C * log_decay(1, dv))

    gn = gn_ref[...]

    def advance(state, dec_row, upd):
        return upd if state is None else state * cdec_scr[dec_row:dec_row + 1, :] + upd

    for s in range(pack):
        base = s * L

        def kv(n, which):
            rows = slice(base + n * C, base + (n + 1) * C)
            kd = (k_ref[rows, :].astype(F32) * dec_scr[which]).astype(BF16)
            return lax.dot_general(kd, v_ref[rows, :], (((0,), (0,)), ((), ())),
                                   preferred_element_type=F32)

        state_b = s0_ref[s, 0, 1, 0] if has_s0 else None
        has_b = [False] * nc
        for n in reversed(range(nc)):
            if state_b is not None:
                sb_scr[n] = state_b.astype(BF16)
                has_b[n] = True
            if n > 0 or emit_state:
                state_b = advance(state_b, 1, kv(n, 3))
        if emit_state:
            so_ref[s, 0, 1, 0] = state_b

        state_f = s0_ref[s, 0, 0, 0] if has_s0 else None
        for n in range(nc):
            rows = slice(base + n * C, base + (n + 1) * C)
            q = q_ref[rows, :]
            scores = lax.dot_general(q, k_ref[rows, :], (((1,), (1,)), ((), ())),
                                     preferred_element_type=F32)
            y = _bdot((scores * mask_scr[...]).astype(BF16), v_ref[rows, :])
            qf = q.astype(F32)
            if state_f is not None:
                y = y + _bdot((qf * dec_scr[0]).astype(BF16), state_f.astype(BF16))
            if has_b[n]:
                y = y + _bdot((qf * dec_scr[2]).astype(BF16), sb_scr[n])
            mu = jnp.mean(y, axis=-1, keepdims=True)
            yc = y - mu
            var = jnp.mean(yc * yc, axis=-1, keepdims=True)
            yn = (yc * lax.rsqrt(var + EPS)) * gn
            gate = g_ref[rows, :].astype(F32)
            y_ref[rows, :] = ((gate * jax.nn.sigmoid(gate)) * yn).astype(BF16)
            if n < nc - 1 or emit_state:
                state_f = advance(state_f, 0, kv(n, 1))
        if emit_state:
            so_ref[s, 0, 0, 0] = state_f


def _retention(grp, qkvg, decay, gn_g, s0, heads, dk, dv, emit_state):
    L = grp.seq_len
    C = min(L, RET_CHUNK)
    pack = max(1, min(grp.n_seq, RET_STEP_ROWS // L))
    assert grp.n_seq % pack == 0
    kq = heads
    kv0 = 2 * heads * dk // dv
    kg0 = kv0 + heads
    rows = pack * L
    in_specs = [
        pl.BlockSpec((rows, dk), lambda h, b: (b, h)),
        pl.BlockSpec((rows, dk), lambda h, b: (b, kq + h)),
        pl.BlockSpec((rows, dv), lambda h, b: (b, kv0 + h)),
        pl.BlockSpec((rows, dv), lambda h, b: (b, kg0 + h)),
        pl.BlockSpec((1, dv), lambda h, b: (0, h)),
    ]
    args = [decay, qkvg, qkvg, qkvg, qkvg, gn_g.reshape(1, heads * dv)]
    state_block = (pack, 1, 2, 1, dk, dv)
    state_index = lambda h, b: (b, 0, 0, h, 0, 0)
    if s0 is not None:
        in_specs.append(pl.BlockSpec(state_block, state_index))
        args.append(s0)
    out_specs = [pl.BlockSpec((rows, dv), lambda h, b: (b, h))]
    out_shape = [jax.ShapeDtypeStruct((grp.rows, heads * dv), BF16)]
    if emit_state:
        out_specs.append(pl.BlockSpec(state_block, state_index))
        out_shape.append(jax.ShapeDtypeStruct((grp.n_seq, 1, 2, heads, dk, dv), F32))
    kern = functools.partial(_retention_kernel, L=L, C=C, pack=pack, has_s0=s0 is not None,
                             emit_state=emit_state)
    return _pipelined_call(
        kern,
        grid=(heads, grp.n_seq // pack),
        direct_specs=[pl.BlockSpec(memory_space=pltpu.SMEM)],
        in_specs=in_specs,
        out_specs=out_specs,
        out_shape=out_shape,
        scratch_shapes=[
            pltpu.VMEM((C, C), F32),
            pltpu.VMEM((4, C, dk), F32),
            pltpu.VMEM((8, dv), F32),
            pltpu.VMEM((L // C, dk, dv), BF16),
        ],
        name=f"retention_{L}",
    )(*args)


def kernel(x_prompt, x_sample, state_ret, c, c_ctx, w_ada, b_ada, norm_g, final_g, hy_w_in, hy_b_in, hy_conv_w, hy_conv_b, hy_f_w1, hy_f_b1, hy_f_w2, hy_f_b2, hy_f_w3, hy_f_b3, hy_f_freq, hy_f_wout, hy_f_skip, hy_w_out, hy_b_out, ret_w_qkvg, ret_decay, ret_gn_g, ret_w_o, mlp_w1, mlp_w2):
    n_prompt, prompt_len, d = x_prompt.shape
    n_sample, sample_len, _ = x_sample.shape
    heads, dk, dv = state_ret.shape[3:]
    depth = w_ada.shape[0]
    assert depth == 2 and hy_w_in.shape[0] == 1 and ret_w_qkvg.shape[0] == 1
    assert n_sample + 1 <= MOD_ROWS

    cond = jnp.concatenate(
        [c_ctx[None, :], c, jnp.zeros((MOD_ROWS - 1 - n_sample, d), F32)], axis=0)
    mod = _ada_mod(cond, w_ada, b_ada)
    mod3 = mod.reshape(depth * MOD_ROWS, 1, mod.shape[-1])

    w_in, w_out, w_qkvg = hy_w_in[0], hy_w_out[0], ret_w_qkvg[0]
    w_o = ret_w_o[0].astype(BF16)
    w1, w2 = mlp_w1.astype(BF16), mlp_w2.astype(BF16)

    def run(grp, x, s0, latent):
        L = grp.seq_len
        proj = _hyena_in(grp, x, norm_g[0, 0], mod3, 0, w_in, hy_b_in[0], hy_conv_w[0],
                         hy_conv_b[0])
        fwd, inv = _dft_matrices(L)
        spectra = _filter_spectra(L, d, fwd, hy_f_w1[0], hy_f_b1[0], hy_f_w2[0], hy_f_b2[0],
                                  hy_f_w3[0], hy_f_b3[0], hy_f_freq[0], hy_f_wout[0])
        z = _longconv(grp, proj, spectra, hy_f_skip[0], fwd, inv)
        x = _proj_res(grp, z, w_out, hy_b_out[0], x, mod3, 0)
        x = _mlp(grp, x, norm_g[0, 1], mod3, 0, w1, w2)
        qkvg = _qkvg(grp, x, norm_g[1, 0], mod3, 1, w_qkvg, dk, rotary=latent)
        outs = _retention(grp, qkvg, ret_decay[0], ret_gn_g[0], s0, heads, dk, dv,
                          emit_state=s0 is None)
        x = _proj_res(grp, outs[0], w_o, None, x, mod3, 1)
        y = _mlp(grp, x, norm_g[1, 1], mod3, 1, w1, w2, final_g=final_g)
        return y, (outs[1] if s0 is None else None)

    prompt = _Group(n_prompt, prompt_len, d, mod_row0=0, mod_row_step=0)
    sample = _Group(n_sample, sample_len, d, mod_row0=1, mod_row_step=1)
    y_prompt, new_state = run(prompt, x_prompt.reshape(-1, d), None, False)
    y_sample, _ = run(sample, x_sample.reshape(-1, d), state_ret, True)
    return (y_prompt.reshape(x_prompt.shape), y_sample.reshape(x_sample.shape),
            new_state.astype(x_prompt.dtype))
```

```python
import functools
import math

import numpy as np
import jax
import jax.numpy as jnp
from jax import lax
from jax.experimental import pallas as pl
from jax.experimental.pallas import tpu as pltpu

F32 = jnp.float32
BF16 = jnp.bfloat16

EPS = 1e-6
GRID_W = 64
ROPE_BASE = 10000.0
FILTER_BANDS = 16
DECAY_TARGET = 1e-2
MIN_DECAY = math.log(DECAY_TARGET) / 1.5
MAX_DECAY = math.log(DECAY_TARGET) / 0.3

V7X_VMEM_BYTES = 64 * 1024 * 1024
VMEM_LIMIT = V7X_VMEM_BYTES - 8 * 1024 * 1024

TOKEN_TILE = 1024
COL_TILE = 1024
SUB_COLS = 512
PROJ_WEIGHT_BUFFERS = 3
RES_COL_TILE = 1024
RES_WEIGHT_CACHE_BYTES = 8 * 1024 * 1024
FF_TILE = 1024
MLP_ROWS = 256
ADA_COL_TILE = 2048
CONV_TILE_ELEMS = 256 * 1024
CONV_SHORT_SEQ = 256
CONV_MIN_LANES = 512
FILT_CH_TILE = 1024
RET_CHUNK = 256
RET_STEP_ROWS = 2048
MOD_ROWS = 8


def _params(semantics):
    return pltpu.CompilerParams(dimension_semantics=semantics, vmem_limit_bytes=VMEM_LIMIT)


def _pipelined_call(kernel, *, grid, in_specs, out_specs, out_shape, direct_specs=(),
                    scratch_shapes=(), name):
    single = not isinstance(out_specs, (list, tuple))
    outs = [out_specs] if single else list(out_specs)
    n_direct, n_piped = len(direct_specs), len(in_specs) + len(outs)
    any_spec = pl.BlockSpec(memory_space=pl.ANY)

    def call(*refs):
        direct, piped, scratch = (refs[:n_direct], refs[n_direct:n_direct + n_piped],
                                  refs[n_direct + n_piped:])
        pltpu.emit_pipeline(functools.partial(kernel, *direct), grid=grid,
                            in_specs=list(in_specs), out_specs=outs)(*piped, scratches=scratch)

    return pl.pallas_call(
        call,
        in_specs=list(direct_specs) + [any_spec] * len(in_specs),
        out_specs=any_spec if single else [any_spec] * len(outs),
        out_shape=out_shape,
        scratch_shapes=list(scratch_shapes),
        compiler_params=pltpu.CompilerParams(vmem_limit_bytes=VMEM_LIMIT),
        name=name,
    )


def _bdot(a, b):
    return jnp.dot(a, b, preferred_element_type=F32)


def _fdot(a, b):
    return jnp.dot(a, b, preferred_element_type=F32, precision=lax.Precision.HIGHEST)


def _modulated_norm(x, g, scale, shift):
    ms = jnp.mean(x * x, axis=-1, keepdims=True)
    return (x * lax.rsqrt(ms + EPS) * g) * (1.0 + scale) + shift


def _normed_dot(fill, x_ref, g_ref, sc_ref, sh_ref, h_scr, w):
    if not fill:
        return _bdot(h_scr[...], w)
    parts = []
    for r in range(x_ref.shape[0] // MLP_ROWS):
        rows = slice(r * MLP_ROWS, (r + 1) * MLP_ROWS)
        h = _modulated_norm(x_ref[rows, :], g_ref[...], sc_ref[0], sh_ref[0]).astype(BF16)
        h_scr[rows, :] = h
        parts.append(_bdot(h, w))
    return jnp.concatenate(parts, axis=0)


class _Group:
    def __init__(self, n_seq, seq_len, d, mod_row0, mod_row_step):
        assert seq_len & (seq_len - 1) == 0
        assert (n_seq * seq_len) % TOKEN_TILE == 0
        assert TOKEN_TILE % seq_len == 0 or (mod_row_step == 0 and seq_len % TOKEN_TILE == 0)
        assert mod_row_step == 0 or seq_len == TOKEN_TILE
        self.n_seq, self.seq_len, self.d = n_seq, seq_len, d
        self.rows = n_seq * seq_len
        self.tiles = self.rows // TOKEN_TILE
        self.mod_row0, self.mod_row_step = mod_row0, mod_row_step

    def mod_spec(self, layer, chunk, width, tiled):
        per = self.d // width
        base = layer * MOD_ROWS + self.mod_row0
        step = self.mod_row_step

        def index(i, j):
            return (base + step * i, 0, chunk * per + (j if tiled else 0))

        return pl.BlockSpec((1, 1, width), index)


def _ada_kernel(cond_ref, w_ref, b_ref, o_ref):
    c = cond_ref[...]
    a = (c * jax.nn.sigmoid(c)).astype(BF16)
    o_ref[0] = _bdot(a, w_ref[0].astype(BF16)) + b_ref[0]


def _ada_mod(cond, w_ada, b_ada):
    depth, d, n = w_ada.shape
    tn = ADA_COL_TILE
    return pl.pallas_call(
        _ada_kernel,
        grid=(depth, n // tn),
        in_specs=[
            pl.BlockSpec((MOD_ROWS, d), lambda l, j: (0, 0)),
            pl.BlockSpec((1, d, tn), lambda l, j: (l, 0, j)),
            pl.BlockSpec((1, 1, tn), lambda l, j: (l, 0, j)),
        ],
        out_specs=pl.BlockSpec((1, MOD_ROWS, tn), lambda l, j: (l, 0, j)),
        out_shape=jax.ShapeDtypeStruct((depth, MOD_ROWS, n), F32),
        compiler_params=_params(("arbitrary", "arbitrary")),
        name="ada_mod",
    )(cond, w_ada, b_ada.reshape(depth, 1, n))


def _hyena_in_kernel(x_ref, g_ref, sc_ref, sh_ref, w_ref, b_ref, cw_ref, cb_ref, o_ref, h_scr,
                     *, seq_len):
    def project(fill):
        rows = h_scr.shape[0]
        pos = lax.broadcasted_iota(jnp.int32, (rows, SUB_COLS), 0) & (seq_len - 1)
        first = pos == 0
        last = pos == seq_len - 1
        for s in range(o_ref.shape[1] // SUB_COLS):
            cols = slice(s * SUB_COLS, (s + 1) * SUB_COLS)
            p = _normed_dot(fill and s == 0, x_ref, g_ref, sc_ref, sh_ref, h_scr,
                            w_ref[:, cols].astype(BF16)) + b_ref[:, cols]
            prev = jnp.where(first, 0.0, pltpu.roll(p, 1, 0))
            nxt = jnp.where(last, 0.0, pltpu.roll(p, rows - 1, 0))
            o_ref[:, cols] = (prev * cw_ref[0:1, cols] + p * cw_ref[1:2, cols]
                              + nxt * cw_ref[2:3, cols] + cb_ref[:, cols])

    j = pl.program_id(1)
    pl.when(j == 0)(lambda: project(True))
    pl.when(j > 0)(lambda: project(False))


def _hyena_in(grp, x, norm_g, mod3, layer, w, b, conv_w, conv_b):
    d = grp.d
    n = w.shape[1]
    return _pipelined_call(
        functools.partial(_hyena_in_kernel, seq_len=grp.seq_len),
        grid=(grp.tiles, n // COL_TILE),
        in_specs=[
            pl.BlockSpec((TOKEN_TILE, d), lambda i, j: (i, 0)),
            pl.BlockSpec((1, d), lambda i, j: (0, 0)),
            grp.mod_spec(layer, 1, d, False),
            grp.mod_spec(layer, 0, d, False),
            pl.BlockSpec((d, COL_TILE), lambda i, j: (0, j),
                     pipeline_mode=pl.Buffered(PROJ_WEIGHT_BUFFERS)),
            pl.BlockSpec((1, COL_TILE), lambda i, j: (0, j)),
            pl.BlockSpec((3, COL_TILE), lambda i, j: (0, j)),
            pl.BlockSpec((1, COL_TILE), lambda i, j: (0, j)),
        ],
        out_specs=pl.BlockSpec((TOKEN_TILE, COL_TILE), lambda i, j: (i, j)),
        out_shape=jax.ShapeDtypeStruct((grp.rows, n), F32),
        scratch_shapes=[pltpu.VMEM((TOKEN_TILE, d), BF16)],
        name=f"hyena_in_{grp.seq_len}",
    )(x, norm_g.reshape(1, d), mod3, mod3, w, b.reshape(1, n), conv_w, conv_b.reshape(1, n))


def _dft_matrices(L):
    f = np.arange(L, dtype=np.int64)[:, None]
    t = np.arange(L, dtype=np.int64)[None, :]
    ang = ((f * t) % (2 * L)).astype(np.float64) * (np.pi / L)
    cos, sin = np.cos(ang), np.sin(ang)
    nyq = np.where(np.arange(L) % 2 == 0, 1.0, -1.0)
    bot = sin.copy()
    bot[0, :] = nyq
    fwd = np.concatenate([cos, bot], axis=0)
    wgt = np.full((L,), 2.0)
    wgt[0] = 1.0
    inv_cos = (cos * wgt[:, None]).T / (2 * L)
    inv_sin = (-2.0 * sin).T / (2 * L)
    inv_sin[:, 0] = nyq / (2 * L)
    inv = np.concatenate([inv_cos, inv_sin], axis=1)
    return jnp.asarray(fwd, F32).astype(BF16), jnp.asarray(inv, F32).astype(BF16)


def _filter_features(L, width):
    pos = np.arange(L, dtype=np.float64)
    t = pos / L
    omega = 2.0 * math.pi * pos / L
    bands = np.linspace(1e-4, FILTER_BANDS - 1, FILTER_BANDS)
    ang = omega[:, None] * bands[None, :]
    feat = np.concatenate([t[:, None], np.cos(ang), -np.sin(ang)], axis=-1)
    feat = np.pad(feat, ((0, 0), (0, width - feat.shape[1])))
    return jnp.asarray(feat, F32), jnp.asarray(t[:, None], F32)


def _filter_kernel(feat_ref, w1_ref, b1_ref, w2_ref, b2_ref, w3_ref, b3_ref, freq_ref,
                   wof_ref, wob_ref, t_ref, delta_ref, fwd_ref,
                   kr_ref, kiz_ref, krn_ref, h_scr, *, L):
    first = jnp.logical_and(pl.program_id(0) == 0, pl.program_id(1) == 0)

    @pl.when(first)
    def _():
        h = jnp.sin(freq_ref[0:1, :] * (_fdot(feat_ref[...], w1_ref[...]) + b1_ref[...]))
        h = jnp.sin(freq_ref[1:2, :] * (_fdot(h, w2_ref[...]) + b2_ref[...]))
        h = jnp.sin(freq_ref[2:3, :] * (_fdot(h, w3_ref[...]) + b3_ref[...]))
        h_scr[...] = h

    h = h_scr[...].astype(BF16)
    window = jnp.exp(-t_ref[...] * delta_ref[...])
    hf = _bdot(h, wof_ref[...].astype(BF16)) * window
    hb = _bdot(h, wob_ref[...].astype(BF16)) * window
    row = lax.broadcasted_iota(jnp.int32, hf.shape, 0)
    hb = jnp.where(row == 0, 0.0, hb)
    tap_sum = hf + hb
    tap_dif = hb - hf
    kr = _bdot(fwd_ref[0:L, :], tap_sum.astype(BF16))
    ki = _bdot(fwd_ref[L:2 * L, :], tap_dif.astype(BF16))
    sign = jnp.where((row & 1) == 0, 1.0, -1.0)
    k_nyq = jnp.sum(tap_sum * sign, axis=0, keepdims=True)
    kr_ref[0] = kr
    kiz_ref[0] = jnp.where(row == 0, 0.0, ki)
    krn_ref[0] = jnp.where(row == 0, -k_nyq, kr)


def _filter_spectra(L, d, fwd, w1, b1, w2, b2, w3, b3, freq, wout):
    width = w2.shape[0]
    order = wout.shape[1] // (2 * d)
    feat, t = _filter_features(L, width)
    w1p = jnp.pad(w1, ((0, width - w1.shape[0]), (0, 0)))
    deltas = jnp.asarray(np.abs(np.linspace(MIN_DECAY, MAX_DECAY, d))[None, :], F32)
    ct = FILT_CH_TILE
    per = d // ct
    small = lambda shape: pl.BlockSpec(shape, lambda o, j: (0, 0))
    out_spec = pl.BlockSpec((1, L, ct), lambda o, j: (o, 0, j))
    out_shape = jax.ShapeDtypeStruct((order, L, d), F32)
    return pl.pallas_call(
        functools.partial(_filter_kernel, L=L),
        grid=(order, per),
        in_specs=[
            small((L, width)), small((width, width)), small((1, width)),
            small((width, width)), small((1, width)),
            small((width, width)), small((1, width)), small((3, width)),
            pl.BlockSpec((width, ct), lambda o, j: (0, (2 * o) * per + j)),
            pl.BlockSpec((width, ct), lambda o, j: (0, (2 * o + 1) * per + j)),
            small((L, 1)),
            pl.BlockSpec((1, ct), lambda o, j: (0, j)),
            small((2 * L, L)),
        ],
        out_specs=[out_spec, out_spec, out_spec],
        out_shape=[out_shape, out_shape, out_shape],
        scratch_shapes=[pltpu.VMEM((L, width), F32)],
        compiler_params=_params(("arbitrary", "arbitrary")),
        name=f"hyena_filter_{L}",
    )(feat, w1p, b1.reshape(1, width), w2, b2.reshape(1, width), w3, b3.reshape(1, width),
      freq, wout, wout, t, deltas, fwd)


def _longconv_kernel(v_ref, x1_ref, x2_ref, kr_ref, kiz_ref, krn_ref, skip_ref, fwd_ref, inv_ref,
                     o_ref, *, L, pack):
    def lanes(ref):
        return jnp.concatenate([ref[s * L:(s + 1) * L, :] for s in range(pack)], axis=1)

    def tiled(x):
        return jnp.concatenate([x] * pack, axis=1)

    def conv(u, o):
        spec = _bdot(fwd_ref[...], u.astype(BF16))
        a, b = spec[:L], spec[L:]
        kr, kiz, krn = tiled(kr_ref[o]), tiled(kiz_ref[o]), tiled(krn_ref[o])
        top = (a * kr + b * kiz).astype(BF16)
        bot = (a * kiz - b * krn).astype(BF16)
        return _bdot(inv_ref[:, 0:L], top) + _bdot(inv_ref[:, L:2 * L], bot)

    v = lanes(v_ref)
    z = lanes(x1_ref) * (conv(v, 0) + v * tiled(skip_ref[0:1, :]))
    z = lanes(x2_ref) * (conv(z, 1) + z * tiled(skip_ref[1:2, :]))
    ct = o_ref.shape[1]
    for s in range(pack):
        o_ref[s * L:(s + 1) * L, :] = z[:, s * ct:(s + 1) * ct].astype(BF16)


def _longconv(grp, proj, spectra, skip, fwd, inv):
    L, d = grp.seq_len, grp.d
    ct = min(d, (2 if L <= CONV_SHORT_SEQ else 1) * CONV_TILE_ELEMS // L)
    pack = max(1, CONV_MIN_LANES // ct)
    assert grp.n_seq % pack == 0
    per = d // ct
    kr, kiz, krn = spectra
    order = kr.shape[0]
    kspec = pl.BlockSpec((order, L, ct), lambda j, b: (0, 0, j))
    return _pipelined_call(
        functools.partial(_longconv_kernel, L=L, pack=pack),
        grid=(per, grp.n_seq // pack),
        in_specs=[
            pl.BlockSpec((pack * L, ct), lambda j, b: (b, j)),
            pl.BlockSpec((pack * L, ct), lambda j, b: (b, per + j)),
            pl.BlockSpec((pack * L, ct), lambda j, b: (b, 2 * per + j)),
            kspec, kspec, kspec,
            pl.BlockSpec((order, ct), lambda j, b: (0, j)),
            pl.BlockSpec((2 * L, L), lambda j, b: (0, 0)),
            pl.BlockSpec((L, 2 * L), lambda j, b: (0, 0)),
        ],
        out_specs=pl.BlockSpec((pack * L, ct), lambda j, b: (b, j)),
        out_shape=jax.ShapeDtypeStruct((grp.rows, d), BF16),
        name=f"hyena_longconv_{L}",
    )(proj, proj, proj, kr, kiz, krn, skip, fwd, inv)


def _proj_res_kernel(*refs, has_bias, cached):
    refs = list(refs)
    a_ref, w_ref = refs[:2]
    refs = refs[2:]
    b_ref = refs.pop(0) if has_bias else None
    x_ref, gate_ref, o_ref = refs[:3]
    if cached:
        wc_scr = refs[3]
        j = pl.program_id(1)

        @pl.when(pl.program_id(0) == 0)
        def _():
            wc_scr[j] = w_ref[...].astype(BF16)

        w = wc_scr[j]
    else:
        w = w_ref[...]
    m = _bdot(a_ref[...], w)
    if has_bias:
        m = m + b_ref[...]
    o_ref[...] = x_ref[...] + gate_ref[0] * m


def _proj_res(grp, a, w, b, x, mod3, layer):
    k, n = w.shape
    cached = w.dtype != BF16
    assert not cached or k * n * 2 <= RES_WEIGHT_CACHE_BYTES
    tn = RES_COL_TILE
    nj = n // tn
    if cached:
        w_index = lambda i, j: (0, jnp.where(i == 0, j, nj - 1))
    else:
        w_index = lambda i, j: (0, j)
    in_specs = [
        pl.BlockSpec((TOKEN_TILE, k), lambda i, j: (i, 0)),
        pl.BlockSpec((k, tn), w_index),
    ]
    args = [a, w]
    if b is not None:
        in_specs.append(pl.BlockSpec((1, tn), lambda i, j: (0, j)))
        args.append(b.reshape(1, n))
    in_specs += [
        pl.BlockSpec((TOKEN_TILE, tn), lambda i, j: (i, j)),
        grp.mod_spec(layer, 2, tn, True),
    ]
    args += [x, mod3]
    return _pipelined_call(
        functools.partial(_proj_res_kernel, has_bias=b is not None, cached=cached),
        grid=(grp.tiles, nj),
        in_specs=in_specs,
        out_specs=pl.BlockSpec((TOKEN_TILE, tn), lambda i, j: (i, j)),
        out_shape=jax.ShapeDtypeStruct((grp.rows, n), F32),
        scratch_shapes=[pltpu.VMEM((nj, k, tn), BF16)] if cached else [],
        name=f"proj_res_{k}_{grp.seq_len}",
    )(*args)


def _mlp_kernel(*refs, final_norm):
    if final_norm:
        (x_hbm, g_ref, sc_ref, sh_ref, gate_ref, w1_ref, w2_ref, fg_ref, o_ref,
         h_scr, x_buf, x_sem) = refs
    else:
        x_hbm, g_ref, sc_ref, sh_ref, gate_ref, w1_ref, w2_ref, o_ref, h_scr, x_buf, x_sem = refs
    i = pl.program_id(0)
    f = pl.program_id(1)
    last_f = pl.num_programs(1) - 1
    tile_rows = x_buf.shape[0]

    def x_copy(tile):
        start = pl.multiple_of(tile * tile_rows, tile_rows)
        return pltpu.make_async_copy(x_hbm.at[pl.ds(start, tile_rows), :], x_buf, x_sem)

    @pl.when(jnp.logical_and(i == 0, f == 0))
    def _():
        x_copy(0).start()

    @pl.when(f == 0)
    def _():
        x_copy(i).wait()

    @pl.when(jnp.logical_and(f == 1, i + 1 < pl.num_programs(0)))
    def _():
        x_copy(i + 1).start()

    def step(first, last):
        w1 = w1_ref[0]
        w2 = w2_ref[0]
        for r in range(tile_rows // MLP_ROWS):
            rows = slice(r * MLP_ROWS, (r + 1) * MLP_ROWS)
            if first:
                x = x_buf[rows, :]
                h = _modulated_norm(x, g_ref[...], sc_ref[0], sh_ref[0]).astype(BF16)
                h_scr[rows, :] = h
            else:
                h = h_scr[rows, :]
            u = jnp.square(jnp.maximum(_bdot(h, w1), 0.0)).astype(BF16)
            acc = (x if first else o_ref[rows, :]) + gate_ref[0] * _bdot(u, w2)
            if last and final_norm:
                ms = jnp.mean(acc * acc, axis=-1, keepdims=True)
                acc = acc * lax.rsqrt(ms + EPS) * fg_ref[...]
            o_ref[rows, :] = acc

    pl.when(f == 0)(lambda: step(True, False))
    pl.when(jnp.logical_and(f > 0, f < last_f))(lambda: step(False, False))
    pl.when(f == last_f)(lambda: step(False, True))


def _mlp(grp, x, norm_g, mod3, layer, w1, w2, final_g=None):
    d = grp.d
    ff = w1.shape[2]
    nf = ff // FF_TILE
    assert nf >= 2
    block_specs = [
        pl.BlockSpec((1, d), lambda i, f: (0, 0)),
        grp.mod_spec(layer, 4, d, False),
        grp.mod_spec(layer, 3, d, False),
        grp.mod_spec(layer, 5, d, False),
        pl.BlockSpec((1, d, FF_TILE), lambda i, f: (layer, 0, f)),
        pl.BlockSpec((1, FF_TILE, d), lambda i, f: (layer, f, 0)),
    ]
    args = [x, norm_g.reshape(1, d), mod3, mod3, mod3, w1, w2]
    if final_g is not None:
        block_specs.append(pl.BlockSpec((1, d), lambda i, f: (0, 0)))
        args.append(final_g.reshape(1, d))
    return _pipelined_call(
        functools.partial(_mlp_kernel, final_norm=final_g is not None),
        grid=(grp.tiles, nf),
        direct_specs=[pl.BlockSpec(memory_space=pl.ANY)],
        in_specs=block_specs,
        out_specs=pl.BlockSpec((TOKEN_TILE, d), lambda i, f: (i, 0)),
        out_shape=jax.ShapeDtypeStruct((grp.rows, d), F32),
        scratch_shapes=[pltpu.VMEM((TOKEN_TILE, d), BF16),
                        pltpu.VMEM((TOKEN_TILE, d), F32),
                        pltpu.SemaphoreType.DMA(())],
        name=f"mlp_{layer}_{grp.seq_len}",
    )(*args)


def _rotary_tables(L, dk, width):
    n_rows = L // GRID_W
    rows = np.repeat(np.arange(n_rows, dtype=np.float64), GRID_W)
    cols = np.tile(np.arange(GRID_W, dtype=np.float64), n_rows)
    n_pairs = dk // 4
    inv = ROPE_BASE ** (-np.arange(n_pairs, dtype=np.float64) / n_pairs)
    ang = np.concatenate([rows[:, None] * inv, cols[:, None] * inv], axis=-1)
    cos = np.repeat(np.cos(ang), 2, axis=-1)
    sin = np.repeat(np.sin(ang), 2, axis=-1)
    sin[:, 0::2] *= -1.0
    reps = width // dk
    return (jnp.asarray(np.tile(cos, (1, reps)), F32), jnp.asarray(np.tile(sin, (1, reps)), F32))


def _qkvg_kernel(*refs, rotary, qk_tiles, k_tile0, k_scale):
    if rotary:
        x_ref, g_ref, sc_ref, sh_ref, w_ref, cos_ref, sin_ref, o_ref, h_scr = refs
    else:
        x_ref, g_ref, sc_ref, sh_ref, w_ref, o_ref, h_scr = refs
    j = pl.program_id(1)
    is_qk = j < qk_tiles
    scale = jnp.where(jnp.logical_and(j >= k_tile0, is_qk), k_scale, 1.0)

    def project(fill, rotate):
        if rotate:
            lane = lax.broadcasted_iota(jnp.int32, (h_scr.shape[0], SUB_COLS), 1)
            even = (lane & 1) == 0
        for s in range(o_ref.shape[1] // SUB_COLS):
            cols = slice(s * SUB_COLS, (s + 1) * SUB_COLS)
            acc = _normed_dot(fill and s == 0, x_ref, g_ref, sc_ref, sh_ref, h_scr,
                              w_ref[:, cols].astype(BF16))
            if rotate:
                partner = jnp.where(even, pltpu.roll(acc, SUB_COLS - 1, 1), pltpu.roll(acc, 1, 1))
                acc = acc * cos_ref[...] + partner * sin_ref[...]
            o_ref[:, cols] = (acc * scale).astype(BF16)

    pl.when(j == 0)(lambda: project(True, rotary))
    if rotary:
        pl.when(jnp.logical_and(j > 0, is_qk))(lambda: project(False, True))
        pl.when(jnp.logical_not(is_qk))(lambda: project(False, False))
    else:
        pl.when(j > 0)(lambda: project(False, False))


def _qkvg(grp, x, norm_g, mod3, layer, w, dk, rotary):
    d = grp.d
    n = w.shape[1]
    in_specs = [
        pl.BlockSpec((TOKEN_TILE, d), lambda i, j: (i, 0)),
        pl.BlockSpec((1, d), lambda i, j: (0, 0)),
        grp.mod_spec(layer, 1, d, False),
        grp.mod_spec(layer, 0, d, False),
        pl.BlockSpec((d, COL_TILE), lambda i, j: (0, j),
                     pipeline_mode=pl.Buffered(PROJ_WEIGHT_BUFFERS)),
    ]
    args = [x, norm_g.reshape(1, d), mod3, mod3, w]
    if rotary:
        assert grp.seq_len == TOKEN_TILE
        cos, sin = _rotary_tables(grp.seq_len, dk, SUB_COLS)
        table = pl.BlockSpec((TOKEN_TILE, SUB_COLS), lambda i, j: (0, 0))
        in_specs += [table, table]
        args += [cos, sin]
    kern = functools.partial(_qkvg_kernel, rotary=rotary, qk_tiles=2 * d // COL_TILE,
                             k_tile0=d // COL_TILE, k_scale=dk ** -0.5)
    return _pipelined_call(
        kern,
        grid=(grp.tiles, n // COL_TILE),
        in_specs=in_specs,
        out_specs=pl.BlockSpec((TOKEN_TILE, COL_TILE), lambda i, j: (i, j)),
        out_shape=jax.ShapeDtypeStruct((grp.rows, n), BF16),
        scratch_shapes=[pltpu.VMEM((TOKEN_TILE, d), BF16)],
        name=f"ret_qkvg_{grp.seq_len}",
    )(*args)


def _log_sigmoid(x):
    return jnp.minimum(x, 0.0) - jnp.log1p(jnp.exp(-jnp.abs(x)))


def _retention_kernel(*refs, L, C, pack, has_s0, emit_state):
    refs = list(refs)
    decay_ref, q_ref, k_ref, v_ref, g_ref, gn_ref = refs[:6]
    refs = refs[6:]
    s0_ref = refs.pop(0) if has_s0 else None
    y_ref = refs.pop(0)
    so_ref = refs.pop(0) if emit_state else None
    mask_scr, dec_scr, cdec_scr, sb_scr = refs

    head = pl.program_id(0)
    dk = q_ref.shape[1]
    dv = v_ref.shape[1]
    nc = L // C

    @pl.when(pl.program_id(1) == 0)
    def _():
        def log_decay(direction, width):
            return _log_sigmoid(jnp.full((1, width), decay_ref[direction, head], F32))

        ri = lax.broadcasted_iota(jnp.int32, (C, C), 0)
        ci = lax.broadcasted_iota(jnp.int32, (C, C), 1)
        diff = (ri - ci).astype(F32)
        lower = diff >= 0
        upper = diff <= 0
        mask_scr[...] = (
            jnp.where(lower, jnp.exp(jnp.where(lower, diff, 0.0) * log_decay(0, C)), 0.0)
            + jnp.where(upper, jnp.exp(jnp.where(upper, -diff, 0.0) * log_decay(1, C)), 0.0))
        rk = lax.broadcasted_iota(jnp.int32, (C, dk), 0).astype(F32)
        dec_scr[0] = jnp.exp((rk + 1.0) * log_decay(0, dk))
        dec_scr[1] = jnp.exp((C - 1.0 - rk) * log_decay(0, dk))
        dec_scr[2] = jnp.exp((C - rk) * log_decay(1, dk))
        dec_scr[3] = jnp.exp(rk * log_decay(1, dk))
        cdec_scr[0:1, :] = jnp.exp(C * log_decay(0, dv))
        cdec_scr[1:2, :] = jnp.exp(C * log_decay(1, dv))

    gn = gn_ref[...]

    def advance(state, dec_row, upd):
        return upd if state is None else state * cdec_scr[dec_row:dec_row + 1, :] + upd

    for s in range(pack):
        base = s * L

        def kv(n, which):
            rows = slice(base + n * C, base + (n + 1) * C)
            kd = (k_ref[rows, :].astype(F32) * dec_scr[which]).astype(BF16)
            return lax.dot_general(kd, v_ref[rows, :], (((0,), (0,)), ((), ())),
                                   preferred_element_type=F32)

        state_b = s0_ref[s, 0, 1, 0] if has_s0 else None
        has_b = [False] * nc
        for n in reversed(range(nc)):
            if state_b is not None:
                sb_scr[n] = state_b.astype(BF16)
                has_b[n] = True
            if n > 0 or emit_state:
                state_b = advance(state_b, 1, kv(n, 3))
        if emit_state:
            so_ref[s, 0, 1, 0] = state_b

        state_f = s0_ref[s, 0, 0, 0] if has_s0 else None
        for n in range(nc):
            rows = slice(base + n * C, base + (n + 1) * C)
            q = q_ref[rows, :]
            scores = lax.dot_general(q, k_ref[rows, :], (((1,), (1,)), ((), ())),
                                     preferred_element_type=F32)
            y = _bdot((scores * mask_scr[...]).astype(BF16), v_ref[rows, :])
            qf = q.astype(F32)
            if state_f is not None:
                y = y + _bdot((qf * dec_scr[0]).astype(BF16), state_f.astype(BF16))
            if has_b[n]:
                y = y + _bdot((qf * dec_scr[2]).astype(BF16), sb_scr[n])
            mu = jnp.mean(y, axis=-1, keepdims=True)
            yc = y - mu
            var = jnp.mean(yc * yc, axis=-1, keepdims=True)
            yn = (yc * lax.rsqrt(var + EPS)) * gn
            gate = g_ref[rows, :].astype(F32)
            y_ref[rows, :] = ((gate * jax.nn.sigmoid(gate)) * yn).astype(BF16)
            if n < nc - 1 or emit_state:
                state_f = advance(state_f, 0, kv(n, 1))
        if emit_state:
            so_ref[s, 0, 0, 0] = state_f


def _retention(grp, qkvg, decay, gn_g, s0, heads, dk, dv, emit_state):
    L = grp.seq_len
    C = min(L, RET_CHUNK)
    pack = max(1, min(grp.n_seq, RET_STEP_ROWS // L))
    assert grp.n_seq % pack == 0
    kq = heads
    kv0 = 2 * heads * dk // dv
    kg0 = kv0 + heads
    rows = pack * L
    in_specs = [
        pl.BlockSpec((rows, dk), lambda h, b: (b, h)),
        pl.BlockSpec((rows, dk), lambda h, b: (b, kq + h)),
        pl.BlockSpec((rows, dv), lambda h, b: (b, kv0 + h)),
        pl.BlockSpec((rows, dv), lambda h, b: (b, kg0 + h)),
        pl.BlockSpec((1, dv), lambda h, b: (0, h)),
    ]
    args = [decay, qkvg, qkvg, qkvg, qkvg, gn_g.reshape(1, heads * dv)]
    state_block = (pack, 1, 2, 1, dk, dv)
    state_index = lambda h, b: (b, 0, 0, h, 0, 0)
    if s0 is not None:
        in_specs.append(pl.BlockSpec(state_block, state_index))
        args.append(s0)
    out_specs = [pl.BlockSpec((rows, dv), lambda h, b: (b, h))]
    out_shape = [jax.ShapeDtypeStruct((grp.rows, heads * dv), BF16)]
    if emit_state:
        out_specs.append(pl.BlockSpec(state_block, state_index))
        out_shape.append(jax.ShapeDtypeStruct((grp.n_seq, 1, 2, heads, dk, dv), F32))
    kern = functools.partial(_retention_kernel, L=L, C=C, pack=pack, has_s0=s0 is not None,
                             emit_state=emit_state)
    return _pipelined_call(
        kern,
        grid=(heads, grp.n_seq // pack),
        direct_specs=[pl.BlockSpec(memory_space=pltpu.SMEM)],
        in_specs=in_specs,
        out_specs=out_specs,
        out_shape=out_shape,
        scratch_shapes=[
            pltpu.VMEM((C, C), F32),
            pltpu.VMEM((4, C, dk), F32),
            pltpu.VMEM((8, dv), F32),
            pltpu.VMEM((L // C, dk, dv), BF16),
        ],
        name=f"retention_{L}",
    )(*args)


def kernel(x_prompt, x_sample, state_ret, c, c_ctx, w_ada, b_ada, norm_g, final_g, hy_w_in, hy_b_in, hy_conv_w, hy_conv_b, hy_f_w1, hy_f_b1, hy_f_w2, hy_f_b2, hy_f_w3, hy_f_b3, hy_f_freq, hy_f_wout, hy_f_skip, hy_w_out, hy_b_out, ret_w_qkvg, ret_decay, ret_gn_g, ret_w_o, mlp_w1, mlp_w2):
    n_prompt, prompt_len, d = x_prompt.shape
    n_sample, sample_len, _ = x_sample.shape
    heads, dk, dv = state_ret.shape[3:]
    depth = w_ada.shape[0]
    assert depth == 2 and hy_w_in.shape[0] == 1 and ret_w_qkvg.shape[0] == 1
    assert n_sample + 1 <= MOD_ROWS

    cond = jnp.concatenate(
        [c_ctx[None, :], c, jnp.zeros((MOD_ROWS - 1 - n_sample, d), F32)], axis=0)
    mod = _ada_mod(cond, w_ada, b_ada)
    mod3 = mod.reshape(depth * MOD_ROWS, 1, mod.shape[-1])

    w_in, w_out, w_qkvg = hy_w_in[0], hy_w_out[0], ret_w_qkvg[0]
    w_o = ret_w_o[0].astype(BF16)
    w1, w2 = mlp_w1.astype(BF16), mlp_w2.astype(BF16)

    def run(grp, x, s0, latent):
        L = grp.seq_len
        proj = _hyena_in(grp, x, norm_g[0, 0], mod3, 0, w_in, hy_b_in[0], hy_conv_w[0],
                         hy_conv_b[0])
        fwd, inv = _dft_matrices(L)
        spectra = _filter_spectra(L, d, fwd, hy_f_w1[0], hy_f_b1[0], hy_f_w2[0], hy_f_b2[0],
                                  hy_f_w3[0], hy_f_b3[0], hy_f_freq[0], hy_f_wout[0])
        z = _longconv(grp, proj, spectra, hy_f_skip[0], fwd, inv)
        x = _proj_res(grp, z, w_out, hy_b_out[0], x, mod3, 0)
        x = _mlp(grp, x, norm_g[0, 1], mod3, 0, w1, w2)
        qkvg = _qkvg(grp, x, norm_g[1, 0], mod3, 1, w_qkvg, dk, rotary=latent)
        outs = _retention(grp, qkvg, ret_decay[0], ret_gn_g[0], s0, heads, dk, dv,
                          emit_state=s0 is None)
        x = _proj_res(grp, outs[0], w_o, None, x, mod3, 1)
        y = _mlp(grp, x, norm_g[1, 1], mod3, 1, w1, w2, final_g=final_g)
        return y, (outs[1] if s0 is None else None)

    prompt = _Group(n_prompt, prompt_len, d, mod_row0=0, mod_row_step=0)
    sample = _Group(n_sample, sample_len, d, mod_row0=1, mod_row_step=1)
    y_prompt, new_state = run(prompt, x_prompt.reshape(-1, d), None, False)
    y_sample, _ = run(sample, x_sample.reshape(-1, d), state_ret, True)
    return (y_prompt.reshape(x_prompt.shape), y_sample.reshape(x_sample.shape),
            new_state.astype(x_prompt.dtype))
```

```python
import functools
import math

import numpy as np
import jax
import jax.numpy as jnp
from jax import lax
from jax.experimental import pallas as pl
from jax.experimental.pallas import tpu as pltpu

F32 = jnp.float32
BF16 = jnp.bfloat16

EPS = 1e-6
GRID_W = 64
ROPE_BASE = 10000.0
FILTER_BANDS = 16
DECAY_TARGET = 1e-2
MIN_DECAY = math.log(DECAY_TARGET) / 1.5
MAX_DECAY = math.log(DECAY_TARGET) / 0.3

V7X_VMEM_BYTES = 64 * 1024 * 1024
VMEM_LIMIT = V7X_VMEM_BYTES - 8 * 1024 * 1024

TOKEN_TILE = 1024
COL_TILE = 1024
SUB_COLS = 512
PROJ_WEIGHT_BUFFERS = 3
RES_COL_TILE = 1024
RES_WEIGHT_CACHE_BYTES = 8 * 1024 * 1024
FF_TILE = 1024
MLP_WEIGHT_BUFFERS = 3
MLP_ROWS = 256
ADA_COL_TILE = 2048
CONV_TILE_ELEMS = 256 * 1024
CONV_SHORT_SEQ = 256
CONV_MIN_LANES = 512
FILT_CH_TILE = 1024
RET_CHUNK = 256
RET_STEP_ROWS = 2048
MOD_ROWS = 8


def _params(semantics):
    return pltpu.CompilerParams(dimension_semantics=semantics, vmem_limit_bytes=VMEM_LIMIT)


def _pipelined_call(kernel, *, grid, in_specs, out_specs, out_shape, direct_specs=(),
                    scratch_shapes=(), name):
    single = not isinstance(out_specs, (list, tuple))
    outs = [out_specs] if single else list(out_specs)
    n_direct, n_piped = len(direct_specs), len(in_specs) + len(outs)
    any_spec = pl.BlockSpec(memory_space=pl.ANY)

    def call(*refs):
        direct, piped, scratch = (refs[:n_direct], refs[n_direct:n_direct + n_piped],
                                  refs[n_direct + n_piped:])
        pltpu.emit_pipeline(functools.partial(kernel, *direct), grid=grid,
                            in_specs=list(in_specs), out_specs=outs)(*piped, scratches=scratch)

    return pl.pallas_call(
        call,
        in_specs=list(direct_specs) + [any_spec] * len(in_specs),
        out_specs=any_spec if single else [any_spec] * len(outs),
        out_shape=out_shape,
        scratch_shapes=list(scratch_shapes),
        compiler_params=pltpu.CompilerParams(vmem_limit_bytes=VMEM_LIMIT),
        name=name,
    )


def _bdot(a, b):
    return jnp.dot(a, b, preferred_element_type=F32)


def _fdot(a, b):
    return jnp.dot(a, b, preferred_element_type=F32, precision=lax.Precision.HIGHEST)


def _modulated_norm(x, g, scale, shift):
    ms = jnp.mean(x * x, axis=-1, keepdims=True)
    return (x * lax.rsqrt(ms + EPS) * g) * (1.0 + scale) + shift


def _normed_dot(fill, x_ref, g_ref, sc_ref, sh_ref, h_scr, w):
    if not fill:
        return _bdot(h_scr[...], w)
    parts = []
    for r in range(x_ref.shape[0] // MLP_ROWS):
        rows = slice(r * MLP_ROWS, (r + 1) * MLP_ROWS)
        h = _modulated_norm(x_ref[rows, :], g_ref[...], sc_ref[0], sh_ref[0]).astype(BF16)
        h_scr[rows, :] = h
        parts.append(_bdot(h, w))
    return jnp.concatenate(parts, axis=0)


class _Group:
    def __init__(self, n_seq, seq_len, d, mod_row0, mod_row_step):
        assert seq_len & (seq_len - 1) == 0
        assert (n_seq * seq_len) % TOKEN_TILE == 0
        assert TOKEN_TILE % seq_len == 0 or (mod_row_step == 0 and seq_len % TOKEN_TILE == 0)
        assert mod_row_step == 0 or seq_len == TOKEN_TILE
        self.n_seq, self.seq_len, self.d = n_seq, seq_len, d
        self.rows = n_seq * seq_len
        self.tiles = self.rows // TOKEN_TILE
        self.mod_row0, self.mod_row_step = mod_row0, mod_row_step

    def mod_spec(self, layer, chunk, width, tiled):
        per = self.d // width
        base = layer * MOD_ROWS + self.mod_row0
        step = self.mod_row_step

        def index(i, j):
            return (base + step * i, 0, chunk * per + (j if tiled else 0))

        return pl.BlockSpec((1, 1, width), index)


def _ada_kernel(cond_ref, w_ref, b_ref, o_ref):
    c = cond_ref[...]
    a = (c * jax.nn.sigmoid(c)).astype(BF16)
    o_ref[0] = _bdot(a, w_ref[0].astype(BF16)) + b_ref[0]


def _ada_mod(cond, w_ada, b_ada):
    depth, d, n = w_ada.shape
    tn = ADA_COL_TILE
    return pl.pallas_call(
        _ada_kernel,
        grid=(depth, n // tn),
        in_specs=[
            pl.BlockSpec((MOD_ROWS, d), lambda l, j: (0, 0)),
            pl.BlockSpec((1, d, tn), lambda l, j: (l, 0, j)),
            pl.BlockSpec((1, 1, tn), lambda l, j: (l, 0, j)),
        ],
        out_specs=pl.BlockSpec((1, MOD_ROWS, tn), lambda l, j: (l, 0, j)),
        out_shape=jax.ShapeDtypeStruct((depth, MOD_ROWS, n), F32),
        compiler_params=_params(("arbitrary", "arbitrary")),
        name="ada_mod",
    )(cond, w_ada, b_ada.reshape(depth, 1, n))


def _hyena_in_kernel(x_ref, g_ref, sc_ref, sh_ref, w_ref, b_ref, cw_ref, cb_ref, o_ref, h_scr,
                     *, seq_len):
    def project(fill):
        rows = h_scr.shape[0]
        pos = lax.broadcasted_iota(jnp.int32, (rows, SUB_COLS), 0) & (seq_len - 1)
        first = pos == 0
        last = pos == seq_len - 1
        for s in range(o_ref.shape[1] // SUB_COLS):
            cols = slice(s * SUB_COLS, (s + 1) * SUB_COLS)
            p = _normed_dot(fill and s == 0, x_ref, g_ref, sc_ref, sh_ref, h_scr,
                            w_ref[:, cols].astype(BF16)) + b_ref[:, cols]
            prev = jnp.where(first, 0.0, pltpu.roll(p, 1, 0))
            nxt = jnp.where(last, 0.0, pltpu.roll(p, rows - 1, 0))
            o_ref[:, cols] = (prev * cw_ref[0:1, cols] + p * cw_ref[1:2, cols]
                              + nxt * cw_ref[2:3, cols] + cb_ref[:, cols])

    j = pl.program_id(1)
    pl.when(j == 0)(lambda: project(True))
    pl.when(j > 0)(lambda: project(False))


def _hyena_in(grp, x, norm_g, mod3, layer, w, b, conv_w, conv_b):
    d = grp.d
    n = w.shape[1]
    return _pipelined_call(
        functools.partial(_hyena_in_kernel, seq_len=grp.seq_len),
        grid=(grp.tiles, n // COL_TILE),
        in_specs=[
            pl.BlockSpec((TOKEN_TILE, d), lambda i, j: (i, 0)),
            pl.BlockSpec((1, d), lambda i, j: (0, 0)),
            grp.mod_spec(layer, 1, d, False),
            grp.mod_spec(layer, 0, d, False),
            pl.BlockSpec((d, COL_TILE), lambda i, j: (0, j),
                     pipeline_mode=pl.Buffered(PROJ_WEIGHT_BUFFERS)),
            pl.BlockSpec((1, COL_TILE), lambda i, j: (0, j)),
            pl.BlockSpec((3, COL_TILE), lambda i, j: (0, j)),
            pl.BlockSpec((1, COL_TILE), lambda i, j: (0, j)),
        ],
        out_specs=pl.BlockSpec((TOKEN_TILE, COL_TILE), lambda i, j: (i, j)),
        out_shape=jax.ShapeDtypeStruct((grp.rows, n), F32),
        scratch_shapes=[pltpu.VMEM((TOKEN_TILE, d), BF16)],
        name=f"hyena_in_{grp.seq_len}",
    )(x, norm_g.reshape(1, d), mod3, mod3, w, b.reshape(1, n), conv_w, conv_b.reshape(1, n))


def _dft_matrices(L):
    f = np.arange(L, dtype=np.int64)[:, None]
    t = np.arange(L, dtype=np.int64)[None, :]
    ang = ((f * t) % (2 * L)).astype(np.float64) * (np.pi / L)
    cos, sin = np.cos(ang), np.sin(ang)
    nyq = np.where(np.arange(L) % 2 == 0, 1.0, -1.0)
    bot = sin.copy()
    bot[0, :] = nyq
    fwd = np.concatenate([cos, bot], axis=0)
    wgt = np.full((L,), 2.0)
    wgt[0] = 1.0
    inv_cos = (cos * wgt[:, None]).T / (2 * L)
    inv_sin = (-2.0 * sin).T / (2 * L)
    inv_sin[:, 0] = nyq / (2 * L)
    inv = np.concatenate([inv_cos, inv_sin], axis=1)
    return jnp.asarray(fwd, F32).astype(BF16), jnp.asarray(inv, F32).astype(BF16)


def _filter_features(L, width):
    pos = np.arange(L, dtype=np.float64)
    t = pos / L
    omega = 2.0 * math.pi * pos / L
    bands = np.linspace(1e-4, FILTER_BANDS - 1, FILTER_BANDS)
    ang = omega[:, None] * bands[None, :]
    feat = np.concatenate([t[:, None], np.cos(ang), -np.sin(ang)], axis=-1)
    feat = np.pad(feat, ((0, 0), (0, width - feat.shape[1])))
    return jnp.asarray(feat, F32), jnp.asarray(t[:, None], F32)


def _filter_kernel(feat_ref, w1_ref, b1_ref, w2_ref, b2_ref, w3_ref, b3_ref, freq_ref,
                   wof_ref, wob_ref, t_ref, delta_ref, fwd_ref,
                   kr_ref, kiz_ref, krn_ref, h_scr, *, L):
    first = jnp.logical_and(pl.program_id(0) == 0, pl.program_id(1) == 0)

    @pl.when(first)
    def _():
        h = jnp.sin(freq_ref[0:1, :] * (_fdot(feat_ref[...], w1_ref[...]) + b1_ref[...]))
        h = jnp.sin(freq_ref[1:2, :] * (_fdot(h, w2_ref[...]) + b2_ref[...]))
        h = jnp.sin(freq_ref[2:3, :] * (_fdot(h, w3_ref[...]) + b3_ref[...]))
        h_scr[...] = h

    h = h_scr[...].astype(BF16)
    window = jnp.exp(-t_ref[...] * delta_ref[...])
    hf = _bdot(h, wof_ref[...].astype(BF16)) * window
    hb = _bdot(h, wob_ref[...].astype(BF16)) * window
    row = lax.broadcasted_iota(jnp.int32, hf.shape, 0)
    hb = jnp.where(row == 0, 0.0, hb)
    tap_sum = hf + hb
    tap_dif = hb - hf
    kr = _bdot(fwd_ref[0:L, :], tap_sum.astype(BF16))
    ki = _bdot(fwd_ref[L:2 * L, :], tap_dif.astype(BF16))
    sign = jnp.where((row & 1) == 0, 1.0, -1.0)
    k_nyq = jnp.sum(tap_sum * sign, axis=0, keepdims=True)
    kr_ref[0] = kr
    kiz_ref[0] = jnp.where(row == 0, 0.0, ki)
    krn_ref[0] = jnp.where(row == 0, -k_nyq, kr)


def _filter_spectra(L, d, fwd, w1, b1, w2, b2, w3, b3, freq, wout):
    width = w2.shape[0]
    order = wout.shape[1] // (2 * d)
    feat, t = _filter_features(L, width)
    w1p = jnp.pad(w1, ((0, width - w1.shape[0]), (0, 0)))
    deltas = jnp.asarray(np.abs(np.linspace(MIN_DECAY, MAX_DECAY, d))[None, :], F32)
    ct = FILT_CH_TILE
    per = d // ct
    small = lambda shape: pl.BlockSpec(shape, lambda o, j: (0, 0))
    out_spec = pl.BlockSpec((1, L, ct), lambda o, j: (o, 0, j))
    out_shape = jax.ShapeDtypeStruct((order, L, d), F32)
    return pl.pallas_call(
        functools.partial(_filter_kernel, L=L),
        grid=(order, per),
        in_specs=[
            small((L, width)), small((width, width)), small((1, width)),
            small((width, width)), small((1, width)),
            small((width, width)), small((1, width)), small((3, width)),
            pl.BlockSpec((width, ct), lambda o, j: (0, (2 * o) * per + j)),
            pl.BlockSpec((width, ct), lambda o, j: (0, (2 * o + 1) * per + j)),
            small((L, 1)),
            pl.BlockSpec((1, ct), lambda o, j: (0, j)),
            small((2 * L, L)),
        ],
        out_specs=[out_spec, out_spec, out_spec],
        out_shape=[out_shape, out_shape, out_shape],
        scratch_shapes=[pltpu.VMEM((L, width), F32)],
        compiler_params=_params(("arbitrary", "arbitrary")),
        name=f"hyena_filter_{L}",
    )(feat, w1p, b1.reshape(1, width), w2, b2.reshape(1, width), w3, b3.reshape(1, width),
      freq, wout, wout, t, deltas, fwd)


def _longconv_kernel(v_ref, x1_ref, x2_ref, kr_ref, kiz_ref, krn_ref, skip_ref, fwd_ref, inv_ref,
                     o_ref, *, L, pack):
    def lanes(ref):
        return jnp.concatenate([ref[s * L:(s + 1) * L, :] for s in range(pack)], axis=1)

    def tiled(x):
        return jnp.concatenate([x] * pack, axis=1)

    def conv(u, o):
        spec = _bdot(fwd_ref[...], u.astype(BF16))
        a, b = spec[:L], spec[L:]
        kr, kiz, krn = tiled(kr_ref[o]), tiled(kiz_ref[o]), tiled(krn_ref[o])
        top = (a * kr + b * kiz).astype(BF16)
        bot = (a * kiz - b * krn).astype(BF16)
        return _bdot(inv_ref[:, 0:L], top) + _bdot(inv_ref[:, L:2 * L], bot)

    v = lanes(v_ref)
    z = lanes(x1_ref) * (conv(v, 0) + v * tiled(skip_ref[0:1, :]))
    z = lanes(x2_ref) * (conv(z, 1) + z * tiled(skip_ref[1:2, :]))
    ct = o_ref.shape[1]
    for s in range(pack):
        o_ref[s * L:(s + 1) * L, :] = z[:, s * ct:(s + 1) * ct].astype(BF16)


def _longconv(grp, proj, spectra, skip, fwd, inv):
    L, d = grp.seq_len, grp.d
    ct = min(d, (2 if L <= CONV_SHORT_SEQ else 1) * CONV_TILE_ELEMS // L)
    pack = max(1, CONV_MIN_LANES // ct)
    assert grp.n_seq % pack == 0
    per = d // ct
    kr, kiz, krn = spectra
    order = kr.shape[0]
    kspec = pl.BlockSpec((order, L, ct), lambda j, b: (0, 0, j))
    return _pipelined_call(
        functools.partial(_longconv_kernel, L=L, pack=pack),
        grid=(per, grp.n_seq // pack),
        in_specs=[
            pl.BlockSpec((pack * L, ct), lambda j, b: (b, j)),
            pl.BlockSpec((pack * L, ct), lambda j, b: (b, per + j)),
            pl.BlockSpec((pack * L, ct), lambda j, b: (b, 2 * per + j)),
            kspec, kspec, kspec,
            pl.BlockSpec((order, ct), lambda j, b: (0, j)),
            pl.BlockSpec((2 * L, L), lambda j, b: (0, 0)),
            pl.BlockSpec((L, 2 * L), lambda j, b: (0, 0)),
        ],
        out_specs=pl.BlockSpec((pack * L, ct), lambda j, b: (b, j)),
        out_shape=jax.ShapeDtypeStruct((grp.rows, d), BF16),
        name=f"hyena_longconv_{L}",
    )(proj, proj, proj, kr, kiz, krn, skip, fwd, inv)


def _proj_res_kernel(*refs, has_bias, cached):
    refs = list(refs)
    a_ref, w_ref = refs[:2]
    refs = refs[2:]
    b_ref = refs.pop(0) if has_bias else None
    x_ref, gate_ref, o_ref = refs[:3]
    if cached:
        wc_scr = refs[3]
        j = pl.program_id(1)

        @pl.when(pl.program_id(0) == 0)
        def _():
            wc_scr[j] = w_ref[...].astype(BF16)

        w = wc_scr[j]
    else:
        w = w_ref[...]
    m = _bdot(a_ref[...], w)
    if has_bias:
        m = m + b_ref[...]
    o_ref[...] = x_ref[...] + gate_ref[0] * m


def _proj_res(grp, a, w, b, x, mod3, layer):
    k, n = w.shape
    cached = w.dtype != BF16
    assert not cached or k * n * 2 <= RES_WEIGHT_CACHE_BYTES
    tn = RES_COL_TILE
    nj = n // tn
    if cached:
        w_index = lambda i, j: (0, jnp.where(i == 0, j, nj - 1))
    else:
        w_index = lambda i, j: (0, j)
    in_specs = [
        pl.BlockSpec((TOKEN_TILE, k), lambda i, j: (i, 0)),
        pl.BlockSpec((k, tn), w_index),
    ]
    args = [a, w]
    if b is not None:
        in_specs.append(pl.BlockSpec((1, tn), lambda i, j: (0, j)))
        args.append(b.reshape(1, n))
    in_specs += [
        pl.BlockSpec((TOKEN_TILE, tn), lambda i, j: (i, j)),
        grp.mod_spec(layer, 2, tn, True),
    ]
    args += [x, mod3]
    return _pipelined_call(
        functools.partial(_proj_res_kernel, has_bias=b is not None, cached=cached),
        grid=(grp.tiles, nj),
        in_specs=in_specs,
        out_specs=pl.BlockSpec((TOKEN_TILE, tn), lambda i, j: (i, j)),
        out_shape=jax.ShapeDtypeStruct((grp.rows, n), F32),
        scratch_shapes=[pltpu.VMEM((nj, k, tn), BF16)] if cached else [],
        name=f"proj_res_{k}_{grp.seq_len}",
    )(*args)


def _mlp_kernel(*refs, final_norm):
    if final_norm:
        (x_hbm, g_ref, sc_ref, sh_ref, gate_ref, w1_ref, w2_ref, fg_ref, o_ref,
         h_scr, x_buf, x_sem) = refs
    else:
        x_hbm, g_ref, sc_ref, sh_ref, gate_ref, w1_ref, w2_ref, o_ref, h_scr, x_buf, x_sem = refs
    i = pl.program_id(0)
    f = pl.program_id(1)
    last_f = pl.num_programs(1) - 1
    tile_rows = x_buf.shape[0]

    def x_copy(tile):
        start = pl.multiple_of(tile * tile_rows, tile_rows)
        return pltpu.make_async_copy(x_hbm.at[pl.ds(start, tile_rows), :], x_buf, x_sem)

    @pl.when(jnp.logical_and(i == 0, f == 0))
    def _():
        x_copy(0).start()

    @pl.when(f == 0)
    def _():
        x_copy(i).wait()

    @pl.when(jnp.logical_and(f == 1, i + 1 < pl.num_programs(0)))
    def _():
        x_copy(i + 1).start()

    def step(first, last):
        w1 = w1_ref[0]
        w2 = w2_ref[0]
        for r in range(tile_rows // MLP_ROWS):
            rows = slice(r * MLP_ROWS, (r + 1) * MLP_ROWS)
            if first:
                x = x_buf[rows, :]
                h = _modulated_norm(x, g_ref[...], sc_ref[0], sh_ref[0]).astype(BF16)
                h_scr[rows, :] = h
            else:
                h = h_scr[rows, :]
            u = jnp.square(jnp.maximum(_bdot(h, w1), 0.0)).astype(BF16)
            acc = (x if first else o_ref[rows, :]) + gate_ref[0] * _bdot(u, w2)
            if last and final_norm:
                ms = jnp.mean(acc * acc, axis=-1, keepdims=True)
                acc = acc * lax.rsqrt(ms + EPS) * fg_ref[...]
            o_ref[rows, :] = acc

    pl.when(f == 0)(lambda: step(True, False))
    pl.when(jnp.logical_and(f > 0, f < last_f))(lambda: step(False, False))
    pl.when(f == last_f)(lambda: step(False, True))


def _mlp(grp, x, norm_g, mod3, layer, w1, w2, final_g=None):
    d = grp.d
    ff = w1.shape[2]
    nf = ff // FF_TILE
    assert nf >= 2
    block_specs = [
        pl.BlockSpec((1, d), lambda i, f: (0, 0)),
        grp.mod_spec(layer, 4, d, False),
        grp.mod_spec(layer, 3, d, False),
        grp.mod_spec(layer, 5, d, False),
        pl.BlockSpec((1, d, FF_TILE), lambda i, f: (layer, 0, f),
                     pipeline_mode=pl.Buffered(MLP_WEIGHT_BUFFERS)),
        pl.BlockSpec((1, FF_TILE, d), lambda i, f: (layer, f, 0),
                     pipeline_mode=pl.Buffered(MLP_WEIGHT_BUFFERS)),
    ]
    args = [x, norm_g.reshape(1, d), mod3, mod3, mod3, w1, w2]
    if final_g is not None:
        block_specs.append(pl.BlockSpec((1, d), lambda i, f: (0, 0)))
        args.append(final_g.reshape(1, d))
    return _pipelined_call(
        functools.partial(_mlp_kernel, final_norm=final_g is not None),
        grid=(grp.tiles, nf),
        direct_specs=[pl.BlockSpec(memory_space=pl.ANY)],
        in_specs=block_specs,
        out_specs=pl.BlockSpec((TOKEN_TILE, d), lambda i, f: (i, 0)),
        out_shape=jax.ShapeDtypeStruct((grp.rows, d), F32),
        scratch_shapes=[pltpu.VMEM((TOKEN_TILE, d), BF16),
                        pltpu.VMEM((TOKEN_TILE, d), F32),
                        pltpu.SemaphoreType.DMA(())],
        name=f"mlp_{layer}_{grp.seq_len}",
    )(*args)


def _rotary_tables(L, dk, width):
    n_rows = L // GRID_W
    rows = np.repeat(np.arange(n_rows, dtype=np.float64), GRID_W)
    cols = np.tile(np.arange(GRID_W, dtype=np.float64), n_rows)
    n_pairs = dk // 4
    inv = ROPE_BASE ** (-np.arange(n_pairs, dtype=np.float64) / n_pairs)
    ang = np.concatenate([rows[:, None] * inv, cols[:, None] * inv], axis=-1)
    cos = np.repeat(np.cos(ang), 2, axis=-1)
    sin = np.repeat(np.sin(ang), 2, axis=-1)
    sin[:, 0::2] *= -1.0
    reps = width // dk
    return (jnp.asarray(np.tile(cos, (1, reps)), F32), jnp.asarray(np.tile(sin, (1, reps)), F32))


def _qkvg_kernel(*refs, rotary, qk_tiles, k_tile0, k_scale):
    if rotary:
        x_ref, g_ref, sc_ref, sh_ref, w_ref, cos_ref, sin_ref, o_ref, h_scr = refs
    else:
        x_ref, g_ref, sc_ref, sh_ref, w_ref, o_ref, h_scr = refs
    j = pl.program_id(1)
    is_qk = j < qk_tiles
    scale = jnp.where(jnp.logical_and(j >= k_tile0, is_qk), k_scale, 1.0)

    def project(fill, rotate):
        if rotate:
            lane = lax.broadcasted_iota(jnp.int32, (h_scr.shape[0], SUB_COLS), 1)
            even = (lane & 1) == 0
        for s in range(o_ref.shape[1] // SUB_COLS):
            cols = slice(s * SUB_COLS, (s + 1) * SUB_COLS)
            acc = _normed_dot(fill and s == 0, x_ref, g_ref, sc_ref, sh_ref, h_scr,
                              w_ref[:, cols].astype(BF16))
            if rotate:
                partner = jnp.where(even, pltpu.roll(acc, SUB_COLS - 1, 1), pltpu.roll(acc, 1, 1))
                acc = acc * cos_ref[...] + partner * sin_ref[...]
            o_ref[:, cols] = (acc * scale).astype(BF16)

    pl.when(j == 0)(lambda: project(True, rotary))
    if rotary:
        pl.when(jnp.logical_and(j > 0, is_qk))(lambda: project(False, True))
        pl.when(jnp.logical_not(is_qk))(lambda: project(False, False))
    else:
        pl.when(j > 0)(lambda: project(False, False))


def _qkvg(grp, x, norm_g, mod3, layer, w, dk, rotary):
    d = grp.d
    n = w.shape[1]
    in_specs = [
        pl.BlockSpec((TOKEN_TILE, d), lambda i, j: (i, 0)),
        pl.BlockSpec((1, d), lambda i, j: (0, 0)),
        grp.mod_spec(layer, 1, d, False),
        grp.mod_spec(layer, 0, d, False),
        pl.BlockSpec((d, COL_TILE), lambda i, j: (0, j),
                     pipeline_mode=pl.Buffered(PROJ_WEIGHT_BUFFERS)),
    ]
    args = [x, norm_g.reshape(1, d), mod3, mod3, w]
    if rotary:
        assert grp.seq_len == TOKEN_TILE
        cos, sin = _rotary_tables(grp.seq_len, dk, SUB_COLS)
        table = pl.BlockSpec((TOKEN_TILE, SUB_COLS), lambda i, j: (0, 0))
        in_specs += [table, table]
        args += [cos, sin]
    kern = functools.partial(_qkvg_kernel, rotary=rotary, qk_tiles=2 * d // COL_TILE,
                             k_tile0=d // COL_TILE, k_scale=dk ** -0.5)
    return _pipelined_call(
        kern,
        grid=(grp.tiles, n // COL_TILE),
        in_specs=in_specs,
        out_specs=pl.BlockSpec((TOKEN_TILE, COL_TILE), lambda i, j: (i, j)),
        out_shape=jax.ShapeDtypeStruct((grp.rows, n), BF16),
        scratch_shapes=[pltpu.VMEM((TOKEN_TILE, d), BF16)],
        name=f"ret_qkvg_{grp.seq_len}",
    )(*args)


def _log_sigmoid(x):
    return jnp.minimum(x, 0.0) - jnp.log1p(jnp.exp(-jnp.abs(x)))


def _retention_kernel(*refs, L, C, pack, has_s0, emit_state):
    refs = list(refs)
    decay_ref, q_ref, k_ref, v_ref, g_ref, gn_ref = refs[:6]
    refs = refs[6:]
    s0_ref = refs.pop(0) if has_s0 else None
    y_ref = refs.pop(0)
    so_ref = refs.pop(0) if emit_state else None
    mask_scr, dec_scr, cdec_scr, sb_scr = refs

    head = pl.program_id(0)
    dk = q_ref.shape[1]
    dv = v_ref.shape[1]
    nc = L // C

    @pl.when(pl.program_id(1) == 0)
    def _():
        def log_decay(direction, width):
            return _log_sigmoid(jnp.full((1, width), decay_ref[direction, head], F32))

        ri = lax.broadcasted_iota(jnp.int32, (C, C), 0)
        ci = lax.broadcasted_iota(jnp.int32, (C, C), 1)
        diff = (ri - ci).astype(F32)
        lower = diff >= 0
        upper = diff <= 0
        mask_scr[...] = (
            jnp.where(lower, jnp.exp(jnp.where(lower, diff, 0.0) * log_decay(0, C)), 0.0)
            + jnp.where(upper, jnp.exp(jnp.where(upper, -diff, 0.0) * log_decay(1, C)), 0.0))
        rk = lax.broadcasted_iota(jnp.int32, (C, dk), 0).astype(F32)
        dec_scr[0] = jnp.exp((rk + 1.0) * log_decay(0, dk))
        dec_scr[1] = jnp.exp((C - 1.0 - rk) * log_decay(0, dk))
        dec_scr[2] = jnp.exp((C - rk) * log_decay(1, dk))
        dec_scr[3] = jnp.exp(rk * log_decay(1, dk))
        cdec_scr[0:1, :] = jnp.exp(C * log_decay(0, dv))
        cdec_scr[1:2, :] = jnp.exp(C * log_decay(1, dv))

    gn = gn_ref[...]

    def advance(state, dec_row, upd):
        return upd if state is None else state * cdec_scr[dec_row:dec_row + 1, :] + upd

    for s in range(pack):
        base = s * L

        def kv(n, which):
            rows = slice(base + n * C, base + (n + 1) * C)
            kd = (k_ref[rows, :].astype(F32) * dec_scr[which]).astype(BF16)
            return lax.dot_general(kd, v_ref[rows, :], (((0,), (0,)), ((), ())),
                                   preferred_element_type=F32)

        state_b = s0_ref[s, 0, 1, 0] if has_s0 else None
        has_b = [False] * nc
        for n in reversed(range(nc)):
            if state_b is not None:
                sb_scr[n] = state_b.astype(BF16)
                has_b[n] = True
            if n > 0 or emit_state:
                state_b = advance(state_b, 1, kv(n, 3))
        if emit_state:
            so_ref[s, 0, 1, 0] = state_b

        state_f = s0_ref[s, 0, 0, 0] if has_s0 else None
        for n in range(nc):
            rows = slice(base + n * C, base + (n + 1) * C)
            q = q_ref[rows, :]
            scores = lax.dot_general(q, k_ref[rows, :], (((1,), (1,)), ((), ())),
                                     preferred_element_type=F32)
            y = _bdot((scores * mask_scr[...]).astype(BF16), v_ref[rows, :])
            qf = q.astype(F32)
            if state_f is not None:
                y = y + _bdot((qf * dec_scr[0]).astype(BF16), state_f.astype(BF16))
            if has_b[n]:
                y = y + _bdot((qf * dec_scr[2]).astype(BF16), sb_scr[n])
            mu = jnp.mean(y, axis=-1, keepdims=True)
            yc = y - mu
            var = jnp.mean(yc * yc, axis=-1, keepdims=True)
            yn = (yc * lax.rsqrt(var + EPS)) * gn
            gate = g_ref[rows, :].astype(F32)
            y_ref[rows, :] = ((gate * jax.nn.sigmoid(gate)) * yn).astype(BF16)
            if n < nc - 1 or emit_state:
                state_f = advance(state_f, 0, kv(n, 1))
        if emit_state:
            so_ref[s, 0, 0, 0] = state_f


def _retention(grp, qkvg, decay, gn_g, s0, heads, dk, dv, emit_state):
    L = grp.seq_len
    C = min(L, RET_CHUNK)
    pack = max(1, min(grp.n_seq, RET_STEP_ROWS // L))
    assert grp.n_seq % pack == 0
    kq = heads
    kv0 = 2 * heads * dk // dv
    kg0 = kv0 + heads
    rows = pack * L
    in_specs = [
        pl.BlockSpec((rows, dk), lambda h, b: (b, h)),
        pl.BlockSpec((rows, dk), lambda h, b: (b, kq + h)),
        pl.BlockSpec((rows, dv), lambda h, b: (b, kv0 + h)),
        pl.BlockSpec((rows, dv), lambda h, b: (b, kg0 + h)),
        pl.BlockSpec((1, dv), lambda h, b: (0, h)),
    ]
    args = [decay, qkvg, qkvg, qkvg, qkvg, gn_g.reshape(1, heads * dv)]
    state_block = (pack, 1, 2, 1, dk, dv)
    state_index = lambda h, b: (b, 0, 0, h, 0, 0)
    if s0 is not None:
        in_specs.append(pl.BlockSpec(state_block, state_index))
        args.append(s0)
    out_specs = [pl.BlockSpec((rows, dv), lambda h, b: (b, h))]
    out_shape = [jax.ShapeDtypeStruct((grp.rows, heads * dv), BF16)]
    if emit_state:
        out_specs.append(pl.BlockSpec(state_block, state_index))
        out_shape.append(jax.ShapeDtypeStruct((grp.n_seq, 1, 2, heads, dk, dv), F32))
    kern = functools.partial(_retention_kernel, L=L, C=C, pack=pack, has_s0=s0 is not None,
                             emit_state=emit_state)
    return _pipelined_call(
        kern,
        grid=(heads, grp.n_seq // pack),
        direct_specs=[pl.BlockSpec(memory_space=pltpu.SMEM)],
        in_specs=in_specs,
        out_specs=out_specs,
        out_shape=out_shape,
        scratch_shapes=[
            pltpu.VMEM((C, C), F32),
            pltpu.VMEM((4, C, dk), F32),
            pltpu.VMEM((8, dv), F32),
            pltpu.VMEM((L // C, dk, dv), BF16),
        ],
        name=f"retention_{L}",
    )(*args)


def kernel(x_prompt, x_sample, state_ret, c, c_ctx, w_ada, b_ada, norm_g, final_g, hy_w_in, hy_b_in, hy_conv_w, hy_conv_b, hy_f_w1, hy_f_b1, hy_f_w2, hy_f_b2, hy_f_w3, hy_f_b3, hy_f_freq, hy_f_wout, hy_f_skip, hy_w_out, hy_b_out, ret_w_qkvg, ret_decay, ret_gn_g, ret_w_o, mlp_w1, mlp_w2):
    n_prompt, prompt_len, d = x_prompt.shape
    n_sample, sample_len, _ = x_sample.shape
    heads, dk, dv = state_ret.shape[3:]
    depth = w_ada.shape[0]
    assert depth == 2 and hy_w_in.shape[0] == 1 and ret_w_qkvg.shape[0] == 1
    assert n_sample + 1 <= MOD_ROWS

    cond = jnp.concatenate(
        [c_ctx[None, :], c, jnp.zeros((MOD_ROWS - 1 - n_sample, d), F32)], axis=0)
    mod = _ada_mod(cond, w_ada, b_ada)
    mod3 = mod.reshape(depth * MOD_ROWS, 1, mod.shape[-1])

    w_in, w_out, w_qkvg = hy_w_in[0], hy_w_out[0], ret_w_qkvg[0]
    w_o = ret_w_o[0].astype(BF16)
    w1, w2 = mlp_w1.astype(BF16), mlp_w2.astype(BF16)

    def run(grp, x, s0, latent):
        L = grp.seq_len
        proj = _hyena_in(grp, x, norm_g[0, 0], mod3, 0, w_in, hy_b_in[0], hy_conv_w[0],
                         hy_conv_b[0])
        fwd, inv = _dft_matrices(L)
        spectra = _filter_spectra(L, d, fwd, hy_f_w1[0], hy_f_b1[0], hy_f_w2[0], hy_f_b2[0],
                                  hy_f_w3[0], hy_f_b3[0], hy_f_freq[0], hy_f_wout[0])
        z = _longconv(grp, proj, spectra, hy_f_skip[0], fwd, inv)
        x = _proj_res(grp, z, w_out, hy_b_out[0], x, mod3, 0)
        x = _mlp(grp, x, norm_g[0, 1], mod3, 0, w1, w2)
        qkvg = _qkvg(grp, x, norm_g[1, 0], mod3, 1, w_qkvg, dk, rotary=latent)
        outs = _retention(grp, qkvg, ret_decay[0], ret_gn_g[0], s0, heads, dk, dv,
                          emit_state=s0 is None)
        x = _proj_res(grp, outs[0], w_o, None, x, mod3, 1)
        y = _mlp(grp, x, norm_g[1, 1], mod3, 1, w1, w2, final_g=final_g)
        return y, (outs[1] if s0 is None else None)

    prompt = _Group(n_prompt, prompt_len, d, mod_row0=0, mod_row_step=0)
    sample = _Group(n_sample, sample_len, d, mod_row0=1, mod_row_step=1)
    y_prompt, new_state = run(prompt, x_prompt.reshape(-1, d), None, False)
    y_sample, _ = run(sample, x_sample.reshape(-1, d), state_ret, True)
    return (y_prompt.reshape(x_prompt.shape), y_sample.reshape(x_sample.shape),
            new_state.astype(x_prompt.dtype))
```
